```python
import math
import jax, jax.numpy as jnp
from jax import lax
import numpy as np

D_MODEL = 1024
BATCH = 8
SEQ = 4096
DEPTH = 1

GRID_W = 64
CTX_LEN = 256
EPS = 1e-6
S5_WIDTH = 512
S5_GROUP = 16
S5_GROUPS = S5_WIDTH // S5_GROUP
S5_STATE = 64
S5_DT_MIN = 1e-3
S5_DT_MAX = 1e-1
S5_MAX_RE = -1e-4
DA_HEADS = 4
DA_HEAD_DIM = 64
DA_V_DIM = 2 * DA_HEAD_DIM
DA_QK_WIDTH = DA_HEADS * 2 * DA_HEAD_DIM
DA_WIDTH = DA_HEADS * DA_V_DIM
Q_BLOCK = 128
ROPE_THETA = 10000.0
ROPE_PAIRS = DA_HEAD_DIM // 4
IN_SPLITS = (S5_WIDTH, S5_WIDTH + DA_QK_WIDTH, S5_WIDTH + 2 * DA_QK_WIDTH,
             S5_WIDTH + 2 * DA_QK_WIDTH + DA_WIDTH,
             S5_WIDTH + 2 * DA_QK_WIDTH + DA_WIDTH + D_MODEL)
IN_COLS = S5_WIDTH + 2 * DA_QK_WIDTH + DA_WIDTH + 2 * D_MODEL
N_EXPERTS = 16
EC_CAPACITY = 2
D_EXPERT = 2048

kernel_name = 'hybrid_s5_diffattn_ecmoe_dit_layer'


def rms_norm(x, w):
    xf = x.astype(jnp.float32)
    y = xf * lax.rsqrt(jnp.mean(xf * xf, axis=-1, keepdims=True) + EPS)
    return (y * w.astype(jnp.float32)).astype(x.dtype)


def modulate(h, shift, scale):
    return h * (1.0 + scale) + shift


def split_in(z):
    b, l = z.shape[:2]
    u, q, k, v, ga, gb = jnp.split(z, IN_SPLITS, axis=-1)
    q = q.reshape(b, l, DA_HEADS, 2, DA_HEAD_DIM)
    k = k.reshape(b, l, DA_HEADS, 2, DA_HEAD_DIM)
    v = v.reshape(b, l, DA_HEADS, DA_V_DIM)
    return u, q, k, v, ga, gb


def rope_1d(x, ang):
    x1, x2 = jnp.split(x, 2, axis=-1)
    cos = jnp.cos(ang)[:, None, None, :].astype(x.dtype)
    sin = jnp.sin(ang)[:, None, None, :].astype(x.dtype)
    return jnp.concatenate([x1 * cos - x2 * sin, x2 * cos + x1 * sin], axis=-1)


def rope_2d(x, row_ang, col_ang):
    half = x.shape[-1] // 2
    return jnp.concatenate([rope_1d(x[..., :half], row_ang), rope_1d(x[..., half:], col_ang)], axis=-1)


def _scan_combine(e1, e2):
    a1r, a1i, b1r, b1i = e1
    a2r, a2i, b2r, b2i = e2
    return (a2r * a1r - a2i * a1i, a2r * a1i + a2i * a1r,
            a2r * b1r - a2i * b1i + b2r, a2r * b1i + a2i * b1r + b2i)


def s5_discretise(lam_re, lam_im, log_dt, b_re, b_im):
    lam_re = jnp.minimum(lam_re.astype(jnp.float32), S5_MAX_RE)
    lam_im = lam_im.astype(jnp.float32)
    dt = jnp.exp(log_dt.astype(jnp.float32))[:, None]
    mag = jnp.exp(lam_re * dt)
    lb_re = mag * jnp.cos(lam_im * dt)
    lb_im = mag * jnp.sin(lam_im * dt)
    den = lam_re * lam_re + lam_im * lam_im
    num_re = lb_re - 1.0
    co_re = (num_re * lam_re + lb_im * lam_im) / den
    co_im = (lb_im * lam_re - num_re * lam_im) / den
    br = b_re.astype(jnp.float32)
    bi = b_im.astype(jnp.float32)
    bb_re = co_re[..., None] * br - co_im[..., None] * bi
    bb_im = co_re[..., None] * bi + co_im[..., None] * br
    return lb_re, lb_im, bb_re, bb_im


def s5_scan(u, lb_re, lb_im, bb_re, bb_im, h0, reverse):
    bu_re = jnp.einsum('blgh,gph->blgp', u, bb_re)
    bu_im = jnp.einsum('blgh,gph->blgp', u, bb_im)
    if h0 is not None:
        h0_re, h0_im = h0
        edge = -1 if reverse else 0
        bu_re = bu_re.at[:, edge].add(lb_re * h0_re - lb_im * h0_im)
        bu_im = bu_im.at[:, edge].add(lb_re * h0_im + lb_im * h0_re)
    a_re = jnp.broadcast_to(lb_re, bu_re.shape)
    a_im = jnp.broadcast_to(lb_im, bu_im.shape)
    _, _, h_re, h_im = lax.associative_scan(_scan_combine, (a_re, a_im, bu_re, bu_im),
                                            reverse=reverse, axis=1)
    return h_re, h_im


def s5_readout(h_re, h_im, c_re, c_im):
    return (jnp.einsum('blgp,ghp->blgh', h_re, c_re.astype(jnp.float32))
            - jnp.einsum('blgp,ghp->blgh', h_im, c_im.astype(jnp.float32)))


def s5_output(y, u, d_skip, w_glu, dtype):
    b, l = y.shape[:2]
    y = y + d_skip.astype(jnp.float32).reshape(S5_GROUPS, S5_GROUP) * u
    y = jax.nn.gelu(y.reshape(b, l, S5_WIDTH).astype(dtype))
    return y * jax.nn.sigmoid(y @ w_glu)


def s5_mixer(u_ctx, u_lat, p, need_ctx_out):
    b = u_lat.shape[0]
    uc = u_ctx.astype(jnp.float32).reshape(b, u_ctx.shape[1], S5_GROUPS, S5_GROUP)
    ul = u_lat.astype(jnp.float32).reshape(b, u_lat.shape[1], S5_GROUPS, S5_GROUP)
    y_lat = jnp.zeros_like(ul)
    y_ctx = jnp.zeros_like(uc)
    for d, reverse in ((0, False), (1, True)):
        disc = s5_discretise(p['s5_lam_re'][d], p['s5_lam_im'][d], p['s5_log_dt'][d],
                             p['s5_b_re'][d], p['s5_b_im'][d])
        hc_re, hc_im = s5_scan(uc, *disc, None, reverse)
        edge = 0 if reverse else -1
        hl_re, hl_im = s5_scan(ul, *disc, (hc_re[:, edge], hc_im[:, edge]), reverse)
        y_lat = y_lat + s5_readout(hl_re, hl_im, p['s5_c_re'][d], p['s5_c_im'][d])
        if need_ctx_out:
            y_ctx = y_ctx + s5_readout(hc_re, hc_im, p['s5_c_re'][d], p['s5_c_im'][d])
    out_lat = s5_output(y_lat, ul, p['s5_d'], p['w_glu'], u_lat.dtype)
    out_ctx = s5_output(y_ctx, uc, p['s5_d'], p['w_glu'], u_ctx.dtype) if need_ctx_out else None
    return out_lat, out_ctx


def diff_attention(q, k, v, lam, subln_w, lam_init):
    b, lq = q.shape[:2]
    nb = lq // Q_BLOCK
    qb = jnp.moveaxis(q.reshape(b, nb, Q_BLOCK, DA_HEADS, 2, DA_HEAD_DIM), 1, 0)
    scale = DA_HEAD_DIM ** -0.5

    def one_block(qi):
        s = jnp.einsum('bqhcd,bkhcd->bchqk', qi, k).astype(jnp.float32) * scale
        pr = jax.nn.softmax(s, axis=-1)
        pd = pr[:, 0] - lam * pr[:, 1]
        return jnp.einsum('bhqk,bkhd->bqhd', pd.astype(v.dtype), v)

    o = lax.map(one_block, qb)
    o = jnp.moveaxis(o, 0, 1).reshape(b, lq, DA_HEADS, DA_V_DIM)
    o = rms_norm(o, subln_w) * (1.0 - lam_init)
    return o.reshape(b, lq, DA_WIDTH)


def merge_branches(ya, yb, ga, gb, p):
    m = jax.nn.sigmoid(ga) * (ya @ p['w_proj_a']) + jax.nn.sigmoid(gb) * (yb @ p['w_proj_b'])
    return m @ p['w_out']


def ec_moe(h, w_router, w_gate, w_up, w_down):
    b, l, d = h.shape
    cap = EC_CAPACITY * l // N_EXPERTS
    aff = jax.nn.softmax((h @ w_router).astype(jnp.float32), axis=-1)
    gates, idx = lax.top_k(jnp.swapaxes(aff, 1, 2), cap)
    xs = jax.vmap(lambda hb, ib: hb[ib])(h, idx)
    hid = jax.nn.silu(jnp.einsum('becd,edf->becf', xs, w_gate)) * jnp.einsum('becd,edf->becf', xs, w_up)
    ys = jnp.einsum('becf,efd->becd', hid, w_down) * gates[..., None].astype(h.dtype)
    return jax.vmap(lambda yb, ib: jnp.zeros((l, d), yb.dtype).at[ib.reshape(-1)].add(yb.reshape(-1, d)))(ys, idx)


def hybrid_layer(x, xc, mod_lat, mod_ctx, p, row_ang, col_ang, lam_init, need_ctx_out):
    sh_m, sc_m, g_m, sh_f, sc_f, g_f = jnp.split(mod_lat, 6, axis=-1)
    csh_m, csc_m, cg_m, csh_f, csc_f, cg_f = jnp.split(mod_ctx, 6, axis=-1)
    h = modulate(rms_norm(x, p['norm_pre_mix']), sh_m, sc_m)
    hc = modulate(rms_norm(xc, p['norm_pre_mix']), csh_m, csc_m)
    u, q, k, v, ga, gb = split_in(h @ p['w_in'])
    uc, qc, kc, vc, gac, gbc = split_in(hc @ p['w_in'])
    ya, yac = s5_mixer(uc, u, p, need_ctx_out)
    q = rope_2d(q, row_ang, col_ang)
    k = rope_2d(k, row_ang, col_ang)
    lq1, lk1, lq2, lk2 = p['da_lambda'].astype(jnp.float32)
    lam = jnp.exp(jnp.sum(lq1 * lk1)) - jnp.exp(jnp.sum(lq2 * lk2)) + lam_init
    yb = diff_attention(q, jnp.concatenate([kc, k], axis=1), jnp.concatenate([vc, v], axis=1),
                        lam, p['da_subln'], lam_init)
    x = x + g_m * rms_norm(merge_branches(ya, yb, ga, gb, p), p['norm_post_mix'])
    h = modulate(rms_norm(x, p['norm_pre_ffn']), sh_f, sc_f)
    f = ec_moe(h, p['w_router'], p['w_exp_gate'], p['w_exp_up'], p['w_exp_down'])
    x = x + g_f * rms_norm(f, p['norm_post_ffn'])
    if need_ctx_out:
        ybc = diff_attention(qc, kc, vc, lam, p['da_subln'], lam_init)
        xc = xc + cg_m * rms_norm(merge_branches(yac, ybc, gac, gbc, p), p['norm_post_mix'])
        hc = modulate(rms_norm(xc, p['norm_pre_ffn']), csh_f, csc_f)
        fc = ec_moe(hc, p['w_router'], p['w_exp_gate'], p['w_exp_up'], p['w_exp_down'])
        xc = xc + cg_f * rms_norm(fc, p['norm_post_ffn'])
    return x, xc


def setup_inputs(seed: int = 0) -> dict:
    key = jax.random.key(seed)
    ks = jax.random.split(key, 32)

    def nrm(k, shape, scale):
        return jax.random.normal(k, shape, jnp.float32) * scale

    G, P, H = S5_GROUPS, S5_STATE, S5_GROUP
    lam_im0 = jnp.pi * jnp.arange(P, dtype=jnp.float32)
    return {
        'x': nrm(ks[0], (BATCH, SEQ, D_MODEL), 1.0),
        'c': nrm(ks[1], (BATCH, D_MODEL), 1.0),
        'ctx': nrm(ks[2], (BATCH, CTX_LEN, D_MODEL), 1.0),
        'c_ctx': nrm(ks[3], (D_MODEL,), 1.0),
        'w_ada': nrm(ks[4], (DEPTH, D_MODEL, 6 * D_MODEL), 0.5 * D_MODEL ** -0.5),
        'b_ada': nrm(ks[5], (DEPTH, 6 * D_MODEL), 0.01),
        'norm_pre_mix': 1.0 + nrm(ks[6], (DEPTH, D_MODEL), 0.05),
        'norm_post_mix': 1.0 + nrm(ks[7], (DEPTH, D_MODEL), 0.05),
        'norm_pre_ffn': 1.0 + nrm(ks[8], (DEPTH, D_MODEL), 0.05),
        'norm_post_ffn': 1.0 + nrm(ks[9], (DEPTH, D_MODEL), 0.05),
        'w_in': nrm(ks[10], (DEPTH, D_MODEL, IN_COLS), D_MODEL ** -0.5),
        's5_lam_re': -0.5 + nrm(ks[11], (DEPTH, 2, G, P), 0.01),
        's5_lam_im': lam_im0 + nrm(ks[12], (DEPTH, 2, G, P), 0.01),
        's5_log_dt': jax.random.uniform(ks[13], (DEPTH, 2, G), jnp.float32,
                                        math.log(S5_DT_MIN), math.log(S5_DT_MAX)),
        's5_b_re': nrm(ks[14], (DEPTH, 2, G, P, H), (2 * H) ** -0.5),
        's5_b_im': nrm(ks[15], (DEPTH, 2, G, P, H), (2 * H) ** -0.5),
        's5_c_re': nrm(ks[16], (DEPTH, 2, G, H, P), P ** -0.5),
        's5_c_im': nrm(ks[17], (DEPTH, 2, G, H, P), P ** -0.5),
        's5_d': nrm(ks[18], (DEPTH, S5_WIDTH), 1.0),
        'w_glu': nrm(ks[19], (DEPTH, S5_WIDTH, S5_WIDTH), S5_WIDTH ** -0.5),
        'da_lambda': nrm(ks[20], (DEPTH, 4, DA_HEAD_DIM), 0.1),
        'da_subln': 1.0 + nrm(ks[21], (DEPTH, DA_V_DIM), 0.05),
        'w_proj_a': nrm(ks[22], (DEPTH, S5_WIDTH, D_MODEL), S5_WIDTH ** -0.5),
        'w_proj_b': nrm(ks[23], (DEPTH, DA_WIDTH, D_MODEL), DA_WIDTH ** -0.5),
        'w_out': nrm(ks[24], (DEPTH, D_MODEL, D_MODEL), D_MODEL ** -0.5),
        'w_router': nrm(ks[25], (DEPTH, D_MODEL, N_EXPERTS), D_MODEL ** -0.5),
        'w_exp_gate': nrm(ks[26], (DEPTH, N_EXPERTS, D_MODEL, D_EXPERT), D_MODEL ** -0.5),
        'w_exp_up': nrm(ks[27], (DEPTH, N_EXPERTS, D_MODEL, D_EXPERT), D_MODEL ** -0.5),
        'w_exp_down': nrm(ks[28], (DEPTH, N_EXPERTS, D_EXPERT, D_MODEL), D_EXPERT ** -0.5),
    }


def reference(x, c, ctx, c_ctx, w_ada, b_ada, norm_pre_mix, norm_post_mix, norm_pre_ffn,
              norm_post_ffn, w_in, s5_lam_re, s5_lam_im, s5_log_dt, s5_b_re, s5_b_im,
              s5_c_re, s5_c_im, s5_d, w_glu, da_lambda, da_subln, w_proj_a, w_proj_b,
              w_out, w_router, w_exp_gate, w_exp_up, w_exp_down):
    seq_len = x.shape[1]
    rows = seq_len // GRID_W
    row = jnp.repeat(jnp.arange(rows), GRID_W).astype(jnp.float32)
    col = jnp.tile(jnp.arange(GRID_W), rows).astype(jnp.float32)
    inv_freq = ROPE_THETA ** (-jnp.arange(ROPE_PAIRS, dtype=jnp.float32) / ROPE_PAIRS)
    row_ang = row[:, None] * inv_freq[None, :]
    col_ang = col[:, None] * inv_freq[None, :]
    xc = ctx
    for i in range(DEPTH):
        p = {
            'norm_pre_mix': norm_pre_mix[i], 'norm_post_mix': norm_post_mix[i],
            'norm_pre_ffn': norm_pre_ffn[i], 'norm_post_ffn': norm_post_ffn[i],
            'w_in': w_in[i], 's5_lam_re': s5_lam_re[i], 's5_lam_im': s5_lam_im[i],
            's5_log_dt': s5_log_dt[i], 's5_b_re': s5_b_re[i], 's5_b_im': s5_b_im[i],
            's5_c_re': s5_c_re[i], 's5_c_im': s5_c_im[i], 's5_d': s5_d[i], 'w_glu': w_glu[i],
            'da_lambda': da_lambda[i], 'da_subln': da_subln[i], 'w_proj_a': w_proj_a[i],
            'w_proj_b': w_proj_b[i], 'w_out': w_out[i], 'w_router': w_router[i],
            'w_exp_gate': w_exp_gate[i], 'w_exp_up': w_exp_up[i], 'w_exp_down': w_exp_down[i],
        }
        mod_lat = (jax.nn.silu(c) @ w_ada[i] + b_ada[i])[:, None, :]
        mod_ctx = jax.nn.silu(c_ctx) @ w_ada[i] + b_ada[i]
        lam_init = 0.8 - 0.6 * math.exp(-0.3 * i)
        x, xc = hybrid_layer(x, xc, mod_lat, mod_ctx, p, row_ang, col_ang, lam_init,
                             need_ctx_out=(i < DEPTH - 1))
    return x
```

```python
import functools
import math

import jax
import jax.numpy as jnp
from jax import lax
from jax.experimental import pallas as pl
from jax.experimental.pallas import tpu as pltpu

F32 = jnp.float32
BF16 = jnp.bfloat16

EPS = 1e-6
GRID_W = 64
ROPE_THETA = 10000.0
S5_GROUP = 16
S5_STATE = 64
S5_DT_MAX_RE = -1e-4
S5_CHUNK = 16
DA_HEADS = 4
DA_HEAD_DIM = 64
DA_V_DIM = 128
N_EXPERTS = 16
EC_CAPACITY = 2
LANES = 128
SUBLANES = 8
VMEM_LIMIT = 60 * 1024 * 1024
LOG2E = 1.4426950408889634
F32_TINY = 1e-37


def _cparams(sem):
    return pltpu.CompilerParams(dimension_semantics=sem, vmem_limit_bytes=VMEM_LIMIT)


def _rms(x, eps=EPS):
    return x * lax.rsqrt(jnp.mean(x * x, axis=-1, keepdims=True) + eps)


def _sigmoid(x):
    return 1.0 / (1.0 + jnp.exp(-x))


def _ada_kernel(c_ref, w_ref, b_ref, o_ref):
    c = c_ref[...]
    s = (c * _sigmoid(c)).astype(BF16)
    o_ref[...] = jnp.dot(s, w_ref[...].astype(BF16), preferred_element_type=F32) + b_ref[...]


def _ada(c_all, w_ada, b_ada):
    rows, d = c_all.shape
    n = w_ada.shape[1]
    tn = 512
    return pl.pallas_call(
        _ada_kernel,
        grid=(n // tn,),
        in_specs=[pl.BlockSpec((rows, d), lambda j: (0, 0)),
                  pl.BlockSpec((d, tn), lambda j: (0, j)),
                  pl.BlockSpec((1, tn), lambda j: (0, j))],
        out_specs=pl.BlockSpec((rows, tn), lambda j: (0, j)),
        out_shape=jax.ShapeDtypeStruct((rows, n), F32),
        compiler_params=_cparams(("arbitrary",)),
        name="ada",
    )(c_all, w_ada, b_ada.reshape(1, n))


def _block_transpose8(vs):
    lane_blk = lax.broadcasted_iota(jnp.int32, vs[0].shape, 1) // S5_GROUP
    vs = list(vs)
    for s in (4, 2, 1):
        upper = (lane_blk & s) != 0
        for i in range(8):
            if i & s:
                continue
            a, b = vs[i], vs[i + s]
            vs[i] = jnp.where(upper, pltpu.roll(b, S5_GROUP * s, 1), a)
            vs[i + s] = jnp.where(upper, b, pltpu.roll(a, LANES - S5_GROUP * s, 1))
    return vs


def _tokens_to_chunks(scr, out_ref, tm):
    nc = tm // S5_CHUNK
    for gt in range(scr.shape[0]):
        for j in range(S5_CHUNK // 8):
            vs = [scr[gt, pl.ds(8 * j + i, nc, stride=S5_CHUNK), :] for i in range(8)]
            vs = _block_transpose8(vs)
            for gi in range(8):
                out_ref[gt * 8 + gi, :, j * LANES:(j + 1) * LANES] = vs[gi].astype(out_ref.dtype)


def _chunks_to_tokens(in_ref, scr, tm):
    nc = tm // S5_CHUNK
    for gt in range(scr.shape[0]):
        for j in range(S5_CHUNK // 8):
            vs = [in_ref[gt * 8 + gi, :, j * LANES:(j + 1) * LANES].astype(F32) for gi in range(8)]
            vs = _block_transpose8(vs)
            for i in range(8):
                scr[gt, pl.ds(8 * j + i, nc, stride=S5_CHUNK), :] = vs[i]


def _swap16(x):
    lane = lax.broadcasted_iota(jnp.int32, x.shape, 1)
    return jnp.where((lane & 16) == 0, pltpu.roll(x, LANES - 16, 1), pltpu.roll(x, 16, 1))


def _inproj_kernel(x_ref, nw_ref, sh_ref, sc_ref, w_ref, cos_ref, sin_ref, *refs, latent, widths):
    x = x_ref[0]
    tm = x.shape[0]
    h = _rms(x) * nw_ref[...]
    h = (h * (1.0 + sc_ref[0]) + sh_ref[0]).astype(BF16)
    s5w, qkw, vw, dm = widths
    o_u, o_q, o_k, o_v, o_ga = 0, s5w, s5w + qkw, s5w + 2 * qkw, s5w + 2 * qkw + vw

    def proj(lo, n):
        return jnp.dot(h, w_ref[:, lo:lo + n], preferred_element_type=F32)

    def rope(z, scale):
        cos = cos_ref[...]
        sin = sin_ref[...]
        parts = []
        for j in range(z.shape[1] // LANES):
            zj = z[:, j * LANES:(j + 1) * LANES]
            parts.append((zj * cos + _swap16(zj) * sin) * scale)
        return jnp.concatenate(parts, axis=1)

    if latent:
        u_ref, q_ref, k_ref, v_ref, ga_ref, gb_ref, u_scr = refs
        q_ref[0] = rope(proj(o_q, qkw), DA_HEAD_DIM ** -0.5 * LOG2E).astype(BF16)
        k_ref[0] = rope(proj(o_k, qkw), 1.0).astype(BF16)
        v_ref[0] = proj(o_v, vw).astype(BF16)
        ga_ref[0] = _sigmoid(proj(o_ga, dm)).astype(BF16)
        gb_ref[0] = _sigmoid(proj(o_ga + dm, dm)).astype(BF16)
    else:
        u_ref, k_ref, v_ref, u_scr = refs
        k_ref[0] = proj(o_k, qkw).astype(BF16)
        v_ref[0] = proj(o_v, vw).astype(BF16)
    u = proj(o_u, s5w)
    for gt in range(s5w // LANES):
        u_scr[gt] = u[:, gt * LANES:(gt + 1) * LANES]
    _tokens_to_chunks(u_scr, u_ref, tm)


def _inproj(x, nw, sh, sc, w_in, cos_t, sin_t, *, latent, widths, tm):
    b, l, d = x.shape
    s5w, qkw, vw, dm = widths
    groups = s5w // S5_GROUP
    nc = tm // S5_CHUNK
    per_b = sh.shape[0] > 1
    mod_spec = pl.BlockSpec((1, 1, d), (lambda i, j: (i, 0, 0)) if per_b else (lambda i, j: (0, 0, 0)))

    def tok_spec(n):
        return pl.BlockSpec((1, tm, n), lambda i, j: (i, j, 0))

    nj = l // tm
    u_spec = pl.BlockSpec((groups, nc, S5_CHUNK * S5_GROUP), lambda i, j: (0, i * nj + j, 0))
    u_shape = jax.ShapeDtypeStruct((groups, b * (l // S5_CHUNK), S5_CHUNK * S5_GROUP), BF16)
    out_w = (qkw, qkw, vw, dm, dm) if latent else (qkw, vw)
    return pl.pallas_call(
        functools.partial(_inproj_kernel, latent=latent, widths=widths),
        grid=(b, nj),
        in_specs=[tok_spec(d),
                  pl.BlockSpec((1, d), lambda i, j: (0, 0)),
                  mod_spec, mod_spec,
                  pl.BlockSpec(w_in.shape, lambda i, j: (0, 0)),
                  pl.BlockSpec((tm, LANES), lambda i, j: (j, 0)),
                  pl.BlockSpec((tm, LANES), lambda i, j: (j, 0))],
        out_specs=[u_spec] + [tok_spec(n) for n in out_w],
        out_shape=[u_shape] + [jax.ShapeDtypeStruct((b, l, n), BF16) for n in out_w],
        scratch_shapes=[pltpu.VMEM((s5w // LANES, tm, LANES), F32)],
        compiler_params=_cparams(("parallel", "parallel")),
        name="inproj_lat" if latent else "inproj_ctx",
    )(x, nw, sh, sc, w_in, cos_t, sin_t)


def _rope_tables(seq_len):
    rows = seq_len // GRID_W
    pairs = DA_HEAD_DIM // 4
    row = jnp.repeat(jnp.arange(rows), GRID_W).astype(F32)
    col = jnp.tile(jnp.arange(GRID_W), rows).astype(F32)
    inv_freq = ROPE_THETA ** (-jnp.arange(pairs, dtype=F32) / pairs)
    ra = row[:, None] * inv_freq[None, :]
    ca = col[:, None] * inv_freq[None, :]
    cos64 = jnp.concatenate([jnp.cos(ra), jnp.cos(ra), jnp.cos(ca), jnp.cos(ca)], axis=1)
    sin64 = jnp.concatenate([-jnp.sin(ra), jnp.sin(ra), -jnp.sin(ca), jnp.sin(ca)], axis=1)
    return jnp.tile(cos64, (1, 2)), jnp.tile(sin64, (1, 2))


def _s5_matrices(lam_re, lam_im, log_dt, b_re, b_im, c_re, c_im, d_skip):
    hp = lax.Precision.HIGHEST
    t = S5_CHUNK
    lam_re = jnp.minimum(lam_re.astype(F32), S5_DT_MAX_RE)
    lam_im = lam_im.astype(F32)
    dt = jnp.exp(log_dt.astype(F32))[..., None]
    g, p = lam_re.shape[1:]
    hh = S5_GROUP
    mag = jnp.exp(lam_re * dt)
    lb_re = mag * jnp.cos(lam_im * dt)
    lb_im = mag * jnp.sin(lam_im * dt)
    den = lam_re * lam_re + lam_im * lam_im
    num_re = lb_re - 1.0
    co_re = (num_re * lam_re + lb_im * lam_im) / den
    co_im = (lb_im * lam_re - num_re * lam_im) / den
    br = b_re.astype(F32)
    bi = b_im.astype(F32)
    bb_re = co_re[..., None] * br - co_im[..., None] * bi
    bb_im = co_re[..., None] * bi + co_im[..., None] * br
    j = jnp.arange(t + 1, dtype=F32)[None, None, :, None]
    pmag = jnp.exp(lam_re[:, :, None, :] * dt[:, :, None, :] * j)
    pang = lam_im[:, :, None, :] * dt[:, :, None, :] * j
    pw_re = pmag * jnp.cos(pang)
    pw_im = pmag * jnp.sin(pang)
    cr = c_re.astype(F32)
    ci = c_im.astype(F32)
    cp_re = cr[:, :, None] * pw_re[:, :, :t, None, :] - ci[:, :, None] * pw_im[:, :, :t, None, :]
    cp_im = cr[:, :, None] * pw_im[:, :, :t, None, :] + ci[:, :, None] * pw_re[:, :, :t, None, :]
    taps = (jnp.einsum('dgjhp,dgpi->dgjhi', cp_re, bb_re, precision=hp)
            - jnp.einsum('dgjhp,dgpi->dgjhi', cp_im, bb_im, precision=hp))
    sig = jnp.arange(t)[:, None]
    tau = jnp.arange(t)[None, :]
    kf = jnp.where((tau >= sig)[None, :, :, None, None], taps[0][:, jnp.clip(tau - sig, 0, t - 1)], 0.0)
    kr = jnp.where((sig >= tau)[None, :, :, None, None], taps[1][:, jnp.clip(sig - tau, 0, t - 1)], 0.0)
    skip = d_skip.astype(F32).reshape(g, hh)
    eye_t = jnp.eye(t, dtype=F32)[None, :, :, None, None]
    eye_h = jnp.eye(hh, dtype=F32)[None, None, None] * skip[:, None, None, :, None]
    m_full = kf + kr + eye_t * eye_h
    m_mat = m_full.transpose(0, 1, 4, 2, 3).reshape(g, t * hh, t * hh)
    pf_re = pw_re[0][:, t - 1 - jnp.arange(t)]
    pf_im = pw_im[0][:, t - 1 - jnp.arange(t)]
    pr_re = pw_re[1][:, :t]
    pr_im = pw_im[1][:, :t]

    def state_in(pr_, pi_, br_, bi_):
        brt = br_.transpose(0, 2, 1)[:, None]
        bit = bi_.transpose(0, 2, 1)[:, None]
        re = pr_[:, :, None, :] * brt - pi_[:, :, None, :] * bit
        im = pr_[:, :, None, :] * bit + pi_[:, :, None, :] * brt
        return re.reshape(g, t * hh, p), im.reshape(g, t * hh, p)

    wf_re, wf_im = state_in(pf_re, pf_im, bb_re[0], bb_im[0])
    wr_re, wr_im = state_in(pr_re, pr_im, bb_re[1], bb_im[1])
    w_mat = jnp.concatenate([wf_re, wr_re, wf_im, wr_im], axis=2)
    ef_re = pw_re[0][:, 1:t + 1]
    ef_im = pw_im[0][:, 1:t + 1]
    er_re = pw_re[1][:, t - jnp.arange(t)]
    er_im = pw_im[1][:, t - jnp.arange(t)]

    def state_out(e_re, e_im, cr_, ci_):
        g_re = cr_[:, None] * e_re[:, :, None, :] - ci_[:, None] * e_im[:, :, None, :]
        g_im = cr_[:, None] * e_im[:, :, None, :] + ci_[:, None] * e_re[:, :, None, :]
        return (g_re.transpose(0, 3, 1, 2).reshape(g, p, t * hh),
                (-g_im).transpose(0, 3, 1, 2).reshape(g, p, t * hh))

    vf_re, vf_im = state_out(ef_re, ef_im, cr[0], ci[0])
    vr_re, vr_im = state_out(er_re, er_im, cr[1], ci[1])
    v_mat = jnp.concatenate([vf_re, vr_re, vf_im, vr_im], axis=1)
    a_mat = jnp.stack([jnp.concatenate([pw_re[0][:, t], pw_re[1][:, t]], axis=1),
                       jnp.concatenate([pw_im[0][:, t], pw_im[1][:, t]], axis=1)], axis=1)
    return w_mat.astype(BF16), m_mat.astype(BF16), v_mat.astype(BF16), a_mat


def _s5_kernel(uc_ref, ul_ref, w_ref, m_ref, v_ref, a_ref, y_ref, sc_scr, sl_scr, hf_scr, hr_scr, *, n_ctx, n_lat):
    nb = SUBLANES
    half = S5_STATE
    ul = ul_ref[0]
    w = w_ref[0]
    s_c = jnp.dot(uc_ref[0], w, preferred_element_type=F32)
    s_l = jnp.dot(ul, w, preferred_element_type=F32)
    for part in range(2):
        sc_scr[part] = s_c[:, part * LANES:(part + 1) * LANES]
        sl_scr[part] = s_l[:, part * LANES:(part + 1) * LANES]
    a = a_ref[0]
    a_re = a[0:1, :]
    a_im = a[1:2, :]
    is_fwd = lax.broadcasted_iota(jnp.int32, (nb, LANES), 1) < half

    def pick(scr, jf, jr, n):
        rows_f = pl.ds(jf, nb, stride=n)
        rows_r = pl.ds(jr, nb, stride=n)
        return (jnp.where(is_fwd, scr[0, rows_f, :], scr[0, rows_r, :]),
                jnp.where(is_fwd, scr[1, rows_f, :], scr[1, rows_r, :]))

    def update(h_re, h_im, s_re, s_im):
        return a_re * h_re - a_im * h_im + s_re, a_re * h_im + a_im * h_re + s_im

    def ctx_step(i, carry):
        return update(*carry, *pick(sc_scr, i, n_ctx - 1 - i, n_ctx))

    def lat_step(c, carry):
        h_re, h_im = carry
        rows_f = pl.ds(c, nb, stride=n_lat)
        rows_r = pl.ds(n_lat - 1 - c, nb, stride=n_lat)
        hf_scr[0, rows_f, :] = h_re
        hf_scr[1, rows_f, :] = h_im
        hr_scr[0, rows_r, :] = h_re
        hr_scr[1, rows_r, :] = h_im
        return update(h_re, h_im, *pick(sl_scr, c, n_lat - 1 - c, n_lat))

    zero = jnp.zeros((nb, LANES), F32)
    carry = lax.fori_loop(0, n_ctx, ctx_step, (zero, zero))
    lax.fori_loop(0, n_lat, lat_step, carry)
    lane = lax.broadcasted_iota(jnp.int32, hf_scr.shape[1:], 1)
    hin = jnp.concatenate([jnp.where(lane < half, hf_scr[part], hr_scr[part]) for part in range(2)],
                          axis=1).astype(BF16)
    y = jnp.dot(ul, m_ref[0], preferred_element_type=F32)
    y = y + jnp.dot(hin, v_ref[0], preferred_element_type=F32)
    y_ref[0] = y.astype(BF16)


def _s5(u_ctx, u_lat, w_mat, m_mat, v_mat, a_mat, *, n_ctx, n_lat):
    g, rc, k = u_ctx.shape
    rl = u_lat.shape[1]
    mat_spec = pl.BlockSpec((1, k, k), lambda i: (i, 0, 0))
    return pl.pallas_call(
        functools.partial(_s5_kernel, n_ctx=n_ctx, n_lat=n_lat),
        grid=(g,),
        in_specs=[pl.BlockSpec((1, rc, k), lambda i: (i, 0, 0)),
                  pl.BlockSpec((1, rl, k), lambda i: (i, 0, 0)),
                  mat_spec, mat_spec, mat_spec,
                  pl.BlockSpec((1, 2, LANES), lambda i: (i, 0, 0))],
        out_specs=pl.BlockSpec((1, rl, k), lambda i: (i, 0, 0)),
        out_shape=jax.ShapeDtypeStruct((g, rl, k), BF16),
        scratch_shapes=[pltpu.VMEM((2, rc, LANES), F32), pltpu.VMEM((2, rl, LANES), F32),
                        pltpu.VMEM((2, rl, LANES), F32), pltpu.VMEM((2, rl, LANES), F32)],
        compiler_params=_cparams(("parallel",)),
        name="s5",
    )(u_ctx, u_lat, w_mat, m_mat, v_mat, a_mat)


def _attn_kernel(lam_ref, q_ref, kt_ref, v_ref, sw_ref, o_ref, *, out_scale):
    lam = lam_ref[0, 0]
    sw = sw_ref[...]
    for h in range(DA_HEADS):
        q = q_ref[0, :, h * DA_V_DIM:(h + 1) * DA_V_DIM]
        tq = q.shape[0]
        lane = lax.broadcasted_iota(jnp.int32, q.shape, 1)
        zero = jnp.zeros_like(q)
        q2 = jnp.concatenate([jnp.where(lane < DA_HEAD_DIM, q, zero),
                              jnp.where(lane >= DA_HEAD_DIM, q, zero)], axis=0)
        s = jnp.dot(q2, kt_ref[0, h], preferred_element_type=F32)
        m = jnp.max(s, axis=-1, keepdims=True)
        p = jnp.exp2(s - m)
        l = jnp.sum(p, axis=-1, keepdims=True)
        pd = p[:tq] * (1.0 / l[:tq]) - p[tq:] * (lam / l[tq:])
        o = jnp.dot(pd.astype(BF16), v_ref[0, :, h * DA_V_DIM:(h + 1) * DA_V_DIM],
                    preferred_element_type=F32)
        o_ref[0, :, h * DA_V_DIM:(h + 1) * DA_V_DIM] = (_rms(o) * sw * out_scale).astype(BF16)


def _attn(lam, q, kt, v, subln, *, tq, out_scale):
    b, l, w = q.shape
    lk = v.shape[1]
    return pl.pallas_call(
        functools.partial(_attn_kernel, out_scale=out_scale),
        grid=(b, l // tq),
        in_specs=[pl.BlockSpec(memory_space=pltpu.SMEM),
                  pl.BlockSpec((1, tq, w), lambda i, j: (i, j, 0)),
                  pl.BlockSpec((1, DA_HEADS, DA_V_DIM, lk), lambda i, j: (i, 0, 0, 0)),
                  pl.BlockSpec((1, lk, w), lambda i, j: (i, 0, 0)),
                  pl.BlockSpec((1, DA_V_DIM), lambda i, j: (0, 0))],
        out_specs=pl.BlockSpec((1, tq, w), lambda i, j: (i, j, 0)),
        out_shape=jax.ShapeDtypeStruct((b, l, w), BF16),
        compiler_params=_cparams(("parallel", "parallel")),
        name="attn",
    )(lam, q, kt, v, subln)


def _gelu_tanh(x):
    return 0.5 * x * (1.0 + jnp.tanh(math.sqrt(2.0 / math.pi) * (x + 0.044715 * (x * x * x))))


def _merge_kernel(x_ref, ys_ref, yb_ref, ga_ref, gb_ref, gm_ref, shf_ref, scf_ref,
                  wglu_ref, wpa_ref, wpb_ref, wout_ref, npost_ref, npre_ref, wrh_ref, wrl_ref,
                  x1_ref, h2_ref, aff_ref, afft_ref, ys_scr):
    tm = x_ref.shape[1]
    _chunks_to_tokens(ys_ref, ys_scr, tm)
    ya = _gelu_tanh(jnp.concatenate([ys_scr[gt] for gt in range(ys_scr.shape[0])], axis=1))
    ya = ya * _sigmoid(jnp.dot(ya.astype(BF16), wglu_ref[...], preferred_element_type=F32))
    pa = jnp.dot(ya.astype(BF16), wpa_ref[...], preferred_element_type=F32)
    pb = jnp.dot(yb_ref[0], wpb_ref[...], preferred_element_type=F32)
    mix = ga_ref[0].astype(F32) * pa + gb_ref[0].astype(F32) * pb
    o = jnp.dot(mix.astype(BF16), wout_ref[...], preferred_element_type=F32)
    x1 = x_ref[0] + gm_ref[0] * (_rms(o) * npost_ref[...])
    x1_ref[0] = x1
    h = _rms(x1) * npre_ref[...]
    h = h * (1.0 + scf_ref[0]) + shf_ref[0]
    dk = h.shape[1] // LANES
    for k in range(dk):
        h2_ref[0, pl.ds(k, tm, stride=dk), :] = h[:, k * LANES:(k + 1) * LANES]
    h_hi = h.astype(BF16)
    h_lo = (h - h_hi.astype(F32)).astype(BF16)
    wh = wrh_ref[...]
    logits = (jnp.dot(h_hi, wh, preferred_element_type=F32)
              + jnp.dot(h_lo, wh, preferred_element_type=F32)
              + jnp.dot(h_hi, wrl_ref[...], preferred_element_type=F32))
    lane = lax.broadcasted_iota(jnp.int32, logits.shape, 1)
    logits = jnp.where(lane < N_EXPERTS, logits, -1e30)
    ex = jnp.exp(logits - jnp.max(logits, axis=-1, keepdims=True))
    aff = ex / jnp.sum(ex, axis=-1, keepdims=True)
    aff_ref[0] = aff
    afft_ref[0] = aff.T[:N_EXPERTS, :]


def _merge(x, ys_rows, yb, ga, gb, gm, shf, scf, wglu, wpa, wpb, wout, npost, npre, wrh, wrl, *, tm):
    b, l, d = x.shape
    dk = d // LANES
    groups, _, ck = ys_rows.shape
    nj = l // tm

    def tok(n):
        return pl.BlockSpec((1, tm, n), lambda i, j: (i, j, 0))

    def full(a):
        return pl.BlockSpec(a.shape, lambda i, j: (0,) * a.ndim)

    mod = pl.BlockSpec((1, 1, d), lambda i, j: (i, 0, 0))
    return pl.pallas_call(
        _merge_kernel,
        grid=(b, nj),
        in_specs=[tok(d),
                  pl.BlockSpec((groups, tm // S5_CHUNK, ck), lambda i, j: (0, i * nj + j, 0)),
                  tok(yb.shape[2]), tok(d), tok(d), mod, mod, mod,
                  full(wglu), full(wpa), full(wpb), full(wout), full(npost), full(npre), full(wrh), full(wrl)],
        out_specs=[tok(d),
                   pl.BlockSpec((1, tm * dk, LANES), lambda i, j: (i, j, 0)),
                   tok(LANES),
                   pl.BlockSpec((1, N_EXPERTS, tm), lambda i, j: (i, 0, j))],
        out_shape=[jax.ShapeDtypeStruct((b, l, d), F32),
                   jax.ShapeDtypeStruct((b, l * dk, LANES), F32),
                   jax.ShapeDtypeStruct((b, l, LANES), F32),
                   jax.ShapeDtypeStruct((b, N_EXPERTS, l), F32)],
        scratch_shapes=[pltpu.VMEM((groups * S5_GROUP // LANES, tm, LANES), F32)],
        compiler_params=_cparams(("parallel", "parallel")),
        name="merge",
    )(x, ys_rows, yb, ga, gb, gm, shf, scf, wglu, wpa, wpb, wout, npost, npre, wrh, wrl)


def _route_kernel(a_ref, tok_ref, tri_ref, idx_ref, pos_scr, *, cap):
    a = a_ref[0]
    ne, l = a.shape
    capf = float(cap)

    def count_ge(thr):
        return jnp.sum(jnp.where(a >= thr, 1.0, 0.0), axis=1, keepdims=True)

    def narrow(mid, lo, hi):
        mid = jnp.minimum(jnp.maximum(mid, lo), hi)
        ok = count_ge(mid) >= capf
        return jnp.where(ok, mid, lo), jnp.where(ok, hi, mid)

    def geo(_, lohi):
        lo, hi = lohi
        return narrow(jnp.sqrt(jnp.maximum(lo, F32_TINY) * hi), lo, hi)

    def ari(_, lohi):
        lo, hi = lohi
        return narrow(0.5 * lo + 0.5 * hi, lo, hi)

    lohi = (jnp.zeros((ne, 1), F32), jnp.full((ne, 1), 2.0, F32))
    lohi = lax.fori_loop(0, 34, geo, lohi)
    lo, hi = lax.fori_loop(0, 8, ari, lohi)
    gt = a >= hi
    eq = (a >= lo) & jnp.logical_not(gt)
    need = capf - jnp.sum(jnp.where(gt, 1.0, 0.0), axis=1, keepdims=True)

    tri = tri_ref[...]

    def excl_cumsum(mask):
        mb = jnp.where(mask, 1.0, 0.0).astype(BF16)
        off = jnp.zeros((ne, 1), F32)
        outs = []
        for j in range(l // LANES):
            blk = mb[:, j * LANES:(j + 1) * LANES]
            outs.append(jnp.dot(blk, tri, preferred_element_type=F32) + off)
            off = off + jnp.sum(blk.astype(F32), axis=1, keepdims=True)
        return jnp.concatenate(outs, axis=1)

    sel = gt | (eq & (excl_cumsum(eq) < need))
    pos_scr[...] = jnp.where(sel, excl_cumsum(sel), -1.0)

    tc = 512
    slot = lax.broadcasted_iota(jnp.int32, (cap, tc), 0).astype(F32)

    def per_expert(e, _):
        acc = jnp.zeros((cap, LANES), F32)
        for c in range(l // tc):
            pc = pos_scr[pl.ds(e, 1), c * tc:(c + 1) * tc]
            onehot = jnp.where(pc == slot, 1.0, 0.0).astype(BF16)
            acc = acc + jnp.dot(onehot, tok_ref[c * tc:(c + 1) * tc, :], preferred_element_type=F32)
        acc_t = acc.T
        idx = acc_t[0:1, :] * 64.0 + acc_t[1:2, :]
        idx_ref[0, pl.ds(e, 1), :] = idx.astype(jnp.int32)
        return 0

    lax.fori_loop(0, ne, per_expert, 0)


def _route(aff_t, *, cap):
    b, ne, l = aff_t.shape
    t = jnp.arange(l, dtype=jnp.int32)
    tok = jnp.zeros((l, LANES), F32).at[:, 0].set((t >> 6).astype(F32)).at[:, 1].set((t & 63).astype(F32)).astype(BF16)
    tri = (jnp.arange(LANES)[:, None] < jnp.arange(LANES)[None, :]).astype(BF16)
    return pl.pallas_call(
        functools.partial(_route_kernel, cap=cap),
        grid=(b,),
        in_specs=[pl.BlockSpec((1, ne, l), lambda i: (i, 0, 0)),
                  pl.BlockSpec((l, LANES), lambda i: (0, 0)),
                  pl.BlockSpec((LANES, LANES), lambda i: (0, 0))],
        out_specs=pl.BlockSpec((1, ne, cap), lambda i: (i, 0, 0)),
        out_shape=jax.ShapeDtypeStruct((b, ne, cap), jnp.int32),
        scratch_shapes=[pltpu.VMEM((ne, l), F32)],
        compiler_params=_cparams(("parallel",)),
        name="route",
    )(aff_t, tok, tri)


def _moe_kernel(idx_ref, h2_ref, aff_ref, wg_ref, wu_ref, wd_ref, f_ref,
                xs_scr, xg_scr, ys_scr, y2_scr, g_scr, *, cap, unroll):
    e = pl.program_id(1)
    fh = pl.program_id(2)
    nf = pl.num_programs(2)
    d = xs_scr.shape[1]
    dk = d // LANES

    @pl.when((e == 0) & (fh == 0))
    def _():
        f_ref[...] = jnp.zeros_like(f_ref)

    @pl.when(fh == 0)
    def _():
        def gather(s, _):
            t = idx_ref[0, 0, s]
            xg_scr[pl.ds(pl.multiple_of(s * dk, dk), dk), :] = h2_ref[0, pl.ds(pl.multiple_of(t * dk, dk), dk), :]
            g_scr[pl.ds(s, 1), :] = aff_ref[0, pl.ds(t, 1), :]
            return 0

        lax.fori_loop(0, cap, gather, 0, unroll=unroll)
        for k in range(dk):
            xs_scr[:, k * LANES:(k + 1) * LANES] = xg_scr[pl.ds(k, cap, stride=dk), :].astype(BF16)

    xs = xs_scr[...]
    gg = jnp.dot(xs, wg_ref[0], preferred_element_type=F32)
    uu = jnp.dot(xs, wu_ref[0], preferred_element_type=F32)
    hid = (gg * _sigmoid(gg) * uu).astype(BF16)
    y = jnp.dot(hid, wd_ref[0], preferred_element_type=F32)

    @pl.when(fh == 0)
    def _():
        ys_scr[...] = y

    @pl.when((fh > 0) & (fh < nf - 1))
    def _():
        ys_scr[...] += y

    @pl.when(fh == nf - 1)
    def _():
        lane = lax.broadcasted_iota(jnp.int32, g_scr.shape, 1)
        gate = jnp.sum(jnp.where(lane == e, g_scr[...], 0.0), axis=1, keepdims=True)
        yg = (ys_scr[...] + y) * gate
        for k in range(dk):
            y2_scr[pl.ds(k, cap, stride=dk), :] = yg[:, k * LANES:(k + 1) * LANES]

        def scatter(s0, _):
            rows = []
            vals = []
            for i in range(unroll):
                s = s0 * unroll + i
                r = pl.multiple_of(idx_ref[0, 0, s] * dk, dk)
                rows.append(r)
                vals.append(f_ref[0, pl.ds(r, dk), :] + y2_scr[pl.ds(pl.multiple_of(s * dk, dk), dk), :])
            for r, v in zip(rows, vals):
                f_ref[0, pl.ds(r, dk), :] = v
            return 0

        lax.fori_loop(0, cap // unroll, scatter, 0)


def _moe(idx, h2, aff, wg, wu, wd, *, cap, nsplit):
    b, ne, _ = idx.shape
    l = aff.shape[1]
    d, fdim = wg.shape[1:]
    dk = d // LANES
    tf = fdim // nsplit
    once = pl.Buffered(1)
    return pl.pallas_call(
        functools.partial(_moe_kernel, cap=cap, unroll=4),
        grid=(b, ne, nsplit),
        in_specs=[pl.BlockSpec((1, 1, cap), lambda i, e, f: (i * ne + e, 0, 0), memory_space=pltpu.SMEM),
                  pl.BlockSpec((1, l * dk, LANES), lambda i, e, f: (i, 0, 0), pipeline_mode=once),
                  pl.BlockSpec((1, l, LANES), lambda i, e, f: (i, 0, 0), pipeline_mode=once),
                  pl.BlockSpec((1, d, tf), lambda i, e, f: (e, 0, f)),
                  pl.BlockSpec((1, d, tf), lambda i, e, f: (e, 0, f)),
                  pl.BlockSpec((1, tf, d), lambda i, e, f: (e, f, 0))],
        out_specs=pl.BlockSpec((1, l * dk, LANES), lambda i, e, f: (i, 0, 0), pipeline_mode=once),
        out_shape=jax.ShapeDtypeStruct((b, l * dk, LANES), F32),
        scratch_shapes=[pltpu.VMEM((cap, d), BF16),
                        pltpu.VMEM((cap * dk, LANES), F32),
                        pltpu.VMEM((cap, d), F32),
                        pltpu.VMEM((cap * dk, LANES), F32),
                        pltpu.VMEM((cap, LANES), F32)],
        compiler_params=_cparams(("arbitrary", "arbitrary", "arbitrary")),
        name="moe",
    )(idx.reshape(b * ne, 1, cap), h2, aff, wg, wu, wd)


def _final_kernel(f_ref, x1_ref, gf_ref, nw_ref, o_ref):
    tm, d = x1_ref.shape[1:]
    dk = d // LANES
    f = jnp.concatenate([f_ref[0, pl.ds(k, tm, stride=dk), :] for k in range(dk)], axis=1)
    o_ref[0] = x1_ref[0] + gf_ref[0] * (_rms(f) * nw_ref[...])


def _final(f, x1, gf, nw, *, tm):
    b, l, d = x1.shape
    dk = d // LANES
    return pl.pallas_call(
        _final_kernel,
        grid=(b, l // tm),
        in_specs=[pl.BlockSpec((1, tm * dk, LANES), lambda i, j: (i, j, 0)),
                  pl.BlockSpec((1, tm, d), lambda i, j: (i, j, 0)),
                  pl.BlockSpec((1, 1, d), lambda i, j: (i, 0, 0)),
                  pl.BlockSpec((1, d), lambda i, j: (0, 0))],
        out_specs=pl.BlockSpec((1, tm, d), lambda i, j: (i, j, 0)),
        out_shape=jax.ShapeDtypeStruct((b, l, d), F32),
        compiler_params=_cparams(("parallel", "parallel")),
        name="final",
    )(f, x1, gf, nw)


def kernel(x, c, ctx, c_ctx, w_ada, b_ada, norm_pre_mix, norm_post_mix, norm_pre_ffn, norm_post_ffn, w_in, s5_lam_re, s5_lam_im, s5_log_dt, s5_b_re, s5_b_im, s5_c_re, s5_c_im, s5_d, w_glu, da_lambda, da_subln, w_proj_a, w_proj_b, w_out, w_router, w_exp_gate, w_exp_up, w_exp_down):
    depth = w_ada.shape[0]
    assert depth == 1, "single trunk layer: the context stream's outputs are never consumed"
    b, l, d = x.shape
    lc = ctx.shape[1]
    assert b == SUBLANES and l % (S5_CHUNK * 32) == 0 and lc % (S5_CHUNK * 16) == 0
    s5w = s5_d.shape[1]
    qkw = DA_HEADS * 2 * DA_HEAD_DIM
    vw = DA_HEADS * DA_V_DIM
    widths = (s5w, qkw, vw, d)
    lam_init = 0.8 - 0.6 * math.exp(-0.3 * 0)

    c_all = jnp.zeros((2 * SUBLANES, d), F32).at[:b].set(c).at[b].set(c_ctx)
    mod = _ada(c_all, w_ada[0], b_ada[0])
    sh_m, sc_m, g_m, sh_f, sc_f, g_f = [mod[:b, i * d:(i + 1) * d].reshape(b, 1, d) for i in range(6)]
    csh_m = mod[b:b + 1, 0:d].reshape(1, 1, d)
    csc_m = mod[b:b + 1, d:2 * d].reshape(1, 1, d)

    w_in_b = w_in[0].astype(BF16)
    cos_t, sin_t = _rope_tables(l)
    npm = norm_pre_mix[0].reshape(1, d)
    u, q, k, v, ga, gb = _inproj(x, npm, sh_m, sc_m, w_in_b, cos_t, sin_t, latent=True, widths=widths, tm=512)
    uc, kc, vc = _inproj(ctx, npm, csh_m, csc_m, w_in_b, cos_t[:lc], sin_t[:lc], latent=False, widths=widths, tm=lc)

    w_mat, m_mat, v_mat, a_mat = _s5_matrices(s5_lam_re[0], s5_lam_im[0], s5_log_dt[0], s5_b_re[0], s5_b_im[0],
                                              s5_c_re[0], s5_c_im[0], s5_d[0])
    ys_rows = _s5(uc, u, w_mat, m_mat, v_mat, a_mat, n_ctx=lc // S5_CHUNK, n_lat=l // S5_CHUNK)

    lq1, lk1, lq2, lk2 = da_lambda[0].astype(F32)
    lam = jnp.exp(jnp.sum(lq1 * lk1)) - jnp.exp(jnp.sum(lq2 * lk2)) + lam_init
    k_all = jnp.concatenate([kc, k], axis=1)
    kt = k_all.reshape(b, lc + l, DA_HEADS, DA_V_DIM).transpose(0, 2, 3, 1)
    v_all = jnp.concatenate([vc, v], axis=1)
    yb = _attn(lam.reshape(1, 1), q, kt, v_all, da_subln[0].reshape(1, DA_V_DIM), tq=128, out_scale=1.0 - lam_init)

    wr = jnp.zeros((d, LANES), F32).at[:, :N_EXPERTS].set(w_router[0])
    wr_hi = wr.astype(BF16)
    wr_lo = (wr - wr_hi.astype(F32)).astype(BF16)
    x1, h2, aff, aff_t = _merge(x, ys_rows, yb, ga, gb, g_m, sh_f, sc_f,
                                w_glu[0].astype(BF16), w_proj_a[0].astype(BF16), w_proj_b[0].astype(BF16),
                                w_out[0].astype(BF16), norm_post_mix[0].reshape(1, d), norm_pre_ffn[0].reshape(1, d),
                                wr_hi, wr_lo, tm=512)

    cap = EC_CAPACITY * l // N_EXPERTS
    idx = _route(aff_t, cap=cap)
    f = _moe(idx, h2, aff, w_exp_gate[0].astype(BF16), w_exp_up[0].astype(BF16), w_exp_down[0].astype(BF16),
             cap=cap, nsplit=4)
    return _final(f, x1, g_f, norm_post_ffn[0].reshape(1, d), tm=512)
```

```python
import functools
import math

import jax
import jax.numpy as jnp
from jax import lax
from jax.experimental import pallas as pl
from jax.experimental.pallas import tpu as pltpu

F32 = jnp.float32
BF16 = jnp.bfloat16

EPS = 1e-6
GRID_W = 64
ROPE_THETA = 10000.0
S5_GROUP = 16
S5_STATE = 64
S5_DT_MAX_RE = -1e-4
S5_CHUNK = 16
DA_HEADS = 4
DA_HEAD_DIM = 64
DA_V_DIM = 128
N_EXPERTS = 16
EC_CAPACITY = 2
LANES = 128
SUBLANES = 8
VMEM_LIMIT = 60 * 1024 * 1024
LOG2E = 1.4426950408889634
F32_TINY = 1e-37


def _cparams(sem):
    return pltpu.CompilerParams(dimension_semantics=sem, vmem_limit_bytes=VMEM_LIMIT)


def _rms(x, eps=EPS):
    return x * lax.rsqrt(jnp.mean(x * x, axis=-1, keepdims=True) + eps)


def _sigmoid(x):
    return 1.0 / (1.0 + jnp.exp(-x))


def _ada_kernel(c_ref, w_ref, b_ref, o_ref):
    c = c_ref[...]
    s = (c * _sigmoid(c)).astype(BF16)
    o_ref[...] = jnp.dot(s, w_ref[...].astype(BF16), preferred_element_type=F32) + b_ref[...]


def _ada(c_all, w_ada, b_ada):
    rows, d = c_all.shape
    n = w_ada.shape[1]
    tn = 512
    return pl.pallas_call(
        _ada_kernel,
        grid=(n // tn,),
        in_specs=[pl.BlockSpec((rows, d), lambda j: (0, 0)),
                  pl.BlockSpec((d, tn), lambda j: (0, j)),
                  pl.BlockSpec((1, tn), lambda j: (0, j))],
        out_specs=pl.BlockSpec((rows, tn), lambda j: (0, j)),
        out_shape=jax.ShapeDtypeStruct((rows, n), F32),
        compiler_params=_cparams(("arbitrary",)),
        name="ada",
    )(c_all, w_ada, b_ada.reshape(1, n))


def _block_transpose8(vs):
    lane_blk = lax.broadcasted_iota(jnp.int32, vs[0].shape, 1) // S5_GROUP
    vs = list(vs)
    for s in (4, 2, 1):
        upper = (lane_blk & s) != 0
        for i in range(8):
            if i & s:
                continue
            a, b = vs[i], vs[i + s]
            vs[i] = jnp.where(upper, pltpu.roll(b, S5_GROUP * s, 1), a)
            vs[i + s] = jnp.where(upper, b, pltpu.roll(a, LANES - S5_GROUP * s, 1))
    return vs


def _tokens_to_chunks(scr, out_ref, tm):
    nc = tm // S5_CHUNK
    for gt in range(scr.shape[0]):
        for j in range(S5_CHUNK // 8):
            vs = [scr[gt, pl.ds(8 * j + i, nc, stride=S5_CHUNK), :] for i in range(8)]
            vs = _block_transpose8(vs)
            for gi in range(8):
                out_ref[gt * 8 + gi, :, j * LANES:(j + 1) * LANES] = vs[gi].astype(out_ref.dtype)


def _chunks_to_tokens(in_ref, scr, tm):
    nc = tm // S5_CHUNK
    for gt in range(scr.shape[0]):
        for j in range(S5_CHUNK // 8):
            vs = [in_ref[gt * 8 + gi, :, j * LANES:(j + 1) * LANES].astype(F32) for gi in range(8)]
            vs = _block_transpose8(vs)
            for i in range(8):
                scr[gt, pl.ds(8 * j + i, nc, stride=S5_CHUNK), :] = vs[i]


def _swap16(x):
    lane = lax.broadcasted_iota(jnp.int32, x.shape, 1)
    return jnp.where((lane & 16) == 0, pltpu.roll(x, LANES - 16, 1), pltpu.roll(x, 16, 1))


def _inproj_kernel(x_ref, nw_ref, sh_ref, sc_ref, w_ref, cos_ref, sin_ref, *refs, latent, widths):
    x = x_ref[0]
    tm = x.shape[0]
    h = _rms(x) * nw_ref[...]
    h = (h * (1.0 + sc_ref[0]) + sh_ref[0]).astype(BF16)
    s5w, qkw, vw, dm = widths
    o_u, o_q, o_k, o_v, o_ga = 0, s5w, s5w + qkw, s5w + 2 * qkw, s5w + 2 * qkw + vw

    def proj(lo, n):
        return jnp.dot(h, w_ref[:, lo:lo + n], preferred_element_type=F32)

    def rope(z, scale):
        cos = cos_ref[...]
        sin = sin_ref[...]
        parts = []
        for j in range(z.shape[1] // LANES):
            zj = z[:, j * LANES:(j + 1) * LANES]
            parts.append((zj * cos + _swap16(zj) * sin) * scale)
        return jnp.concatenate(parts, axis=1)

    if latent:
        u_ref, q_ref, k_ref, v_ref, ga_ref, gb_ref, u_scr = refs
        q_ref[0] = rope(proj(o_q, qkw), DA_HEAD_DIM ** -0.5 * LOG2E).astype(BF16)
        k_ref[0] = rope(proj(o_k, qkw), 1.0).astype(BF16)
        v_ref[0] = proj(o_v, vw).astype(BF16)
        ga_ref[0] = _sigmoid(proj(o_ga, dm)).astype(BF16)
        gb_ref[0] = _sigmoid(proj(o_ga + dm, dm)).astype(BF16)
    else:
        u_ref, k_ref, v_ref, u_scr = refs
        k_ref[0] = proj(o_k, qkw).astype(BF16)
        v_ref[0] = proj(o_v, vw).astype(BF16)
    u = proj(o_u, s5w)
    for gt in range(s5w // LANES):
        u_scr[gt] = u[:, gt * LANES:(gt + 1) * LANES]
    _tokens_to_chunks(u_scr, u_ref, tm)


def _inproj(x, nw, sh, sc, w_in, cos_t, sin_t, *, latent, widths, tm):
    b, l, d = x.shape
    s5w, qkw, vw, dm = widths
    groups = s5w // S5_GROUP
    nc = tm // S5_CHUNK
    per_b = sh.shape[0] > 1
    mod_spec = pl.BlockSpec((1, 1, d), (lambda i, j: (i, 0, 0)) if per_b else (lambda i, j: (0, 0, 0)))

    def tok_spec(n):
        return pl.BlockSpec((1, tm, n), lambda i, j: (i, j, 0))

    nj = l // tm
    u_spec = pl.BlockSpec((groups, nc, S5_CHUNK * S5_GROUP), lambda i, j: (0, i * nj + j, 0))
    u_shape = jax.ShapeDtypeStruct((groups, b * (l // S5_CHUNK), S5_CHUNK * S5_GROUP), BF16)
    out_w = (qkw, qkw, vw, dm, dm) if latent else (qkw, vw)
    return pl.pallas_call(
        functools.partial(_inproj_kernel, latent=latent, widths=widths),
        grid=(b, nj),
        in_specs=[tok_spec(d),
                  pl.BlockSpec((1, d), lambda i, j: (0, 0)),
                  mod_spec, mod_spec,
                  pl.BlockSpec(w_in.shape, lambda i, j: (0, 0)),
                  pl.BlockSpec((tm, LANES), lambda i, j: (j, 0)),
                  pl.BlockSpec((tm, LANES), lambda i, j: (j, 0))],
        out_specs=[u_spec] + [tok_spec(n) for n in out_w],
        out_shape=[u_shape] + [jax.ShapeDtypeStruct((b, l, n), BF16) for n in out_w],
        scratch_shapes=[pltpu.VMEM((s5w // LANES, tm, LANES), F32)],
        compiler_params=_cparams(("parallel", "parallel")),
        name="inproj_lat" if latent else "inproj_ctx",
    )(x, nw, sh, sc, w_in, cos_t, sin_t)


def _rope_tables(seq_len):
    rows = seq_len // GRID_W
    pairs = DA_HEAD_DIM // 4
    row = jnp.repeat(jnp.arange(rows), GRID_W).astype(F32)
    col = jnp.tile(jnp.arange(GRID_W), rows).astype(F32)
    inv_freq = ROPE_THETA ** (-jnp.arange(pairs, dtype=F32) / pairs)
    ra = row[:, None] * inv_freq[None, :]
    ca = col[:, None] * inv_freq[None, :]
    cos64 = jnp.concatenate([jnp.cos(ra), jnp.cos(ra), jnp.cos(ca), jnp.cos(ca)], axis=1)
    sin64 = jnp.concatenate([-jnp.sin(ra), jnp.sin(ra), -jnp.sin(ca), jnp.sin(ca)], axis=1)
    return jnp.tile(cos64, (1, 2)), jnp.tile(sin64, (1, 2))


def _s5_matrices(lam_re, lam_im, log_dt, b_re, b_im, c_re, c_im, d_skip):
    hp = lax.Precision.HIGHEST
    t = S5_CHUNK
    lam_re = jnp.minimum(lam_re.astype(F32), S5_DT_MAX_RE)
    lam_im = lam_im.astype(F32)
    dt = jnp.exp(log_dt.astype(F32))[..., None]
    g, p = lam_re.shape[1:]
    hh = S5_GROUP
    mag = jnp.exp(lam_re * dt)
    lb_re = mag * jnp.cos(lam_im * dt)
    lb_im = mag * jnp.sin(lam_im * dt)
    den = lam_re * lam_re + lam_im * lam_im
    num_re = lb_re - 1.0
    co_re = (num_re * lam_re + lb_im * lam_im) / den
    co_im = (lb_im * lam_re - num_re * lam_im) / den
    br = b_re.astype(F32)
    bi = b_im.astype(F32)
    bb_re = co_re[..., None] * br - co_im[..., None] * bi
    bb_im = co_re[..., None] * bi + co_im[..., None] * br
    j = jnp.arange(t + 1, dtype=F32)[None, None, :, None]
    pmag = jnp.exp(lam_re[:, :, None, :] * dt[:, :, None, :] * j)
    pang = lam_im[:, :, None, :] * dt[:, :, None, :] * j
    pw_re = pmag * jnp.cos(pang)
    pw_im = pmag * jnp.sin(pang)
    cr = c_re.astype(F32)
    ci = c_im.astype(F32)
    cp_re = cr[:, :, None] * pw_re[:, :, :t, None, :] - ci[:, :, None] * pw_im[:, :, :t, None, :]
    cp_im = cr[:, :, None] * pw_im[:, :, :t, None, :] + ci[:, :, None] * pw_re[:, :, :t, None, :]
    taps = (jnp.einsum('dgjhp,dgpi->dgjhi', cp_re, bb_re, precision=hp)
            - jnp.einsum('dgjhp,dgpi->dgjhi', cp_im, bb_im, precision=hp))
    sig = jnp.arange(t)[:, None]
    tau = jnp.arange(t)[None, :]
    kf = jnp.where((tau >= sig)[None, :, :, None, None], taps[0][:, jnp.clip(tau - sig, 0, t - 1)], 0.0)
    kr = jnp.where((sig >= tau)[None, :, :, None, None], taps[1][:, jnp.clip(sig - tau, 0, t - 1)], 0.0)
    skip = d_skip.astype(F32).reshape(g, hh)
    eye_t = jnp.eye(t, dtype=F32)[None, :, :, None, None]
    eye_h = jnp.eye(hh, dtype=F32)[None, None, None] * skip[:, None, None, :, None]
    m_full = kf + kr + eye_t * eye_h
    m_mat = m_full.transpose(0, 1, 4, 2, 3).reshape(g, t * hh, t * hh)
    pf_re = pw_re[0][:, t - 1 - jnp.arange(t)]
    pf_im = pw_im[0][:, t - 1 - jnp.arange(t)]
    pr_re = pw_re[1][:, :t]
    pr_im = pw_im[1][:, :t]

    def state_in(pr_, pi_, br_, bi_):
        brt = br_.transpose(0, 2, 1)[:, None]
        bit = bi_.transpose(0, 2, 1)[:, None]
        re = pr_[:, :, None, :] * brt - pi_[:, :, None, :] * bit
        im = pr_[:, :, None, :] * bit + pi_[:, :, None, :] * brt
        return re.reshape(g, t * hh, p), im.reshape(g, t * hh, p)

    wf_re, wf_im = state_in(pf_re, pf_im, bb_re[0], bb_im[0])
    wr_re, wr_im = state_in(pr_re, pr_im, bb_re[1], bb_im[1])
    w_mat = jnp.concatenate([wf_re, wr_re, wf_im, wr_im], axis=2)
    ef_re = pw_re[0][:, 1:t + 1]
    ef_im = pw_im[0][:, 1:t + 1]
    er_re = pw_re[1][:, t - jnp.arange(t)]
    er_im = pw_im[1][:, t - jnp.arange(t)]

    def state_out(e_re, e_im, cr_, ci_):
        g_re = cr_[:, None] * e_re[:, :, None, :] - ci_[:, None] * e_im[:, :, None, :]
        g_im = cr_[:, None] * e_im[:, :, None, :] + ci_[:, None] * e_re[:, :, None, :]
        return (g_re.transpose(0, 3, 1, 2).reshape(g, p, t * hh),
                (-g_im).transpose(0, 3, 1, 2).reshape(g, p, t * hh))

    vf_re, vf_im = state_out(ef_re, ef_im, cr[0], ci[0])
    vr_re, vr_im = state_out(er_re, er_im, cr[1], ci[1])
    v_mat = jnp.concatenate([vf_re, vr_re, vf_im, vr_im], axis=1)
    a_mat = jnp.stack([jnp.concatenate([pw_re[0][:, t], pw_re[1][:, t]], axis=1),
                       jnp.concatenate([pw_im[0][:, t], pw_im[1][:, t]], axis=1)], axis=1)
    return w_mat.astype(BF16), m_mat.astype(BF16), v_mat.astype(BF16), a_mat


def _s5_kernel(uc_ref, ul_ref, w_ref, m_ref, v_ref, a_ref, y_ref, sc_scr, sl_scr, hf_scr, hr_scr, *, n_ctx, n_lat):
    nb = SUBLANES
    half = S5_STATE
    ul = ul_ref[0]
    w = w_ref[0]
    s_c = jnp.dot(uc_ref[0], w, preferred_element_type=F32)
    s_l = jnp.dot(ul, w, preferred_element_type=F32)
    for part in range(2):
        for b in range(nb):
            sc_scr[part, pl.ds(b, n_ctx, stride=nb), :] = s_c[b * n_ctx:(b + 1) * n_ctx, part * LANES:(part + 1) * LANES]
            sl_scr[part, pl.ds(b, n_lat, stride=nb), :] = s_l[b * n_lat:(b + 1) * n_lat, part * LANES:(part + 1) * LANES]
    a = a_ref[0]
    a_re = a[0:1, :]
    a_im = a[1:2, :]
    is_fwd = lax.broadcasted_iota(jnp.int32, (nb, LANES), 1) < half

    def tile(j):
        return pl.ds(pl.multiple_of(j * nb, nb), nb)

    def pick(scr, jf, jr):
        return (jnp.where(is_fwd, scr[0, tile(jf), :], scr[0, tile(jr), :]),
                jnp.where(is_fwd, scr[1, tile(jf), :], scr[1, tile(jr), :]))

    def update(h_re, h_im, s_re, s_im):
        return a_re * h_re - a_im * h_im + s_re, a_re * h_im + a_im * h_re + s_im

    def ctx_step(i, carry):
        return update(*carry, *pick(sc_scr, i, n_ctx - 1 - i))

    def lat_step(c, carry):
        h_re, h_im = carry
        cr = n_lat - 1 - c
        hf_scr[0, tile(c), :] = h_re
        hf_scr[1, tile(c), :] = h_im
        hr_scr[0, tile(cr), :] = h_re
        hr_scr[1, tile(cr), :] = h_im
        return update(h_re, h_im, *pick(sl_scr, c, cr))

    zero = jnp.zeros((nb, LANES), F32)
    carry = lax.fori_loop(0, n_ctx, ctx_step, (zero, zero))
    lax.fori_loop(0, n_lat, lat_step, carry, unroll=2)
    lane = lax.broadcasted_iota(jnp.int32, (n_lat, LANES), 1)

    def entry_states(b):
        rows = pl.ds(b, n_lat, stride=nb)
        return jnp.concatenate([jnp.where(lane < half, hf_scr[part, rows, :], hr_scr[part, rows, :])
                                for part in range(2)], axis=1)

    hin = jnp.concatenate([entry_states(b) for b in range(nb)], axis=0).astype(BF16)
    y = jnp.dot(ul, m_ref[0], preferred_element_type=F32)
    y = y + jnp.dot(hin, v_ref[0], preferred_element_type=F32)
    y_ref[0] = y.astype(BF16)


def _s5(u_ctx, u_lat, w_mat, m_mat, v_mat, a_mat, *, n_ctx, n_lat):
    g, rc, k = u_ctx.shape
    rl = u_lat.shape[1]
    mat_spec = pl.BlockSpec((1, k, k), lambda i: (i, 0, 0))
    return pl.pallas_call(
        functools.partial(_s5_kernel, n_ctx=n_ctx, n_lat=n_lat),
        grid=(g,),
        in_specs=[pl.BlockSpec((1, rc, k), lambda i: (i, 0, 0)),
                  pl.BlockSpec((1, rl, k), lambda i: (i, 0, 0)),
                  mat_spec, mat_spec, mat_spec,
                  pl.BlockSpec((1, 2, LANES), lambda i: (i, 0, 0))],
        out_specs=pl.BlockSpec((1, rl, k), lambda i: (i, 0, 0)),
        out_shape=jax.ShapeDtypeStruct((g, rl, k), BF16),
        scratch_shapes=[pltpu.VMEM((2, rc, LANES), F32), pltpu.VMEM((2, rl, LANES), F32),
                        pltpu.VMEM((2, rl, LANES), F32), pltpu.VMEM((2, rl, LANES), F32)],
        compiler_params=_cparams(("parallel",)),
        name="s5",
    )(u_ctx, u_lat, w_mat, m_mat, v_mat, a_mat)


ATTN_KEY_CHUNK = 256


def _attn_kernel(lam_ref, q_ref, kt_ref, v_ref, sw_ref, o_ref, s_a, s_b, p_a, p_b, *, out_scale):
    lam = lam_ref[0, 0]
    sw = sw_ref[...]
    tq = q_ref.shape[1]
    lk = kt_ref.shape[3]
    kc = ATTN_KEY_CHUNK
    nch = lk // kc
    s_bufs = (s_a, s_b)
    p_bufs = (p_a, p_b)
    lane = lax.broadcasted_iota(jnp.int32, (tq, DA_V_DIM), 1)
    items = [dict() for _ in range(DA_HEADS)]

    def keys(c):
        return slice(c * kc, (c + 1) * kc)

    def cols(h):
        return slice(h * DA_V_DIM, (h + 1) * DA_V_DIM)

    def stage_a(h, c):
        it = items[h]
        if c == 0:
            q = q_ref[0, :, cols(h)]
            zero = jnp.zeros_like(q)
            it['q2'] = jnp.concatenate([jnp.where(lane < DA_HEAD_DIM, q, zero),
                                        jnp.where(lane >= DA_HEAD_DIM, q, zero)], axis=0)
        s = jnp.dot(it['q2'], kt_ref[0, h, :, keys(c)], preferred_element_type=F32)
        s_bufs[h % 2][:, keys(c)] = s
        mx = jnp.maximum(s[:, :LANES], s[:, LANES:])
        it['mx'] = mx if c == 0 else jnp.maximum(it['mx'], mx)
        if c == nch - 1:
            it['m'] = jnp.broadcast_to(jnp.max(it['mx'], axis=-1, keepdims=True), (2 * tq, kc))

    def stage_b(h, c):
        it = items[h]
        p = jnp.exp2(s_bufs[h % 2][:, keys(c)] - it['m'])
        ps = p[:, :LANES] + p[:, LANES:]
        it['ls'] = ps if c == 0 else it['ls'] + ps
        p_bufs[h % 2][:, keys(c)] = p.astype(BF16)
        if c == nch - 1:
            l = jnp.sum(it['ls'], axis=-1, keepdims=True)
            it['r1'] = jnp.broadcast_to((1.0 / l[:tq]).astype(BF16), (tq, kc))
            it['r2'] = jnp.broadcast_to((lam / l[tq:]).astype(BF16), (tq, kc))

    def stage_c(h, c):
        it = items[h]
        pb = p_bufs[h % 2]
        pd = pb[0:tq, keys(c)] * it['r1'] - pb[tq:2 * tq, keys(c)] * it['r2']
        part = jnp.dot(pd, v_ref[0, keys(c), cols(h)], preferred_element_type=F32)
        it['acc'] = part if c == 0 else it['acc'] + part
        if c == nch - 1:
            o_ref[0, :, cols(h)] = (_rms(it['acc']) * sw * out_scale).astype(BF16)

    for slot in range(DA_HEADS + 2):
        for c in range(nch):
            if slot < DA_HEADS:
                stage_a(slot, c)
            if 0 <= slot - 1 < DA_HEADS:
                stage_b(slot - 1, c)
            if 0 <= slot - 2 < DA_HEADS:
                stage_c(slot - 2, c)


def _attn(lam, q, kt, v, subln, *, tq, out_scale):
    b, l, w = q.shape
    lk = v.shape[1]
    assert lk % ATTN_KEY_CHUNK == 0
    return pl.pallas_call(
        functools.partial(_attn_kernel, out_scale=out_scale),
        scratch_shapes=[pltpu.VMEM((2 * tq, lk), F32), pltpu.VMEM((2 * tq, lk), F32),
                        pltpu.VMEM((2 * tq, lk), BF16), pltpu.VMEM((2 * tq, lk), BF16)],
        grid=(b, l // tq),
        in_specs=[pl.BlockSpec(memory_space=pltpu.SMEM),
                  pl.BlockSpec((1, tq, w), lambda i, j: (i, j, 0)),
                  pl.BlockSpec((1, DA_HEADS, DA_V_DIM, lk), lambda i, j: (i, 0, 0, 0)),
                  pl.BlockSpec((1, lk, w), lambda i, j: (i, 0, 0)),
                  pl.BlockSpec((1, DA_V_DIM), lambda i, j: (0, 0))],
        out_specs=pl.BlockSpec((1, tq, w), lambda i, j: (i, j, 0)),
        out_shape=jax.ShapeDtypeStruct((b, l, w), BF16),
        compiler_params=_cparams(("parallel", "parallel")),
        name="attn",
    )(lam, q, kt, v, subln)


def _gelu_tanh(x):
    return 0.5 * x * (1.0 + jnp.tanh(math.sqrt(2.0 / math.pi) * (x + 0.044715 * (x * x * x))))


def _merge_kernel(x_ref, ys_ref, yb_ref, ga_ref, gb_ref, gm_ref, shf_ref, scf_ref,
                  wglu_ref, wpa_ref, wpb_ref, wout_ref, npost_ref, npre_ref, wrh_ref, wrl_ref,
                  x1_ref, h2_ref, aff_ref, afft_ref, ys_scr):
    tm = x_ref.shape[1]
    _chunks_to_tokens(ys_ref, ys_scr, tm)
    ya = _gelu_tanh(jnp.concatenate([ys_scr[gt] for gt in range(ys_scr.shape[0])], axis=1))
    ya = ya * _sigmoid(jnp.dot(ya.astype(BF16), wglu_ref[...], preferred_element_type=F32))
    pa = jnp.dot(ya.astype(BF16), wpa_ref[...], preferred_element_type=F32)
    pb = jnp.dot(yb_ref[0], wpb_ref[...], preferred_element_type=F32)
    mix = ga_ref[0].astype(F32) * pa + gb_ref[0].astype(F32) * pb
    o = jnp.dot(mix.astype(BF16), wout_ref[...], preferred_element_type=F32)
    x1 = x_ref[0] + gm_ref[0] * (_rms(o) * npost_ref[...])
    x1_ref[0] = x1
    h = _rms(x1) * npre_ref[...]
    h = h * (1.0 + scf_ref[0]) + shf_ref[0]
    dk = h.shape[1] // LANES
    for k in range(dk):
        h2_ref[0, pl.ds(k, tm, stride=dk), :] = h[:, k * LANES:(k + 1) * LANES]
    h_hi = h.astype(BF16)
    h_lo = (h - h_hi.astype(F32)).astype(BF16)
    wh = wrh_ref[...]
    logits = (jnp.dot(h_hi, wh, preferred_element_type=F32)
              + jnp.dot(h_lo, wh, preferred_element_type=F32)
              + jnp.dot(h_hi, wrl_ref[...], preferred_element_type=F32))
    lane = lax.broadcasted_iota(jnp.int32, logits.shape, 1)
    logits = jnp.where(lane < N_EXPERTS, logits, -1e30)
    ex = jnp.exp(logits - jnp.max(logits, axis=-1, keepdims=True))
    aff = ex / jnp.sum(ex, axis=-1, keepdims=True)
    aff_ref[0] = aff
    afft_ref[0] = aff.T[:N_EXPERTS, :]


def _merge(x, ys_rows, yb, ga, gb, gm, shf, scf, wglu, wpa, wpb, wout, npost, npre, wrh, wrl, *, tm):
    b, l, d = x.shape
    dk = d // LANES
    groups, _, ck = ys_rows.shape
    nj = l // tm

    def tok(n):
        return pl.BlockSpec((1, tm, n), lambda i, j: (i, j, 0))

    def full(a):
        return pl.BlockSpec(a.shape, lambda i, j: (0,) * a.ndim)

    mod = pl.BlockSpec((1, 1, d), lambda i, j: (i, 0, 0))
    return pl.pallas_call(
        _merge_kernel,
        grid=(b, nj),
        in_specs=[tok(d),
                  pl.BlockSpec((groups, tm // S5_CHUNK, ck), lambda i, j: (0, i * nj + j, 0)),
                  tok(yb.shape[2]), tok(d), tok(d), mod, mod, mod,
                  full(wglu), full(wpa), full(wpb), full(wout), full(npost), full(npre), full(wrh), full(wrl)],
        out_specs=[tok(d),
                   pl.BlockSpec((1, tm * dk, LANES), lambda i, j: (i, j, 0)),
                   tok(LANES),
                   pl.BlockSpec((1, N_EXPERTS, tm), lambda i, j: (i, 0, j))],
        out_shape=[jax.ShapeDtypeStruct((b, l, d), F32),
                   jax.ShapeDtypeStruct((b, l * dk, LANES), F32),
                   jax.ShapeDtypeStruct((b, l, LANES), F32),
                   jax.ShapeDtypeStruct((b, N_EXPERTS, l), F32)],
        scratch_shapes=[pltpu.VMEM((groups * S5_GROUP // LANES, tm, LANES), F32)],
        compiler_params=_cparams(("parallel", "parallel")),
        name="merge",
    )(x, ys_rows, yb, ga, gb, gm, shf, scf, wglu, wpa, wpb, wout, npost, npre, wrh, wrl)


def _route_kernel(a_ref, tok_ref, tri_ref, idx_ref, pos_scr, *, cap):
    a = a_ref[0]
    ne, l = a.shape
    capf = float(cap)

    def count_ge(thr):
        return jnp.sum(jnp.where(a >= thr, 1.0, 0.0), axis=1, keepdims=True)

    def narrow(mid, lo, hi):
        mid = jnp.minimum(jnp.maximum(mid, lo), hi)
        ok = count_ge(mid) >= capf
        return jnp.where(ok, mid, lo), jnp.where(ok, hi, mid)

    def geo(_, lohi):
        lo, hi = lohi
        return narrow(jnp.sqrt(jnp.maximum(lo, F32_TINY) * hi), lo, hi)

    def ari(_, lohi):
        lo, hi = lohi
        return narrow(0.5 * lo + 0.5 * hi, lo, hi)

    lohi = (jnp.zeros((ne, 1), F32), jnp.full((ne, 1), 2.0, F32))
    lohi = lax.fori_loop(0, 34, geo, lohi)
    lo, hi = lax.fori_loop(0, 8, ari, lohi)
    gt = a >= hi
    eq = (a >= lo) & jnp.logical_not(gt)
    need = capf - jnp.sum(jnp.where(gt, 1.0, 0.0), axis=1, keepdims=True)

    tri = tri_ref[...]

    def excl_cumsum(mask):
        mb = jnp.where(mask, 1.0, 0.0).astype(BF16)
        off = jnp.zeros((ne, 1), F32)
        outs = []
        for j in range(l // LANES):
            blk = mb[:, j * LANES:(j + 1) * LANES]
            outs.append(jnp.dot(blk, tri, preferred_element_type=F32) + off)
            off = off + jnp.sum(blk.astype(F32), axis=1, keepdims=True)
        return jnp.concatenate(outs, axis=1)

    sel = gt | (eq & (excl_cumsum(eq) < need))
    pos_scr[...] = jnp.where(sel, excl_cumsum(sel), -1.0)

    tc = 512
    slot = lax.broadcasted_iota(jnp.int32, (cap, tc), 0).astype(F32)

    def per_expert(e, _):
        acc = jnp.zeros((cap, LANES), F32)
        for c in range(l // tc):
            pc = pos_scr[pl.ds(e, 1), c * tc:(c + 1) * tc]
            onehot = jnp.where(pc == slot, 1.0, 0.0).astype(BF16)
            acc = acc + jnp.dot(onehot, tok_ref[c * tc:(c + 1) * tc, :], preferred_element_type=F32)
        acc_t = acc.T
        idx = acc_t[0:1, :] * 64.0 + acc_t[1:2, :]
        idx_ref[0, pl.ds(e, 1), :] = idx.astype(jnp.int32)
        return 0

    lax.fori_loop(0, ne, per_expert, 0)


def _route(aff_t, *, cap):
    b, ne, l = aff_t.shape
    t = jnp.arange(l, dtype=jnp.int32)
    tok = jnp.zeros((l, LANES), F32).at[:, 0].set((t >> 6).astype(F32)).at[:, 1].set((t & 63).astype(F32)).astype(BF16)
    tri = (jnp.arange(LANES)[:, None] < jnp.arange(LANES)[None, :]).astype(BF16)
    return pl.pallas_call(
        functools.partial(_route_kernel, cap=cap),
        grid=(b,),
        in_specs=[pl.BlockSpec((1, ne, l), lambda i: (i, 0, 0)),
                  pl.BlockSpec((l, LANES), lambda i: (0, 0)),
                  pl.BlockSpec((LANES, LANES), lambda i: (0, 0))],
        out_specs=pl.BlockSpec((1, ne, cap), lambda i: (i, 0, 0)),
        out_shape=jax.ShapeDtypeStruct((b, ne, cap), jnp.int32),
        scratch_shapes=[pltpu.VMEM((ne, l), F32)],
        compiler_params=_cparams(("parallel",)),
        name="route",
    )(aff_t, tok, tri)


ROW_UNROLL = 4


def _gather_kernel(idx_ref, h2_ref, xs_ref, xg_scr, *, cap):
    dk = xg_scr.shape[0] // cap

    def gather(s, _):
        t = idx_ref[0, 0, s]
        xg_scr[pl.ds(pl.multiple_of(s * dk, dk), dk), :] = h2_ref[0, pl.ds(pl.multiple_of(t * dk, dk), dk), :]
        return 0

    lax.fori_loop(0, cap, gather, 0, unroll=ROW_UNROLL)
    for k in range(dk):
        xs_ref[0, 0, :, k * LANES:(k + 1) * LANES] = xg_scr[pl.ds(k, cap, stride=dk), :].astype(BF16)


def _moe_gather(idx, h2, *, d):
    b, ne, cap = idx.shape
    rows = h2.shape[1]
    return pl.pallas_call(
        functools.partial(_gather_kernel, cap=cap),
        grid=(b, ne),
        in_specs=[pl.BlockSpec((1, 1, cap), lambda i, e: (i * ne + e, 0, 0), memory_space=pltpu.SMEM),
                  pl.BlockSpec((1, rows, LANES), lambda i, e: (i, 0, 0), pipeline_mode=pl.Buffered(1))],
        out_specs=pl.BlockSpec((1, 1, cap, d), lambda i, e: (i, e, 0, 0)),
        out_shape=jax.ShapeDtypeStruct((b, ne, cap, d), BF16),
        scratch_shapes=[pltpu.VMEM((cap * d // LANES, LANES), F32)],
        compiler_params=_cparams(("arbitrary", "arbitrary")),
        name="moe_gather",
    )(idx.reshape(b * ne, 1, cap), h2)


def _expert_kernel(xs_ref, wg_ref, wu_ref, wd_ref, ys_ref, *, nsplit):
    xs = xs_ref[0, 0]
    tf = wg_ref.shape[2] // nsplit
    y = None
    for j in range(nsplit):
        cols = slice(j * tf, (j + 1) * tf)
        gg = jnp.dot(xs, wg_ref[0, :, cols], preferred_element_type=F32)
        uu = jnp.dot(xs, wu_ref[0, :, cols], preferred_element_type=F32)
        hid = (gg * _sigmoid(gg) * uu).astype(BF16)
        part = jnp.dot(hid, wd_ref[0, cols, :], preferred_element_type=F32)
        y = part if y is None else y + part
    ys_ref[0, 0] = y.astype(BF16)


def _moe_experts(xs, wg, wu, wd):
    b, ne, cap, d = xs.shape
    fdim = wg.shape[2]
    tok = pl.BlockSpec((1, 1, cap, d), lambda e, i: (i, e, 0, 0))
    return pl.pallas_call(
        functools.partial(_expert_kernel, nsplit=2),
        grid=(ne, b),
        in_specs=[tok,
                  pl.BlockSpec((1, d, fdim), lambda e, i: (e, 0, 0)),
                  pl.BlockSpec((1, d, fdim), lambda e, i: (e, 0, 0)),
                  pl.BlockSpec((1, fdim, d), lambda e, i: (e, 0, 0))],
        out_specs=tok,
        out_shape=jax.ShapeDtypeStruct((b, ne, cap, d), BF16),
        compiler_params=_cparams(("arbitrary", "arbitrary")),
        name="moe_experts",
    )(xs, wg, wu, wd)


def _scatter_kernel(idx_ref, ys_ref, aff_ref, f_ref, y2_scr, g_scr, *, cap):
    e = pl.program_id(1)
    dk = y2_scr.shape[0] // cap

    @pl.when(e == 0)
    def _():
        f_ref[...] = jnp.zeros_like(f_ref)

    def gates(s, _):
        g_scr[pl.ds(s, 1), :] = aff_ref[0, pl.ds(idx_ref[0, 0, s], 1), :]
        return 0

    lax.fori_loop(0, cap, gates, 0, unroll=ROW_UNROLL)
    lane = lax.broadcasted_iota(jnp.int32, g_scr.shape, 1)
    gate = jnp.sum(jnp.where(lane == e, g_scr[...], 0.0), axis=1, keepdims=True)
    yg = ys_ref[0, 0].astype(F32) * gate
    for k in range(dk):
        y2_scr[pl.ds(k, cap, stride=dk), :] = yg[:, k * LANES:(k + 1) * LANES]

    def scatter(s0, _):
        rows = []
        vals = []
        for i in range(ROW_UNROLL):
            s = s0 * ROW_UNROLL + i
            r = pl.multiple_of(idx_ref[0, 0, s] * dk, dk)
            rows.append(r)
            vals.append(f_ref[0, pl.ds(r, dk), :] + y2_scr[pl.ds(pl.multiple_of(s * dk, dk), dk), :])
        for r, v in zip(rows, vals):
            f_ref[0, pl.ds(r, dk), :] = v
        return 0

    lax.fori_loop(0, cap // ROW_UNROLL, scatter, 0)


def _moe_scatter(idx, ys, aff):
    b, ne, cap, d = ys.shape
    l = aff.shape[1]
    dk = d // LANES
    return pl.pallas_call(
        functools.partial(_scatter_kernel, cap=cap),
        grid=(b, ne),
        in_specs=[pl.BlockSpec((1, 1, cap), lambda i, e: (i * ne + e, 0, 0), memory_space=pltpu.SMEM),
                  pl.BlockSpec((1, 1, cap, d), lambda i, e: (i, e, 0, 0)),
                  pl.BlockSpec((1, l, LANES), lambda i, e: (i, 0, 0))],
        out_specs=pl.BlockSpec((1, l * dk, LANES), lambda i, e: (i, 0, 0), pipeline_mode=pl.Buffered(1)),
        out_shape=jax.ShapeDtypeStruct((b, l * dk, LANES), F32),
        scratch_shapes=[pltpu.VMEM((cap * dk, LANES), F32), pltpu.VMEM((cap, LANES), F32)],
        compiler_params=_cparams(("arbitrary", "arbitrary")),
        name="moe_scatter",
    )(idx.reshape(b * ne, 1, cap), ys, aff)


def _final_kernel(f_ref, x1_ref, gf_ref, nw_ref, o_ref):
    tm, d = x1_ref.shape[1:]
    dk = d // LANES
    f = jnp.concatenate([f_ref[0, pl.ds(k, tm, stride=dk), :] for k in range(dk)], axis=1)
    o_ref[0] = x1_ref[0] + gf_ref[0] * (_rms(f) * nw_ref[...])


def _final(f, x1, gf, nw, *, tm):
    b, l, d = x1.shape
    dk = d // LANES
    return pl.pallas_call(
        _final_kernel,
        grid=(b, l // tm),
        in_specs=[pl.BlockSpec((1, tm * dk, LANES), lambda i, j: (i, j, 0)),
                  pl.BlockSpec((1, tm, d), lambda i, j: (i, j, 0)),
                  pl.BlockSpec((1, 1, d), lambda i, j: (i, 0, 0)),
                  pl.BlockSpec((1, d), lambda i, j: (0, 0))],
        out_specs=pl.BlockSpec((1, tm, d), lambda i, j: (i, j, 0)),
        out_shape=jax.ShapeDtypeStruct((b, l, d), F32),
        compiler_params=_cparams(("parallel", "parallel")),
        name="final",
    )(f, x1, gf, nw)


def kernel(x, c, ctx, c_ctx, w_ada, b_ada, norm_pre_mix, norm_post_mix, norm_pre_ffn, norm_post_ffn, w_in, s5_lam_re, s5_lam_im, s5_log_dt, s5_b_re, s5_b_im, s5_c_re, s5_c_im, s5_d, w_glu, da_lambda, da_subln, w_proj_a, w_proj_b, w_out, w_router, w_exp_gate, w_exp_up, w_exp_down):
    depth = w_ada.shape[0]
    assert depth == 1, "single trunk layer: the context stream's outputs are never consumed"
    b, l, d = x.shape
    lc = ctx.shape[1]
    assert b == SUBLANES and l % (S5_CHUNK * 32) == 0 and lc % (S5_CHUNK * 16) == 0
    s5w = s5_d.shape[1]
    qkw = DA_HEADS * 2 * DA_HEAD_DIM
    vw = DA_HEADS * DA_V_DIM
    widths = (s5w, qkw, vw, d)
    lam_init = 0.8 - 0.6 * math.exp(-0.3 * 0)

    c_all = jnp.zeros((2 * SUBLANES, d), F32).at[:b].set(c).at[b].set(c_ctx)
    mod = _ada(c_all, w_ada[0], b_ada[0])
    sh_m, sc_m, g_m, sh_f, sc_f, g_f = [mod[:b, i * d:(i + 1) * d].reshape(b, 1, d) for i in range(6)]
    csh_m = mod[b:b + 1, 0:d].reshape(1, 1, d)
    csc_m = mod[b:b + 1, d:2 * d].reshape(1, 1, d)

    w_in_b = w_in[0].astype(BF16)
    cos_t, sin_t = _rope_tables(l)
    npm = norm_pre_mix[0].reshape(1, d)
    u, q, k, v, ga, gb = _inproj(x, npm, sh_m, sc_m, w_in_b, cos_t, sin_t, latent=True, widths=widths, tm=512)
    uc, kc, vc = _inproj(ctx, npm, csh_m, csc_m, w_in_b, cos_t[:lc], sin_t[:lc], latent=False, widths=widths, tm=lc)

    w_mat, m_mat, v_mat, a_mat = _s5_matrices(s5_lam_re[0], s5_lam_im[0], s5_log_dt[0], s5_b_re[0], s5_b_im[0],
                                              s5_c_re[0], s5_c_im[0], s5_d[0])
    ys_rows = _s5(uc, u, w_mat, m_mat, v_mat, a_mat, n_ctx=lc // S5_CHUNK, n_lat=l // S5_CHUNK)

    lq1, lk1, lq2, lk2 = da_lambda[0].astype(F32)
    lam = jnp.exp(jnp.sum(lq1 * lk1)) - jnp.exp(jnp.sum(lq2 * lk2)) + lam_init
    k_all = jnp.concatenate([kc, k], axis=1)
    kt = k_all.reshape(b, lc + l, DA_HEADS, DA_V_DIM).transpose(0, 2, 3, 1)
    v_all = jnp.concatenate([vc, v], axis=1)
    yb = _attn(lam.reshape(1, 1), q, kt, v_all, da_subln[0].reshape(1, DA_V_DIM), tq=128, out_scale=1.0 - lam_init)

    wr = jnp.zeros((d, LANES), F32).at[:, :N_EXPERTS].set(w_router[0])
    wr_hi = wr.astype(BF16)
    wr_lo = (wr - wr_hi.astype(F32)).astype(BF16)
    x1, h2, aff, aff_t = _merge(x, ys_rows, yb, ga, gb, g_m, sh_f, sc_f,
                                w_glu[0].astype(BF16), w_proj_a[0].astype(BF16), w_proj_b[0].astype(BF16),
                                w_out[0].astype(BF16), norm_post_mix[0].reshape(1, d), norm_pre_ffn[0].reshape(1, d),
                                wr_hi, wr_lo, tm=512)

    cap = EC_CAPACITY * l // N_EXPERTS
    idx = _route(aff_t, cap=cap)
    xs = _moe_gather(idx, h2, d=d)
    ys = _moe_experts(xs, w_exp_gate[0].astype(BF16), w_exp_up[0].astype(BF16), w_exp_down[0].astype(BF16))
    f = _moe_scatter(idx, ys, aff)
    return _final(f, x1, g_f, norm_post_ffn[0].reshape(1, d), tm=512)
```

```python
import functools
import math

import jax
import jax.numpy as jnp
from jax import lax
from jax.experimental import pallas as pl
from jax.experimental.pallas import tpu as pltpu

F32 = jnp.float32
BF16 = jnp.bfloat16

EPS = 1e-6
GRID_W = 64
ROPE_THETA = 10000.0
S5_GROUP = 16
S5_STATE = 64
S5_DT_MAX_RE = -1e-4
S5_CHUNK = 16
DA_HEADS = 4
DA_HEAD_DIM = 64
DA_V_DIM = 128
N_EXPERTS = 16
EC_CAPACITY = 2
LANES = 128
SUBLANES = 8
VMEM_LIMIT = 60 * 1024 * 1024
LOG2E = 1.4426950408889634
F32_TINY = 1e-37


def _cparams(sem):
    return pltpu.CompilerParams(dimension_semantics=sem, vmem_limit_bytes=VMEM_LIMIT)


def _rms(x, eps=EPS):
    return x * lax.rsqrt(jnp.mean(x * x, axis=-1, keepdims=True) + eps)


def _sigmoid(x):
    return 1.0 / (1.0 + jnp.exp(-x))


def _ada_kernel(c_ref, w_ref, b_ref, o_ref):
    c = c_ref[...]
    s = (c * _sigmoid(c)).astype(BF16)
    o_ref[...] = jnp.dot(s, w_ref[...].astype(BF16), preferred_element_type=F32) + b_ref[...]


def _ada(c_all, w_ada, b_ada):
    rows, d = c_all.shape
    n = w_ada.shape[1]
    tn = 512
    return pl.pallas_call(
        _ada_kernel,
        grid=(n // tn,),
        in_specs=[pl.BlockSpec((rows, d), lambda j: (0, 0)),
                  pl.BlockSpec((d, tn), lambda j: (0, j)),
                  pl.BlockSpec((1, tn), lambda j: (0, j))],
        out_specs=pl.BlockSpec((rows, tn), lambda j: (0, j)),
        out_shape=jax.ShapeDtypeStruct((rows, n), F32),
        compiler_params=_cparams(("arbitrary",)),
        name="ada",
    )(c_all, w_ada, b_ada.reshape(1, n))


def _block_transpose8(vs):
    lane_blk = lax.broadcasted_iota(jnp.int32, vs[0].shape, 1) // S5_GROUP
    vs = list(vs)
    for s in (4, 2, 1):
        upper = (lane_blk & s) != 0
        for i in range(8):
            if i & s:
                continue
            a, b = vs[i], vs[i + s]
            vs[i] = jnp.where(upper, pltpu.roll(b, S5_GROUP * s, 1), a)
            vs[i + s] = jnp.where(upper, b, pltpu.roll(a, LANES - S5_GROUP * s, 1))
    return vs


def _tokens_to_chunks(scr, out_ref, tm):
    nc = tm // S5_CHUNK
    for gt in range(scr.shape[0]):
        for j in range(S5_CHUNK // 8):
            vs = [scr[gt, pl.ds(8 * j + i, nc, stride=S5_CHUNK), :] for i in range(8)]
            vs = _block_transpose8(vs)
            for gi in range(8):
                out_ref[gt * 8 + gi, :, j * LANES:(j + 1) * LANES] = vs[gi].astype(out_ref.dtype)


def _chunks_to_tokens(in_ref, scr, row0, nrows):
    nc = nrows // S5_CHUNK
    c0 = row0 // S5_CHUNK
    for gt in range(scr.shape[0]):
        for j in range(S5_CHUNK // 8):
            vs = [in_ref[gt * 8 + gi, c0:c0 + nc, j * LANES:(j + 1) * LANES].astype(F32) for gi in range(8)]
            vs = _block_transpose8(vs)
            for i in range(8):
                scr[gt, pl.ds(row0 + 8 * j + i, nc, stride=S5_CHUNK), :] = vs[i]


def _swap16(x):
    lane = lax.broadcasted_iota(jnp.int32, x.shape, 1)
    return jnp.where((lane & 16) == 0, pltpu.roll(x, LANES - 16, 1), pltpu.roll(x, 16, 1))


def _inproj_kernel(x_ref, nw_ref, sh_ref, sc_ref, w_ref, cos_ref, sin_ref, *refs, latent, widths):
    x = x_ref[0]
    tm = x.shape[0]
    h = _rms(x) * nw_ref[...]
    h = (h * (1.0 + sc_ref[0]) + sh_ref[0]).astype(BF16)
    s5w, qkw, vw, dm = widths
    o_u, o_q, o_k, o_v, o_ga = 0, s5w, s5w + qkw, s5w + 2 * qkw, s5w + 2 * qkw + vw

    def proj(lo, n):
        return jnp.dot(h, w_ref[:, lo:lo + n], preferred_element_type=F32)

    def rope(z, scale):
        cos = cos_ref[...]
        sin = sin_ref[...]
        parts = []
        for j in range(z.shape[1] // LANES):
            zj = z[:, j * LANES:(j + 1) * LANES]
            parts.append((zj * cos + _swap16(zj) * sin) * scale)
        return jnp.concatenate(parts, axis=1)

    if latent:
        u_ref, q_ref, k_ref, v_ref, ga_ref, gb_ref, u_scr = refs
        q_ref[0] = rope(proj(o_q, qkw), DA_HEAD_DIM ** -0.5 * LOG2E).astype(BF16)
        k_ref[0] = rope(proj(o_k, qkw), 1.0).astype(BF16)
        v_ref[0] = proj(o_v, vw).astype(BF16)
        ga_ref[0] = _sigmoid(proj(o_ga, dm)).astype(BF16)
        gb_ref[0] = _sigmoid(proj(o_ga + dm, dm)).astype(BF16)
    else:
        u_ref, k_ref, v_ref, u_scr = refs
        k_ref[0] = proj(o_k, qkw).astype(BF16)
        v_ref[0] = proj(o_v, vw).astype(BF16)
    u = proj(o_u, s5w)
    for gt in range(s5w // LANES):
        u_scr[gt] = u[:, gt * LANES:(gt + 1) * LANES]
    _tokens_to_chunks(u_scr, u_ref, tm)


def _inproj(x, nw, sh, sc, w_in, cos_t, sin_t, *, latent, widths, tm):
    b, l, d = x.shape
    s5w, qkw, vw, dm = widths
    groups = s5w // S5_GROUP
    nc = tm // S5_CHUNK
    per_b = sh.shape[0] > 1
    mod_spec = pl.BlockSpec((1, 1, d), (lambda i, j: (i, 0, 0)) if per_b else (lambda i, j: (0, 0, 0)))

    def tok_spec(n):
        return pl.BlockSpec((1, tm, n), lambda i, j: (i, j, 0))

    nj = l // tm
    u_spec = pl.BlockSpec((groups, nc, S5_CHUNK * S5_GROUP), lambda i, j: (0, i * nj + j, 0))
    u_shape = jax.ShapeDtypeStruct((groups, b * (l // S5_CHUNK), S5_CHUNK * S5_GROUP), BF16)
    out_w = (qkw, qkw, vw, dm, dm) if latent else (qkw, vw)
    return pl.pallas_call(
        functools.partial(_inproj_kernel, latent=latent, widths=widths),
        grid=(b, nj),
        in_specs=[tok_spec(d),
                  pl.BlockSpec((1, d), lambda i, j: (0, 0)),
                  mod_spec, mod_spec,
                  pl.BlockSpec(w_in.shape, lambda i, j: (0, 0)),
                  pl.BlockSpec((tm, LANES), lambda i, j: (j, 0)),
                  pl.BlockSpec((tm, LANES), lambda i, j: (j, 0))],
        out_specs=[u_spec] + [tok_spec(n) for n in out_w],
        out_shape=[u_shape] + [jax.ShapeDtypeStruct((b, l, n), BF16) for n in out_w],
        scratch_shapes=[pltpu.VMEM((s5w // LANES, tm, LANES), F32)],
        compiler_params=_cparams(("parallel", "parallel")),
        name="inproj_lat" if latent else "inproj_ctx",
    )(x, nw, sh, sc, w_in, cos_t, sin_t)


def _rope_tables(seq_len):
    rows = seq_len // GRID_W
    pairs = DA_HEAD_DIM // 4
    row = jnp.repeat(jnp.arange(rows), GRID_W).astype(F32)
    col = jnp.tile(jnp.arange(GRID_W), rows).astype(F32)
    inv_freq = ROPE_THETA ** (-jnp.arange(pairs, dtype=F32) / pairs)
    ra = row[:, None] * inv_freq[None, :]
    ca = col[:, None] * inv_freq[None, :]
    cos64 = jnp.concatenate([jnp.cos(ra), jnp.cos(ra), jnp.cos(ca), jnp.cos(ca)], axis=1)
    sin64 = jnp.concatenate([-jnp.sin(ra), jnp.sin(ra), -jnp.sin(ca), jnp.sin(ca)], axis=1)
    return jnp.tile(cos64, (1, 2)), jnp.tile(sin64, (1, 2))


def _s5_matrices(lam_re, lam_im, log_dt, b_re, b_im, c_re, c_im, d_skip):
    hp = lax.Precision.HIGHEST
    t = S5_CHUNK
    lam_re = jnp.minimum(lam_re.astype(F32), S5_DT_MAX_RE)
    lam_im = lam_im.astype(F32)
    dt = jnp.exp(log_dt.astype(F32))[..., None]
    g, p = lam_re.shape[1:]
    hh = S5_GROUP
    mag = jnp.exp(lam_re * dt)
    lb_re = mag * jnp.cos(lam_im * dt)
    lb_im = mag * jnp.sin(lam_im * dt)
    den = lam_re * lam_re + lam_im * lam_im
    num_re = lb_re - 1.0
    co_re = (num_re * lam_re + lb_im * lam_im) / den
    co_im = (lb_im * lam_re - num_re * lam_im) / den
    br = b_re.astype(F32)
    bi = b_im.astype(F32)
    bb_re = co_re[..., None] * br - co_im[..., None] * bi
    bb_im = co_re[..., None] * bi + co_im[..., None] * br
    j = jnp.arange(t + 1, dtype=F32)[None, None, :, None]
    pmag = jnp.exp(lam_re[:, :, None, :] * dt[:, :, None, :] * j)
    pang = lam_im[:, :, None, :] * dt[:, :, None, :] * j
    pw_re = pmag * jnp.cos(pang)
    pw_im = pmag * jnp.sin(pang)
    cr = c_re.astype(F32)
    ci = c_im.astype(F32)
    cp_re = cr[:, :, None] * pw_re[:, :, :t, None, :] - ci[:, :, None] * pw_im[:, :, :t, None, :]
    cp_im = cr[:, :, None] * pw_im[:, :, :t, None, :] + ci[:, :, None] * pw_re[:, :, :t, None, :]
    taps = (jnp.einsum('dgjhp,dgpi->dgjhi', cp_re, bb_re, precision=hp)
            - jnp.einsum('dgjhp,dgpi->dgjhi', cp_im, bb_im, precision=hp))
    sig = jnp.arange(t)[:, None]
    tau = jnp.arange(t)[None, :]
    kf = jnp.where((tau >= sig)[None, :, :, None, None], taps[0][:, jnp.clip(tau - sig, 0, t - 1)], 0.0)
    kr = jnp.where((sig >= tau)[None, :, :, None, None], taps[1][:, jnp.clip(sig - tau, 0, t - 1)], 0.0)
    skip = d_skip.astype(F32).reshape(g, hh)
    eye_t = jnp.eye(t, dtype=F32)[None, :, :, None, None]
    eye_h = jnp.eye(hh, dtype=F32)[None, None, None] * skip[:, None, None, :, None]
    m_full = kf + kr + eye_t * eye_h
    m_mat = m_full.transpose(0, 1, 4, 2, 3).reshape(g, t * hh, t * hh)
    pf_re = pw_re[0][:, t - 1 - jnp.arange(t)]
    pf_im = pw_im[0][:, t - 1 - jnp.arange(t)]
    pr_re = pw_re[1][:, :t]
    pr_im = pw_im[1][:, :t]

    def state_in(pr_, pi_, br_, bi_):
        brt = br_.transpose(0, 2, 1)[:, None]
        bit = bi_.transpose(0, 2, 1)[:, None]
        re = pr_[:, :, None, :] * brt - pi_[:, :, None, :] * bit
        im = pr_[:, :, None, :] * bit + pi_[:, :, None, :] * brt
        return re.reshape(g, t * hh, p), im.reshape(g, t * hh, p)

    wf_re, wf_im = state_in(pf_re, pf_im, bb_re[0], bb_im[0])
    wr_re, wr_im = state_in(pr_re, pr_im, bb_re[1], bb_im[1])
    w_mat = jnp.concatenate([wf_re, wr_re, wf_im, wr_im], axis=2)
    ef_re = pw_re[0][:, 1:t + 1]
    ef_im = pw_im[0][:, 1:t + 1]
    er_re = pw_re[1][:, t - jnp.arange(t)]
    er_im = pw_im[1][:, t - jnp.arange(t)]

    def state_out(e_re, e_im, cr_, ci_):
        g_re = cr_[:, None] * e_re[:, :, None, :] - ci_[:, None] * e_im[:, :, None, :]
        g_im = cr_[:, None] * e_im[:, :, None, :] + ci_[:, None] * e_re[:, :, None, :]
        return (g_re.transpose(0, 3, 1, 2).reshape(g, p, t * hh),
                (-g_im).transpose(0, 3, 1, 2).reshape(g, p, t * hh))

    vf_re, vf_im = state_out(ef_re, ef_im, cr[0], ci[0])
    vr_re, vr_im = state_out(er_re, er_im, cr[1], ci[1])
    v_mat = jnp.concatenate([vf_re, vr_re, vf_im, vr_im], axis=1)
    a_mat = jnp.stack([jnp.concatenate([pw_re[0][:, t], pw_re[1][:, t]], axis=1),
                       jnp.concatenate([pw_im[0][:, t], pw_im[1][:, t]], axis=1)], axis=1)
    return w_mat.astype(BF16), m_mat.astype(BF16), v_mat.astype(BF16), a_mat


def _s5_kernel(uc_ref, ul_ref, w_ref, m_ref, v_ref, a_ref, y_ref, sc_scr, sl_scr, hf_scr, hr_scr, *, n_ctx, n_lat):
    nb = SUBLANES
    half = S5_STATE
    ul = ul_ref[0]
    w = w_ref[0]
    s_c = jnp.dot(uc_ref[0], w, preferred_element_type=F32)
    s_l = jnp.dot(ul, w, preferred_element_type=F32)
    for part in range(2):
        for b in range(nb):
            sc_scr[part, pl.ds(b, n_ctx, stride=nb), :] = s_c[b * n_ctx:(b + 1) * n_ctx, part * LANES:(part + 1) * LANES]
            sl_scr[part, pl.ds(b, n_lat, stride=nb), :] = s_l[b * n_lat:(b + 1) * n_lat, part * LANES:(part + 1) * LANES]
    a = a_ref[0]
    a_re = a[0:1, :]
    a_im = a[1:2, :]
    is_fwd = lax.broadcasted_iota(jnp.int32, (nb, LANES), 1) < half

    def tile(j):
        return pl.ds(pl.multiple_of(j * nb, nb), nb)

    def pick(scr, jf, jr):
        return (jnp.where(is_fwd, scr[0, tile(jf), :], scr[0, tile(jr), :]),
                jnp.where(is_fwd, scr[1, tile(jf), :], scr[1, tile(jr), :]))

    def update(h_re, h_im, s_re, s_im):
        return a_re * h_re - a_im * h_im + s_re, a_re * h_im + a_im * h_re + s_im

    def ctx_step(i, carry):
        return update(*carry, *pick(sc_scr, i, n_ctx - 1 - i))

    def lat_step(c, carry):
        h_re, h_im = carry
        cr = n_lat - 1 - c
        hf_scr[0, tile(c), :] = h_re
        hf_scr[1, tile(c), :] = h_im
        hr_scr[0, tile(cr), :] = h_re
        hr_scr[1, tile(cr), :] = h_im
        return update(h_re, h_im, *pick(sl_scr, c, cr))

    zero = jnp.zeros((nb, LANES), F32)
    carry = lax.fori_loop(0, n_ctx, ctx_step, (zero, zero))
    lax.fori_loop(0, n_lat, lat_step, carry, unroll=2)
    lane = lax.broadcasted_iota(jnp.int32, (n_lat, LANES), 1)

    def entry_states(b):
        rows = pl.ds(b, n_lat, stride=nb)
        return jnp.concatenate([jnp.where(lane < half, hf_scr[part, rows, :], hr_scr[part, rows, :])
                                for part in range(2)], axis=1)

    hin = jnp.concatenate([entry_states(b) for b in range(nb)], axis=0).astype(BF16)
    y = jnp.dot(ul, m_ref[0], preferred_element_type=F32)
    y = y + jnp.dot(hin, v_ref[0], preferred_element_type=F32)
    y_ref[0] = y.astype(BF16)


def _s5(u_ctx, u_lat, w_mat, m_mat, v_mat, a_mat, *, n_ctx, n_lat):
    g, rc, k = u_ctx.shape
    rl = u_lat.shape[1]
    mat_spec = pl.BlockSpec((1, k, k), lambda i: (i, 0, 0))
    return pl.pallas_call(
        functools.partial(_s5_kernel, n_ctx=n_ctx, n_lat=n_lat),
        grid=(g,),
        in_specs=[pl.BlockSpec((1, rc, k), lambda i: (i, 0, 0)),
                  pl.BlockSpec((1, rl, k), lambda i: (i, 0, 0)),
                  mat_spec, mat_spec, mat_spec,
                  pl.BlockSpec((1, 2, LANES), lambda i: (i, 0, 0))],
        out_specs=pl.BlockSpec((1, rl, k), lambda i: (i, 0, 0)),
        out_shape=jax.ShapeDtypeStruct((g, rl, k), BF16),
        scratch_shapes=[pltpu.VMEM((2, rc, LANES), F32), pltpu.VMEM((2, rl, LANES), F32),
                        pltpu.VMEM((2, rl, LANES), F32), pltpu.VMEM((2, rl, LANES), F32)],
        compiler_params=_cparams(("parallel",)),
        name="s5",
    )(u_ctx, u_lat, w_mat, m_mat, v_mat, a_mat)


ATTN_KEY_CHUNK = 256
ATTN_ITEM_ROWS = 128


def _attn_kernel(lam_ref, q_ref, kt_ref, v_ref, sw_ref, o_ref, s_a, s_b, p_a, p_b, *, out_scale):
    lam = lam_ref[0, 0]
    sw = sw_ref[...]
    tr = ATTN_ITEM_ROWS
    lk = kt_ref.shape[3]
    kc = ATTN_KEY_CHUNK
    nch = lk // kc
    s_bufs = (s_a, s_b)
    p_bufs = (p_a, p_b)
    lane = lax.broadcasted_iota(jnp.int32, (tr, DA_V_DIM), 1)
    work = [(r, h) for r in range(q_ref.shape[1] // tr) for h in range(DA_HEADS)]
    items = [dict() for _ in work]

    def keys(c):
        return slice(c * kc, (c + 1) * kc)

    def rows(i):
        return slice(work[i][0] * tr, (work[i][0] + 1) * tr)

    def cols(i):
        return slice(work[i][1] * DA_V_DIM, (work[i][1] + 1) * DA_V_DIM)

    def stage_a(i, c):
        it = items[i]
        if c == 0:
            q = q_ref[0, rows(i), cols(i)]
            zero = jnp.zeros_like(q)
            it['q2'] = jnp.concatenate([jnp.where(lane < DA_HEAD_DIM, q, zero),
                                        jnp.where(lane >= DA_HEAD_DIM, q, zero)], axis=0)
        s = jnp.dot(it['q2'], kt_ref[0, work[i][1], :, keys(c)], preferred_element_type=F32)
        s_bufs[i % 2][:, keys(c)] = s
        mx = jnp.maximum(s[:, :LANES], s[:, LANES:])
        it['mx'] = mx if c == 0 else jnp.maximum(it['mx'], mx)
        if c == nch - 1:
            it['m'] = jnp.broadcast_to(jnp.max(it['mx'], axis=-1, keepdims=True), (2 * tr, kc))

    def stage_b(i, c):
        it = items[i]
        p = jnp.exp2(s_bufs[i % 2][:, keys(c)] - it['m'])
        ps = p[:, :LANES] + p[:, LANES:]
        it['ls'] = ps if c == 0 else it['ls'] + ps
        p_bufs[i % 2][:, keys(c)] = p.astype(BF16)
        if c == nch - 1:
            l = jnp.sum(it['ls'], axis=-1, keepdims=True)
            it['r1'] = jnp.broadcast_to((1.0 / l[:tr]).astype(BF16), (tr, kc))
            it['r2'] = jnp.broadcast_to((lam / l[tr:]).astype(BF16), (tr, kc))

    def stage_c(i, c):
        it = items[i]
        pb = p_bufs[i % 2]
        pd = pb[0:tr, keys(c)] * it['r1'] - pb[tr:2 * tr, keys(c)] * it['r2']
        part = jnp.dot(pd, v_ref[0, keys(c), cols(i)], preferred_element_type=F32)
        it['acc'] = part if c == 0 else it['acc'] + part
        if c == nch - 1:
            o_ref[0, rows(i), cols(i)] = (_rms(it['acc']) * sw * out_scale).astype(BF16)

    n = len(work)
    for slot in range(n + 2):
        for c in range(nch):
            if slot < n:
                stage_a(slot, c)
            if 0 <= slot - 1 < n:
                stage_b(slot - 1, c)
            if 0 <= slot - 2 < n:
                stage_c(slot - 2, c)


def _attn(lam, q, kt, v, subln, *, tq, out_scale):
    b, l, w = q.shape
    lk = v.shape[1]
    assert lk % ATTN_KEY_CHUNK == 0 and tq % ATTN_ITEM_ROWS == 0
    tr = ATTN_ITEM_ROWS
    return pl.pallas_call(
        functools.partial(_attn_kernel, out_scale=out_scale),
        scratch_shapes=[pltpu.VMEM((2 * tr, lk), F32), pltpu.VMEM((2 * tr, lk), F32),
                        pltpu.VMEM((2 * tr, lk), BF16), pltpu.VMEM((2 * tr, lk), BF16)],
        grid=(b, l // tq),
        in_specs=[pl.BlockSpec(memory_space=pltpu.SMEM),
                  pl.BlockSpec((1, tq, w), lambda i, j: (i, j, 0)),
                  pl.BlockSpec((1, DA_HEADS, DA_V_DIM, lk), lambda i, j: (i, 0, 0, 0)),
                  pl.BlockSpec((1, lk, w), lambda i, j: (i, 0, 0)),
                  pl.BlockSpec((1, DA_V_DIM), lambda i, j: (0, 0))],
        out_specs=pl.BlockSpec((1, tq, w), lambda i, j: (i, j, 0)),
        out_shape=jax.ShapeDtypeStruct((b, l, w), BF16),
        compiler_params=_cparams(("parallel", "parallel")),
        name="attn",
    )(lam, q, kt, v, subln)


def _gelu_tanh(x):
    return 0.5 * x * (1.0 + jnp.tanh(math.sqrt(2.0 / math.pi) * (x + 0.044715 * (x * x * x))))


def _merge_kernel(x_ref, ys_ref, yb_ref, ga_ref, gb_ref, gm_ref, shf_ref, scf_ref,
                  wglu_ref, wpa_ref, wpb_ref, wout_ref, npost_ref, npre_ref, wr2_ref,
                  x1_ref, h2_ref, aff_ref, afft_ref, ys_scr):
    tm, d = x_ref.shape[1:]
    dk = d // LANES
    nr = tm // 2
    state = [dict(), dict()]

    def rows(hf):
        return slice(hf * nr, (hf + 1) * nr)

    def st_relayout(hf):
        _chunks_to_tokens(ys_ref, ys_scr, hf * nr, nr)

    def st_glu(hf):
        st = state[hf]
        st['ya'] = _gelu_tanh(jnp.concatenate([ys_scr[gt, rows(hf), :] for gt in range(ys_scr.shape[0])], axis=1))
        st['glu'] = jnp.dot(st['ya'].astype(BF16), wglu_ref[...], preferred_element_type=F32)

    def st_proj(hf):
        st = state[hf]
        ya = st['ya'] * _sigmoid(st['glu'])
        st['pa'] = jnp.dot(ya.astype(BF16), wpa_ref[...], preferred_element_type=F32)
        st['pb'] = jnp.dot(yb_ref[0, rows(hf), :], wpb_ref[...], preferred_element_type=F32)

    def st_out(hf):
        st = state[hf]
        mix = ga_ref[0, rows(hf), :].astype(F32) * st['pa'] + gb_ref[0, rows(hf), :].astype(F32) * st['pb']
        st['o'] = jnp.dot(mix.astype(BF16), wout_ref[...], preferred_element_type=F32)

    def st_residual(hf):
        st = state[hf]
        x1 = x_ref[0, rows(hf), :] + gm_ref[0] * (_rms(st['o']) * npost_ref[...])
        x1_ref[0, rows(hf), :] = x1
        h = _rms(x1) * npre_ref[...]
        h = h * (1.0 + scf_ref[0]) + shf_ref[0]
        for k in range(dk):
            h2_ref[0, pl.ds(hf * nr * dk + k, nr, stride=dk), :] = h[:, k * LANES:(k + 1) * LANES]
        st['h'] = h

    def st_router(hf):
        st = state[hf]
        h = st['h']
        h_hi = h.astype(BF16)
        h_lo = (h - h_hi.astype(F32)).astype(BF16)
        st['lg'] = jnp.dot(jnp.concatenate([h_hi, h_lo], axis=1), wr2_ref[...], preferred_element_type=F32)

    def st_softmax(hf):
        lg = state[hf]['lg']
        logits = lg[:, :LANES] + lg[:, LANES:]
        lane = lax.broadcasted_iota(jnp.int32, logits.shape, 1)
        logits = jnp.where(lane < N_EXPERTS, logits, -1e30)
        ex = jnp.exp(logits - jnp.max(logits, axis=-1, keepdims=True))
        aff = ex / jnp.sum(ex, axis=-1, keepdims=True)
        aff_ref[0, rows(hf), :] = aff
        afft_ref[0, :, rows(hf)] = aff.T[:N_EXPERTS, :]

    stages = (st_relayout, st_glu, st_proj, st_out, st_residual, st_router, st_softmax)
    for k in range(len(stages) + 1):
        if k < len(stages):
            stages[k](0)
        if k >= 1:
            stages[k - 1](1)


def _merge(x, ys_rows, yb, ga, gb, gm, shf, scf, wglu, wpa, wpb, wout, npost, npre, wr2, *, tm):
    b, l, d = x.shape
    dk = d // LANES
    groups, _, ck = ys_rows.shape
    nj = l // tm

    def tok(n):
        return pl.BlockSpec((1, tm, n), lambda i, j: (i, j, 0))

    def full(a):
        return pl.BlockSpec(a.shape, lambda i, j: (0,) * a.ndim)

    mod = pl.BlockSpec((1, 1, d), lambda i, j: (i, 0, 0))
    return pl.pallas_call(
        _merge_kernel,
        grid=(b, nj),
        in_specs=[tok(d),
                  pl.BlockSpec((groups, tm // S5_CHUNK, ck), lambda i, j: (0, i * nj + j, 0)),
                  tok(yb.shape[2]), tok(d), tok(d), mod, mod, mod,
                  full(wglu), full(wpa), full(wpb), full(wout), full(npost), full(npre), full(wr2)],
        out_specs=[tok(d),
                   pl.BlockSpec((1, tm * dk, LANES), lambda i, j: (i, j, 0)),
                   tok(LANES),
                   pl.BlockSpec((1, N_EXPERTS, tm), lambda i, j: (i, 0, j))],
        out_shape=[jax.ShapeDtypeStruct((b, l, d), F32),
                   jax.ShapeDtypeStruct((b, l * dk, LANES), F32),
                   jax.ShapeDtypeStruct((b, l, LANES), F32),
                   jax.ShapeDtypeStruct((b, N_EXPERTS, l), F32)],
        scratch_shapes=[pltpu.VMEM((groups * S5_GROUP // LANES, tm, LANES), F32)],
        compiler_params=_cparams(("parallel", "parallel")),
        name="merge",
    )(x, ys_rows, yb, ga, gb, gm, shf, scf, wglu, wpa, wpb, wout, npost, npre, wr2)


def _route_kernel(a_ref, tok_ref, tri_ref, idx_ref, pos_scr, *, cap):
    a = a_ref[0]
    ne, l = a.shape
    capf = float(cap)

    def count_ge(thr):
        return jnp.sum(jnp.where(a >= thr, 1.0, 0.0), axis=1, keepdims=True)

    def narrow(mid, lo, hi):
        mid = jnp.minimum(jnp.maximum(mid, lo), hi)
        ok = count_ge(mid) >= capf
        return jnp.where(ok, mid, lo), jnp.where(ok, hi, mid)

    def geo(_, lohi):
        lo, hi = lohi
        return narrow(jnp.sqrt(jnp.maximum(lo, F32_TINY) * hi), lo, hi)

    def ari(_, lohi):
        lo, hi = lohi
        return narrow(0.5 * lo + 0.5 * hi, lo, hi)

    lohi = (jnp.zeros((ne, 1), F32), jnp.full((ne, 1), 2.0, F32))
    lohi = lax.fori_loop(0, 34, geo, lohi)
    lo, hi = lax.fori_loop(0, 8, ari, lohi)
    gt = a >= hi
    eq = (a >= lo) & jnp.logical_not(gt)
    need = capf - jnp.sum(jnp.where(gt, 1.0, 0.0), axis=1, keepdims=True)

    tri = tri_ref[...]

    def excl_cumsum(mask):
        mb = jnp.where(mask, 1.0, 0.0).astype(BF16)
        off = jnp.zeros((ne, 1), F32)
        outs = []
        for j in range(l // LANES):
            blk = mb[:, j * LANES:(j + 1) * LANES]
            outs.append(jnp.dot(blk, tri, preferred_element_type=F32) + off)
            off = off + jnp.sum(blk.astype(F32), axis=1, keepdims=True)
        return jnp.concatenate(outs, axis=1)

    sel = gt | (eq & (excl_cumsum(eq) < need))
    pos_scr[...] = jnp.where(sel, excl_cumsum(sel), -1.0)

    tc = 512
    slot = lax.broadcasted_iota(jnp.int32, (cap, tc), 0).astype(F32)

    def per_expert(e, _):
        acc = jnp.zeros((cap, LANES), F32)
        for c in range(l // tc):
            pc = pos_scr[pl.ds(e, 1), c * tc:(c + 1) * tc]
            onehot = jnp.where(pc == slot, 1.0, 0.0).astype(BF16)
            acc = acc + jnp.dot(onehot, tok_ref[c * tc:(c + 1) * tc, :], preferred_element_type=F32)
        acc_t = acc.T
        idx = acc_t[0:1, :] * 64.0 + acc_t[1:2, :]
        idx_ref[0, pl.ds(e, 1), :] = idx.astype(jnp.int32)
        return 0

    lax.fori_loop(0, ne, per_expert, 0)


def _route(aff_t, *, cap):
    b, ne, l = aff_t.shape
    t = jnp.arange(l, dtype=jnp.int32)
    tok = jnp.zeros((l, LANES), F32).at[:, 0].set((t >> 6).astype(F32)).at[:, 1].set((t & 63).astype(F32)).astype(BF16)
    tri = (jnp.arange(LANES)[:, None] < jnp.arange(LANES)[None, :]).astype(BF16)
    return pl.pallas_call(
        functools.partial(_route_kernel, cap=cap),
        grid=(b,),
        in_specs=[pl.BlockSpec((1, ne, l), lambda i: (i, 0, 0)),
                  pl.BlockSpec((l, LANES), lambda i: (0, 0)),
                  pl.BlockSpec((LANES, LANES), lambda i: (0, 0))],
        out_specs=pl.BlockSpec((1, ne, cap), lambda i: (i, 0, 0)),
        out_shape=jax.ShapeDtypeStruct((b, ne, cap), jnp.int32),
        scratch_shapes=[pltpu.VMEM((ne, l), F32)],
        compiler_params=_cparams(("parallel",)),
        name="route",
    )(aff_t, tok, tri)


ROW_UNROLL = 4


def _gather_kernel(idx_ref, h2_ref, xs_ref, xg_scr, *, cap):
    dk = xg_scr.shape[0] // cap

    def gather(s, _):
        t = idx_ref[0, 0, s]
        xg_scr[pl.ds(pl.multiple_of(s * dk, dk), dk), :] = h2_ref[0, pl.ds(pl.multiple_of(t * dk, dk), dk), :]
        return 0

    lax.fori_loop(0, cap, gather, 0, unroll=ROW_UNROLL)
    for k in range(dk):
        xs_ref[0, 0, :, k * LANES:(k + 1) * LANES] = xg_scr[pl.ds(k, cap, stride=dk), :].astype(BF16)


def _moe_gather(idx, h2, *, d):
    b, ne, cap = idx.shape
    rows = h2.shape[1]
    return pl.pallas_call(
        functools.partial(_gather_kernel, cap=cap),
        grid=(b, ne),
        in_specs=[pl.BlockSpec((1, 1, cap), lambda i, e: (i * ne + e, 0, 0), memory_space=pltpu.SMEM),
                  pl.BlockSpec((1, rows, LANES), lambda i, e: (i, 0, 0), pipeline_mode=pl.Buffered(1))],
        out_specs=pl.BlockSpec((1, 1, cap, d), lambda i, e: (i, e, 0, 0)),
        out_shape=jax.ShapeDtypeStruct((b, ne, cap, d), BF16),
        scratch_shapes=[pltpu.VMEM((cap * d // LANES, LANES), F32)],
        compiler_params=_cparams(("arbitrary", "arbitrary")),
        name="moe_gather",
    )(idx.reshape(b * ne, 1, cap), h2)


def _expert_kernel(xs_ref, wg_ref, wu_ref, wd_ref, ys_ref, *, nsplit):
    xs = xs_ref[0, 0]
    tf = wg_ref.shape[2] // nsplit
    y = None
    for j in range(nsplit):
        cols = slice(j * tf, (j + 1) * tf)
        gg = jnp.dot(xs, wg_ref[0, :, cols], preferred_element_type=F32)
        uu = jnp.dot(xs, wu_ref[0, :, cols], preferred_element_type=F32)
        hid = (gg * _sigmoid(gg) * uu).astype(BF16)
        part = jnp.dot(hid, wd_ref[0, cols, :], preferred_element_type=F32)
        y = part if y is None else y + part
    ys_ref[0, 0] = y.astype(BF16)


def _moe_experts(xs, wg, wu, wd):
    b, ne, cap, d = xs.shape
    fdim = wg.shape[2]
    tok = pl.BlockSpec((1, 1, cap, d), lambda e, i: (i, e, 0, 0))
    return pl.pallas_call(
        functools.partial(_expert_kernel, nsplit=2),
        grid=(ne, b),
        in_specs=[tok,
                  pl.BlockSpec((1, d, fdim), lambda e, i: (e, 0, 0)),
                  pl.BlockSpec((1, d, fdim), lambda e, i: (e, 0, 0)),
                  pl.BlockSpec((1, fdim, d), lambda e, i: (e, 0, 0))],
        out_specs=tok,
        out_shape=jax.ShapeDtypeStruct((b, ne, cap, d), BF16),
        compiler_params=_cparams(("arbitrary", "arbitrary")),
        name="moe_experts",
    )(xs, wg, wu, wd)


def _scatter_kernel(idx_ref, ys_ref, aff_ref, f_ref, y2_scr, g_scr, *, cap):
    e = pl.program_id(1)
    dk = y2_scr.shape[0] // cap

    @pl.when(e == 0)
    def _():
        f_ref[...] = jnp.zeros_like(f_ref)

    def gates(s, _):
        g_scr[pl.ds(s, 1), :] = aff_ref[0, pl.ds(idx_ref[0, 0, s], 1), :]
        return 0

    lax.fori_loop(0, cap, gates, 0, unroll=ROW_UNROLL)
    lane = lax.broadcasted_iota(jnp.int32, g_scr.shape, 1)
    gate = jnp.sum(jnp.where(lane == e, g_scr[...], 0.0), axis=1, keepdims=True)
    yg = ys_ref[0, 0].astype(F32) * gate
    for k in range(dk):
        y2_scr[pl.ds(k, cap, stride=dk), :] = yg[:, k * LANES:(k + 1) * LANES]

    def scatter(s0, _):
        rows = []
        vals = []
        for i in range(ROW_UNROLL):
            s = s0 * ROW_UNROLL + i
            r = pl.multiple_of(idx_ref[0, 0, s] * dk, dk)
            rows.append(r)
            vals.append(f_ref[0, pl.ds(r, dk), :] + y2_scr[pl.ds(pl.multiple_of(s * dk, dk), dk), :])
        for r, v in zip(rows, vals):
            f_ref[0, pl.ds(r, dk), :] = v
        return 0

    lax.fori_loop(0, cap // ROW_UNROLL, scatter, 0)


def _moe_scatter(idx, ys, aff):
    b, ne, cap, d = ys.shape
    l = aff.shape[1]
    dk = d // LANES
    return pl.pallas_call(
        functools.partial(_scatter_kernel, cap=cap),
        grid=(b, ne),
        in_specs=[pl.BlockSpec((1, 1, cap), lambda i, e: (i * ne + e, 0, 0), memory_space=pltpu.SMEM),
                  pl.BlockSpec((1, 1, cap, d), lambda i, e: (i, e, 0, 0)),
                  pl.BlockSpec((1, l, LANES), lambda i, e: (i, 0, 0))],
        out_specs=pl.BlockSpec((1, l * dk, LANES), lambda i, e: (i, 0, 0), pipeline_mode=pl.Buffered(1)),
        out_shape=jax.ShapeDtypeStruct((b, l * dk, LANES), F32),
        scratch_shapes=[pltpu.VMEM((cap * dk, LANES), F32), pltpu.VMEM((cap, LANES), F32)],
        compiler_params=_cparams(("arbitrary", "arbitrary")),
        name="moe_scatter",
    )(idx.reshape(b * ne, 1, cap), ys, aff)


def _final_kernel(f_ref, x1_ref, gf_ref, nw_ref, o_ref):
    tm, d = x1_ref.shape[1:]
    dk = d // LANES
    f = jnp.concatenate([f_ref[0, pl.ds(k, tm, stride=dk), :] for k in range(dk)], axis=1)
    o_ref[0] = x1_ref[0] + gf_ref[0] * (_rms(f) * nw_ref[...])


def _final(f, x1, gf, nw, *, tm):
    b, l, d = x1.shape
    dk = d // LANES
    return pl.pallas_call(
        _final_kernel,
        grid=(b, l // tm),
        in_specs=[pl.BlockSpec((1, tm * dk, LANES), lambda i, j: (i, j, 0)),
                  pl.BlockSpec((1, tm, d), lambda i, j: (i, j, 0)),
                  pl.BlockSpec((1, 1, d), lambda i, j: (i, 0, 0)),
                  pl.BlockSpec((1, d), lambda i, j: (0, 0))],
        out_specs=pl.BlockSpec((1, tm, d), lambda i, j: (i, j, 0)),
        out_shape=jax.ShapeDtypeStruct((b, l, d), F32),
        compiler_params=_cparams(("parallel", "parallel")),
        name="final",
    )(f, x1, gf, nw)


def kernel(x, c, ctx, c_ctx, w_ada, b_ada, norm_pre_mix, norm_post_mix, norm_pre_ffn, norm_post_ffn, w_in, s5_lam_re, s5_lam_im, s5_log_dt, s5_b_re, s5_b_im, s5_c_re, s5_c_im, s5_d, w_glu, da_lambda, da_subln, w_proj_a, w_proj_b, w_out, w_router, w_exp_gate, w_exp_up, w_exp_down):
    depth = w_ada.shape[0]
    assert depth == 1, "single trunk layer: the context stream's outputs are never consumed"
    b, l, d = x.shape
    lc = ctx.shape[1]
    assert b == SUBLANES and l % (S5_CHUNK * 32) == 0 and lc % (S5_CHUNK * 16) == 0
    s5w = s5_d.shape[1]
    qkw = DA_HEADS * 2 * DA_HEAD_DIM
    vw = DA_HEADS * DA_V_DIM
    widths = (s5w, qkw, vw, d)
    lam_init = 0.8 - 0.6 * math.exp(-0.3 * 0)

    c_all = jnp.zeros((2 * SUBLANES, d), F32).at[:b].set(c).at[b].set(c_ctx)
    mod = _ada(c_all, w_ada[0], b_ada[0])
    sh_m, sc_m, g_m, sh_f, sc_f, g_f = [mod[:b, i * d:(i + 1) * d].reshape(b, 1, d) for i in range(6)]
    csh_m = mod[b:b + 1, 0:d].reshape(1, 1, d)
    csc_m = mod[b:b + 1, d:2 * d].reshape(1, 1, d)

    w_in_b = w_in[0].astype(BF16)
    cos_t, sin_t = _rope_tables(l)
    npm = norm_pre_mix[0].reshape(1, d)
    u, q, k, v, ga, gb = _inproj(x, npm, sh_m, sc_m, w_in_b, cos_t, sin_t, latent=True, widths=widths, tm=512)
    uc, kc, vc = _inproj(ctx, npm, csh_m, csc_m, w_in_b, cos_t[:lc], sin_t[:lc], latent=False, widths=widths, tm=lc)

    w_mat, m_mat, v_mat, a_mat = _s5_matrices(s5_lam_re[0], s5_lam_im[0], s5_log_dt[0], s5_b_re[0], s5_b_im[0],
                                              s5_c_re[0], s5_c_im[0], s5_d[0])
    ys_rows = _s5(uc, u, w_mat, m_mat, v_mat, a_mat, n_ctx=lc // S5_CHUNK, n_lat=l // S5_CHUNK)

    lq1, lk1, lq2, lk2 = da_lambda[0].astype(F32)
    lam = jnp.exp(jnp.sum(lq1 * lk1)) - jnp.exp(jnp.sum(lq2 * lk2)) + lam_init
    k_all = jnp.concatenate([kc, k], axis=1)
    kt = k_all.reshape(b, lc + l, DA_HEADS, DA_V_DIM).transpose(0, 2, 3, 1)
    v_all = jnp.concatenate([vc, v], axis=1)
    yb = _attn(lam.reshape(1, 1), q, kt, v_all, da_subln[0].reshape(1, DA_V_DIM), tq=256, out_scale=1.0 - lam_init)

    wr = jnp.zeros((d, LANES), F32).at[:, :N_EXPERTS].set(w_router[0])
    wr_hi = wr.astype(BF16)
    wr_lo = (wr - wr_hi.astype(F32)).astype(BF16)
    wr2 = jnp.concatenate([jnp.concatenate([wr_hi, wr_lo], axis=1),
                           jnp.concatenate([wr_hi, jnp.zeros_like(wr_hi)], axis=1)], axis=0)
    x1, h2, aff, aff_t = _merge(x, ys_rows, yb, ga, gb, g_m, sh_f, sc_f,
                                w_glu[0].astype(BF16), w_proj_a[0].astype(BF16), w_proj_b[0].astype(BF16),
                                w_out[0].astype(BF16), norm_post_mix[0].reshape(1, d), norm_pre_ffn[0].reshape(1, d),
                                wr2, tm=512)

    cap = EC_CAPACITY * l // N_EXPERTS
    idx = _route(aff_t, cap=cap)
    xs = _moe_gather(idx, h2, d=d)
    ys = _moe_experts(xs, w_exp_gate[0].astype(BF16), w_exp_up[0].astype(BF16), w_exp_down[0].astype(BF16))
    f = _moe_scatter(idx, ys, aff)
    return _final(f, x1, g_f, norm_post_ffn[0].reshape(1, d), tm=512)
```

```python
import functools
import math

import numpy as np

import jax
import jax.numpy as jnp
from jax import lax
from jax.experimental import pallas as pl
from jax.experimental.pallas import tpu as pltpu

F32 = jnp.float32
BF16 = jnp.bfloat16

EPS = 1e-6
GRID_W = 64
ROPE_THETA = 10000.0
S5_GROUP = 16
S5_STATE = 64
S5_DT_MAX_RE = -1e-4
S5_CHUNK = 16
DA_HEADS = 4
DA_HEAD_DIM = 64
DA_V_DIM = 128
N_EXPERTS = 16
EC_CAPACITY = 2
LANES = 128
SUBLANES = 8
VMEM_LIMIT = 60 * 1024 * 1024
LOG2E = 1.4426950408889634
F32_TINY = 1e-37


def _cparams(sem):
    return pltpu.CompilerParams(dimension_semantics=sem, vmem_limit_bytes=VMEM_LIMIT)


def _rms(x, eps=EPS):
    return x * lax.rsqrt(jnp.mean(x * x, axis=-1, keepdims=True) + eps)


def _sigmoid(x):
    return 1.0 / (1.0 + jnp.exp(-x))


def _ada_kernel(c_ref, w_ref, b_ref, o_ref):
    c = c_ref[...]
    s = (c * _sigmoid(c)).astype(BF16)
    o_ref[...] = jnp.dot(s, w_ref[...].astype(BF16), preferred_element_type=F32) + b_ref[...]


def _ada(c_all, w_ada, b_ada):
    rows, d = c_all.shape
    n = w_ada.shape[1]
    tn = 512
    return pl.pallas_call(
        _ada_kernel,
        grid=(n // tn,),
        in_specs=[pl.BlockSpec((rows, d), lambda j: (0, 0)),
                  pl.BlockSpec((d, tn), lambda j: (0, j)),
                  pl.BlockSpec((1, tn), lambda j: (0, j))],
        out_specs=pl.BlockSpec((rows, tn), lambda j: (0, j)),
        out_shape=jax.ShapeDtypeStruct((rows, n), F32),
        compiler_params=_cparams(("arbitrary",)),
        name="ada",
    )(c_all, w_ada, b_ada.reshape(1, n))


def _block_transpose8(vs):
    lane_blk = lax.broadcasted_iota(jnp.int32, vs[0].shape, 1) // S5_GROUP
    vs = list(vs)
    for s in (4, 2, 1):
        upper = (lane_blk & s) != 0
        for i in range(8):
            if i & s:
                continue
            a, b = vs[i], vs[i + s]
            vs[i] = jnp.where(upper, pltpu.roll(b, S5_GROUP * s, 1), a)
            vs[i + s] = jnp.where(upper, b, pltpu.roll(a, LANES - S5_GROUP * s, 1))
    return vs


def _tokens_to_chunks(scr, out_ref, row0, nrows):
    nc = nrows // S5_CHUNK
    c0 = row0 // S5_CHUNK
    for gt in range(scr.shape[0]):
        for j in range(S5_CHUNK // 8):
            vs = [scr[gt, pl.ds(row0 + 8 * j + i, nc, stride=S5_CHUNK), :] for i in range(8)]
            vs = _block_transpose8(vs)
            for gi in range(8):
                out_ref[gt * 8 + gi, c0:c0 + nc, j * LANES:(j + 1) * LANES] = vs[gi].astype(out_ref.dtype)


def _chunks_to_tokens(in_ref, scr, row0, nrows):
    nc = nrows // S5_CHUNK
    c0 = row0 // S5_CHUNK
    for gt in range(scr.shape[0]):
        for j in range(S5_CHUNK // 8):
            vs = [in_ref[gt * 8 + gi, c0:c0 + nc, j * LANES:(j + 1) * LANES].astype(F32) for gi in range(8)]
            vs = _block_transpose8(vs)
            for i in range(8):
                scr[gt, pl.ds(row0 + 8 * j + i, nc, stride=S5_CHUNK), :] = vs[i]


def _swap16(x):
    lane = lax.broadcasted_iota(jnp.int32, x.shape, 1)
    return jnp.where((lane & 16) == 0, pltpu.roll(x, LANES - 16, 1), pltpu.roll(x, 16, 1))


def _inproj_kernel(x_ref, nw_ref, sh_ref, sc_ref, w_ref, cos_ref, sin_ref, *refs, latent, widths):
    tm = x_ref.shape[1]
    nparts = 2 if tm % (2 * S5_CHUNK * 16) == 0 else 1
    nr = tm // nparts
    s5w, qkw, vw, dm = widths
    o_u, o_q, o_k, o_v, o_ga = 0, s5w, s5w + qkw, s5w + 2 * qkw, s5w + 2 * qkw + vw
    if latent:
        u_ref, q_ref, k_ref, v_ref, ga_ref, gb_ref, u_scr = refs
    else:
        u_ref, k_ref, v_ref, u_scr = refs
    hs = [None] * nparts

    def rows(part):
        return slice(part * nr, (part + 1) * nr)

    def proj(part, lo, n):
        return jnp.dot(hs[part], w_ref[:, lo:lo + n], preferred_element_type=F32)

    def rope(part, z, scale):
        cos = cos_ref[rows(part), :]
        sin = sin_ref[rows(part), :]
        pieces = []
        for j in range(z.shape[1] // LANES):
            zj = z[:, j * LANES:(j + 1) * LANES]
            pieces.append((zj * cos + _swap16(zj) * sin) * scale)
        return jnp.concatenate(pieces, axis=1)

    def st_norm(part):
        h = _rms(x_ref[0, rows(part), :]) * nw_ref[...]
        hs[part] = (h * (1.0 + sc_ref[0]) + sh_ref[0]).astype(BF16)

    def st_q(part):
        q_ref[0, rows(part), :] = rope(part, proj(part, o_q, qkw), DA_HEAD_DIM ** -0.5 * LOG2E).astype(BF16)

    def st_k(part):
        k = proj(part, o_k, qkw)
        k_ref[0, rows(part), :] = (rope(part, k, 1.0) if latent else k).astype(BF16)

    def st_v(part):
        v_ref[0, rows(part), :] = proj(part, o_v, vw).astype(BF16)

    def st_ga(part):
        ga_ref[0, rows(part), :] = _sigmoid(proj(part, o_ga, dm)).astype(BF16)

    def st_gb(part):
        gb_ref[0, rows(part), :] = _sigmoid(proj(part, o_ga + dm, dm)).astype(BF16)

    def st_u(part):
        u = proj(part, o_u, s5w)
        for gt in range(s5w // LANES):
            u_scr[gt, rows(part), :] = u[:, gt * LANES:(gt + 1) * LANES]
        _tokens_to_chunks(u_scr, u_ref, part * nr, nr)

    stages = (st_norm, st_q, st_k, st_v, st_ga, st_gb, st_u) if latent else (st_norm, st_k, st_v, st_u)
    for step in range(len(stages) + nparts - 1):
        for part in range(nparts):
            if 0 <= step - part < len(stages):
                stages[step - part](part)


def _inproj(x, nw, sh, sc, w_in, cos_t, sin_t, *, latent, widths, tm):
    b, l, d = x.shape
    s5w, qkw, vw, dm = widths
    groups = s5w // S5_GROUP
    nc = tm // S5_CHUNK
    per_b = sh.shape[0] > 1
    mod_spec = pl.BlockSpec((1, 1, d), (lambda i, j: (i, 0, 0)) if per_b else (lambda i, j: (0, 0, 0)))

    def tok_spec(n):
        return pl.BlockSpec((1, tm, n), lambda i, j: (i, j, 0))

    nj = l // tm
    u_spec = pl.BlockSpec((groups, nc, S5_CHUNK * S5_GROUP), lambda i, j: (0, i * nj + j, 0))
    u_shape = jax.ShapeDtypeStruct((groups, b * (l // S5_CHUNK), S5_CHUNK * S5_GROUP), BF16)
    out_w = (qkw, qkw, vw, dm, dm) if latent else (qkw, vw)
    return pl.pallas_call(
        functools.partial(_inproj_kernel, latent=latent, widths=widths),
        grid=(b, nj),
        in_specs=[tok_spec(d),
                  pl.BlockSpec((1, d), lambda i, j: (0, 0)),
                  mod_spec, mod_spec,
                  pl.BlockSpec(w_in.shape, lambda i, j: (0, 0)),
                  pl.BlockSpec((tm, LANES), lambda i, j: (j, 0)),
                  pl.BlockSpec((tm, LANES), lambda i, j: (j, 0))],
        out_specs=[u_spec] + [tok_spec(n) for n in out_w],
        out_shape=[u_shape] + [jax.ShapeDtypeStruct((b, l, n), BF16) for n in out_w],
        scratch_shapes=[pltpu.VMEM((s5w // LANES, tm, LANES), F32)],
        compiler_params=_cparams(("parallel", "parallel")),
        name="inproj_lat" if latent else "inproj_ctx",
    )(x, nw, sh, sc, w_in, cos_t, sin_t)


def _rope_tables(seq_len):
    rows = seq_len // GRID_W
    pairs = DA_HEAD_DIM // 4
    row = np.repeat(np.arange(rows), GRID_W).astype(np.float64)
    col = np.tile(np.arange(GRID_W), rows).astype(np.float64)
    inv_freq = ROPE_THETA ** (-np.arange(pairs, dtype=np.float64) / pairs)
    ra = row[:, None] * inv_freq[None, :]
    ca = col[:, None] * inv_freq[None, :]
    cos64 = np.concatenate([np.cos(ra), np.cos(ra), np.cos(ca), np.cos(ca)], axis=1)
    sin64 = np.concatenate([-np.sin(ra), np.sin(ra), -np.sin(ca), np.sin(ca)], axis=1)
    return (jnp.asarray(np.tile(cos64, (1, 2)), dtype=F32), jnp.asarray(np.tile(sin64, (1, 2)), dtype=F32))


def _s5_matrices(lam_re, lam_im, log_dt, b_re, b_im, c_re, c_im, d_skip):
    hp = lax.Precision.HIGHEST
    t = S5_CHUNK
    lam_re = jnp.minimum(lam_re.astype(F32), S5_DT_MAX_RE)
    lam_im = lam_im.astype(F32)
    dt = jnp.exp(log_dt.astype(F32))[..., None]
    g, p = lam_re.shape[1:]
    hh = S5_GROUP
    mag = jnp.exp(lam_re * dt)
    lb_re = mag * jnp.cos(lam_im * dt)
    lb_im = mag * jnp.sin(lam_im * dt)
    den = lam_re * lam_re + lam_im * lam_im
    num_re = lb_re - 1.0
    co_re = (num_re * lam_re + lb_im * lam_im) / den
    co_im = (lb_im * lam_re - num_re * lam_im) / den
    br = b_re.astype(F32)
    bi = b_im.astype(F32)
    bb_re = co_re[..., None] * br - co_im[..., None] * bi
    bb_im = co_re[..., None] * bi + co_im[..., None] * br
    j = jnp.arange(t + 1, dtype=F32)[None, None, :, None]
    pmag = jnp.exp(lam_re[:, :, None, :] * dt[:, :, None, :] * j)
    pang = lam_im[:, :, None, :] * dt[:, :, None, :] * j
    pw_re = pmag * jnp.cos(pang)
    pw_im = pmag * jnp.sin(pang)
    cr = c_re.astype(F32)
    ci = c_im.astype(F32)
    cp_re = cr[:, :, None] * pw_re[:, :, :t, None, :] - ci[:, :, None] * pw_im[:, :, :t, None, :]
    cp_im = cr[:, :, None] * pw_im[:, :, :t, None, :] + ci[:, :, None] * pw_re[:, :, :t, None, :]
    taps = (jnp.einsum('dgjhp,dgpi->dgjhi', cp_re, bb_re, precision=hp)
            - jnp.einsum('dgjhp,dgpi->dgjhi', cp_im, bb_im, precision=hp))
    sig = np.arange(t)[:, None, None]
    tau = np.arange(t)[None, :, None]
    lag = np.arange(t)[None, None, :]
    place = np.concatenate([tau - sig == lag, sig - tau == lag], axis=2).astype(np.float32)
    taps2 = jnp.concatenate([taps[0], taps[1]], axis=1)
    skip = d_skip.astype(F32).reshape(g, hh)
    diag = (np.eye(t, dtype=np.float32)[None, :, None, :, None]
            * (jnp.eye(hh, dtype=F32)[None] * skip[:, None, :])[:, None, :, None, :])
    m_mat = (jnp.einsum('stj,gjab->gsbta', place, taps2, precision=hp) + diag).reshape(g, t * hh, t * hh)
    pf_re = pw_re[0][:, :t][:, ::-1]
    pf_im = pw_im[0][:, :t][:, ::-1]
    pr_re = pw_re[1][:, :t]
    pr_im = pw_im[1][:, :t]

    def state_in(pr_, pi_, br_, bi_):
        brt = br_.transpose(0, 2, 1)[:, None]
        bit = bi_.transpose(0, 2, 1)[:, None]
        re = pr_[:, :, None, :] * brt - pi_[:, :, None, :] * bit
        im = pr_[:, :, None, :] * bit + pi_[:, :, None, :] * brt
        return re.reshape(g, t * hh, p), im.reshape(g, t * hh, p)

    wf_re, wf_im = state_in(pf_re, pf_im, bb_re[0], bb_im[0])
    wr_re, wr_im = state_in(pr_re, pr_im, bb_re[1], bb_im[1])
    w_mat = jnp.concatenate([wf_re, wr_re, wf_im, wr_im], axis=2)
    ef_re = pw_re[0][:, 1:t + 1]
    ef_im = pw_im[0][:, 1:t + 1]
    er_re = pw_re[1][:, 1:t + 1][:, ::-1]
    er_im = pw_im[1][:, 1:t + 1][:, ::-1]

    def state_out(e_re, e_im, cr_, ci_):
        g_re = cr_[:, None] * e_re[:, :, None, :] - ci_[:, None] * e_im[:, :, None, :]
        g_im = cr_[:, None] * e_im[:, :, None, :] + ci_[:, None] * e_re[:, :, None, :]
        return (g_re.transpose(0, 3, 1, 2).reshape(g, p, t * hh),
                (-g_im).transpose(0, 3, 1, 2).reshape(g, p, t * hh))

    vf_re, vf_im = state_out(ef_re, ef_im, cr[0], ci[0])
    vr_re, vr_im = state_out(er_re, er_im, cr[1], ci[1])
    v_mat = jnp.concatenate([vf_re, vr_re, vf_im, vr_im], axis=1)
    a_mat = jnp.stack([jnp.concatenate([pw_re[0][:, t], pw_re[1][:, t]], axis=1),
                       jnp.concatenate([pw_im[0][:, t], pw_im[1][:, t]], axis=1)], axis=1)
    return w_mat.astype(BF16), m_mat.astype(BF16), v_mat.astype(BF16), a_mat


def _s5_kernel(uc_ref, ul_ref, w_ref, m_ref, v_ref, a_ref, y_ref, sc_scr, sl_scr, hf_scr, hr_scr, *, n_ctx, n_lat):
    nb = SUBLANES
    half = S5_STATE
    ul = ul_ref[0]
    w = w_ref[0]
    s_c = jnp.dot(uc_ref[0], w, preferred_element_type=F32)
    s_l = jnp.dot(ul, w, preferred_element_type=F32)
    for part in range(2):
        for b in range(nb):
            sc_scr[part, pl.ds(b, n_ctx, stride=nb), :] = s_c[b * n_ctx:(b + 1) * n_ctx, part * LANES:(part + 1) * LANES]
            sl_scr[part, pl.ds(b, n_lat, stride=nb), :] = s_l[b * n_lat:(b + 1) * n_lat, part * LANES:(part + 1) * LANES]
    a = a_ref[0]
    a_re = a[0:1, :]
    a_im = a[1:2, :]
    is_fwd = lax.broadcasted_iota(jnp.int32, (nb, LANES), 1) < half

    def tile(j):
        return pl.ds(pl.multiple_of(j * nb, nb), nb)

    def pick(scr, jf, jr):
        return (jnp.where(is_fwd, scr[0, tile(jf), :], scr[0, tile(jr), :]),
                jnp.where(is_fwd, scr[1, tile(jf), :], scr[1, tile(jr), :]))

    def update(h_re, h_im, s_re, s_im):
        return a_re * h_re - a_im * h_im + s_re, a_re * h_im + a_im * h_re + s_im

    def ctx_step(i, carry):
        return update(*carry, *pick(sc_scr, i, n_ctx - 1 - i))

    def lat_step(c, carry):
        h_re, h_im = carry
        cr = n_lat - 1 - c
        hf_scr[0, tile(c), :] = h_re
        hf_scr[1, tile(c), :] = h_im
        hr_scr[0, tile(cr), :] = h_re
        hr_scr[1, tile(cr), :] = h_im
        return update(h_re, h_im, *pick(sl_scr, c, cr))

    zero = jnp.zeros((nb, LANES), F32)
    carry = lax.fori_loop(0, n_ctx, ctx_step, (zero, zero))
    lax.fori_loop(0, n_lat, lat_step, carry, unroll=2)
    lane = lax.broadcasted_iota(jnp.int32, (n_lat, LANES), 1)

    def entry_states(b):
        rows = pl.ds(b, n_lat, stride=nb)
        return jnp.concatenate([jnp.where(lane < half, hf_scr[part, rows, :], hr_scr[part, rows, :])
                                for part in range(2)], axis=1)

    hin = jnp.concatenate([entry_states(b) for b in range(nb)], axis=0).astype(BF16)
    y = jnp.dot(ul, m_ref[0], preferred_element_type=F32)
    y = y + jnp.dot(hin, v_ref[0], preferred_element_type=F32)
    y_ref[0] = y.astype(BF16)


def _s5(u_ctx, u_lat, w_mat, m_mat, v_mat, a_mat, *, n_ctx, n_lat):
    g, rc, k = u_ctx.shape
    rl = u_lat.shape[1]
    mat_spec = pl.BlockSpec((1, k, k), lambda i: (i, 0, 0))
    return pl.pallas_call(
        functools.partial(_s5_kernel, n_ctx=n_ctx, n_lat=n_lat),
        grid=(g,),
        in_specs=[pl.BlockSpec((1, rc, k), lambda i: (i, 0, 0)),
                  pl.BlockSpec((1, rl, k), lambda i: (i, 0, 0)),
                  mat_spec, mat_spec, mat_spec,
                  pl.BlockSpec((1, 2, LANES), lambda i: (i, 0, 0))],
        out_specs=pl.BlockSpec((1, rl, k), lambda i: (i, 0, 0)),
        out_shape=jax.ShapeDtypeStruct((g, rl, k), BF16),
        scratch_shapes=[pltpu.VMEM((2, rc, LANES), F32), pltpu.VMEM((2, rl, LANES), F32),
                        pltpu.VMEM((2, rl, LANES), F32), pltpu.VMEM((2, rl, LANES), F32)],
        compiler_params=_cparams(("parallel",)),
        name="s5",
    )(u_ctx, u_lat, w_mat, m_mat, v_mat, a_mat)


ATTN_KEY_CHUNK = 256
ATTN_ITEM_ROWS = 128


def _attn_kernel(lam_ref, q_ref, kt_ref, v_ref, sw_ref, o_ref, s_a, s_b, p_a, p_b, *, out_scale):
    lam = lam_ref[0, 0]
    sw = sw_ref[...]
    tr = ATTN_ITEM_ROWS
    lk = kt_ref.shape[3]
    kc = ATTN_KEY_CHUNK
    nch = lk // kc
    s_bufs = (s_a, s_b)
    p_bufs = (p_a, p_b)
    lane = lax.broadcasted_iota(jnp.int32, (tr, DA_V_DIM), 1)
    work = [(r, h) for r in range(q_ref.shape[1] // tr) for h in range(DA_HEADS)]
    items = [dict() for _ in work]

    def keys(c):
        return slice(c * kc, (c + 1) * kc)

    def rows(i):
        return slice(work[i][0] * tr, (work[i][0] + 1) * tr)

    def cols(i):
        return slice(work[i][1] * DA_V_DIM, (work[i][1] + 1) * DA_V_DIM)

    def stage_a(i, c):
        it = items[i]
        if c == 0:
            q = q_ref[0, rows(i), cols(i)]
            zero = jnp.zeros_like(q)
            it['q2'] = jnp.concatenate([jnp.where(lane < DA_HEAD_DIM, q, zero),
                                        jnp.where(lane >= DA_HEAD_DIM, q, zero)], axis=0)
        s = jnp.dot(it['q2'], kt_ref[0, work[i][1], :, keys(c)], preferred_element_type=F32)
        s_bufs[i % 2][:, keys(c)] = s
        mx = jnp.maximum(s[:, :LANES], s[:, LANES:])
        it['mx'] = mx if c == 0 else jnp.maximum(it['mx'], mx)
        if c == nch - 1:
            it['m'] = jnp.broadcast_to(jnp.max(it['mx'], axis=-1, keepdims=True), (2 * tr, kc))

    def stage_b(i, c):
        it = items[i]
        p = jnp.exp2(s_bufs[i % 2][:, keys(c)] - it['m'])
        ps = p[:, :LANES] + p[:, LANES:]
        it['ls'] = ps if c == 0 else it['ls'] + ps
        p_bufs[i % 2][:, keys(c)] = p.astype(BF16)
        if c == nch - 1:
            l = jnp.sum(it['ls'], axis=-1, keepdims=True)
            it['r1'] = jnp.broadcast_to((1.0 / l[:tr]).astype(BF16), (tr, kc))
            it['r2'] = jnp.broadcast_to((lam / l[tr:]).astype(BF16), (tr, kc))

    def stage_c(i, c):
        it = items[i]
        pb = p_bufs[i % 2]
        pd = pb[0:tr, keys(c)] * it['r1'] - pb[tr:2 * tr, keys(c)] * it['r2']
        part = jnp.dot(pd, v_ref[0, keys(c), cols(i)], preferred_element_type=F32)
        it['acc'] = part if c == 0 else it['acc'] + part
        if c == nch - 1:
            o_ref[0, rows(i), cols(i)] = (_rms(it['acc']) * sw * out_scale).astype(BF16)

    n = len(work)
    for slot in range(n + 2):
        for c in range(nch):
            if slot < n:
                stage_a(slot, c)
            if 0 <= slot - 1 < n:
                stage_b(slot - 1, c)
            if 0 <= slot - 2 < n:
                stage_c(slot - 2, c)


def _attn(lam, q, kt, v, subln, *, tq, out_scale):
    b, l, w = q.shape
    lk = v.shape[1]
    assert lk % ATTN_KEY_CHUNK == 0 and tq % ATTN_ITEM_ROWS == 0
    tr = ATTN_ITEM_ROWS
    return pl.pallas_call(
        functools.partial(_attn_kernel, out_scale=out_scale),
        scratch_shapes=[pltpu.VMEM((2 * tr, lk), F32), pltpu.VMEM((2 * tr, lk), F32),
                        pltpu.VMEM((2 * tr, lk), BF16), pltpu.VMEM((2 * tr, lk), BF16)],
        grid=(b, l // tq),
        in_specs=[pl.BlockSpec(memory_space=pltpu.SMEM),
                  pl.BlockSpec((1, tq, w), lambda i, j: (i, j, 0)),
                  pl.BlockSpec((1, DA_HEADS, DA_V_DIM, lk), lambda i, j: (i, 0, 0, 0)),
                  pl.BlockSpec((1, lk, w), lambda i, j: (i, 0, 0)),
                  pl.BlockSpec((1, DA_V_DIM), lambda i, j: (0, 0))],
        out_specs=pl.BlockSpec((1, tq, w), lambda i, j: (i, j, 0)),
        out_shape=jax.ShapeDtypeStruct((b, l, w), BF16),
        compiler_params=_cparams(("parallel", "parallel")),
        name="attn",
    )(lam, q, kt, v, subln)


def _gelu_tanh(x):
    return 0.5 * x * (1.0 + jnp.tanh(math.sqrt(2.0 / math.pi) * (x + 0.044715 * (x * x * x))))


def _merge_kernel(x_ref, ys_ref, yb_ref, ga_ref, gb_ref, gm_ref, shf_ref, scf_ref,
                  wglu_ref, wpa_ref, wpb_ref, wout_ref, npost_ref, npre_ref, wr2_ref,
                  x1_ref, h2_ref, aff_ref, afft_ref, ys_scr):
    tm, d = x_ref.shape[1:]
    dk = d // LANES
    nr = tm // 2
    state = [dict(), dict()]

    def rows(hf):
        return slice(hf * nr, (hf + 1) * nr)

    def st_relayout(hf):
        _chunks_to_tokens(ys_ref, ys_scr, hf * nr, nr)

    def st_glu(hf):
        st = state[hf]
        st['ya'] = _gelu_tanh(jnp.concatenate([ys_scr[gt, rows(hf), :] for gt in range(ys_scr.shape[0])], axis=1))
        st['glu'] = jnp.dot(st['ya'].astype(BF16), wglu_ref[...], preferred_element_type=F32)

    def st_proj(hf):
        st = state[hf]
        ya = st['ya'] * _sigmoid(st['glu'])
        st['pa'] = jnp.dot(ya.astype(BF16), wpa_ref[...], preferred_element_type=F32)
        st['pb'] = jnp.dot(yb_ref[0, rows(hf), :], wpb_ref[...], preferred_element_type=F32)

    def st_out(hf):
        st = state[hf]
        mix = ga_ref[0, rows(hf), :].astype(F32) * st['pa'] + gb_ref[0, rows(hf), :].astype(F32) * st['pb']
        st['o'] = jnp.dot(mix.astype(BF16), wout_ref[...], preferred_element_type=F32)

    def st_residual(hf):
        st = state[hf]
        x1 = x_ref[0, rows(hf), :] + gm_ref[0] * (_rms(st['o']) * npost_ref[...])
        x1_ref[0, rows(hf), :] = x1
        h = _rms(x1) * npre_ref[...]
        h = h * (1.0 + scf_ref[0]) + shf_ref[0]
        for k in range(dk):
            h2_ref[0, pl.ds(hf * nr * dk + k, nr, stride=dk), :] = h[:, k * LANES:(k + 1) * LANES]
        st['h'] = h

    def st_router(hf):
        st = state[hf]
        h = st['h']
        h_hi = h.astype(BF16)
        h_lo = (h - h_hi.astype(F32)).astype(BF16)
        st['lg'] = jnp.dot(jnp.concatenate([h_hi, h_lo], axis=1), wr2_ref[...], preferred_element_type=F32)

    def st_softmax(hf):
        lg = state[hf]['lg']
        logits = lg[:, :LANES] + lg[:, LANES:]
        lane = lax.broadcasted_iota(jnp.int32, logits.shape, 1)
        logits = jnp.where(lane < N_EXPERTS, logits, -1e30)
        ex = jnp.exp(logits - jnp.max(logits, axis=-1, keepdims=True))
        aff = ex / jnp.sum(ex, axis=-1, keepdims=True)
        aff_ref[0, rows(hf), :] = aff
        afft_ref[0, :, rows(hf)] = aff.T[:N_EXPERTS, :]

    stages = (st_relayout, st_glu, st_proj, st_out, st_residual, st_router, st_softmax)
    for k in range(len(stages) + 1):
        if k < len(stages):
            stages[k](0)
        if k >= 1:
            stages[k - 1](1)


def _merge(x, ys_rows, yb, ga, gb, gm, shf, scf, wglu, wpa, wpb, wout, npost, npre, wr2, *, tm):
    b, l, d = x.shape
    dk = d // LANES
    groups, _, ck = ys_rows.shape
    nj = l // tm

    def tok(n):
        return pl.BlockSpec((1, tm, n), lambda i, j: (i, j, 0))

    def full(a):
        return pl.BlockSpec(a.shape, lambda i, j: (0,) * a.ndim)

    mod = pl.BlockSpec((1, 1, d), lambda i, j: (i, 0, 0))
    return pl.pallas_call(
        _merge_kernel,
        grid=(b, nj),
        in_specs=[tok(d),
                  pl.BlockSpec((groups, tm // S5_CHUNK, ck), lambda i, j: (0, i * nj + j, 0)),
                  tok(yb.shape[2]), tok(d), tok(d), mod, mod, mod,
                  full(wglu), full(wpa), full(wpb), full(wout), full(npost), full(npre), full(wr2)],
        out_specs=[tok(d),
                   pl.BlockSpec((1, tm * dk, LANES), lambda i, j: (i, j, 0)),
                   tok(LANES),
                   pl.BlockSpec((1, N_EXPERTS, tm), lambda i, j: (i, 0, j))],
        out_shape=[jax.ShapeDtypeStruct((b, l, d), F32),
                   jax.ShapeDtypeStruct((b, l * dk, LANES), F32),
                   jax.ShapeDtypeStruct((b, l, LANES), F32),
                   jax.ShapeDtypeStruct((b, N_EXPERTS, l), F32)],
        scratch_shapes=[pltpu.VMEM((groups * S5_GROUP // LANES, tm, LANES), F32)],
        compiler_params=_cparams(("parallel", "parallel")),
        name="merge",
    )(x, ys_rows, yb, ga, gb, gm, shf, scf, wglu, wpa, wpb, wout, npost, npre, wr2)


def _route_kernel(a_ref, tok_ref, tri_ref, idx_ref, pos_scr, *, cap):
    a = a_ref[0]
    ne, l = a.shape
    capf = float(cap)

    def count_ge(thr):
        return jnp.sum(jnp.where(a >= thr, 1.0, 0.0), axis=1, keepdims=True)

    def narrow(mid, lo, hi):
        mid = jnp.minimum(jnp.maximum(mid, lo), hi)
        ok = count_ge(mid) >= capf
        return jnp.where(ok, mid, lo), jnp.where(ok, hi, mid)

    def geo(_, lohi):
        lo, hi = lohi
        return narrow(jnp.sqrt(jnp.maximum(lo, F32_TINY) * hi), lo, hi)

    def ari(_, lohi):
        lo, hi = lohi
        return narrow(0.5 * lo + 0.5 * hi, lo, hi)

    lohi = (jnp.zeros((ne, 1), F32), jnp.full((ne, 1), 2.0, F32))
    lohi = lax.fori_loop(0, 34, geo, lohi)
    lo, hi = lax.fori_loop(0, 8, ari, lohi)
    gt = a >= hi
    eq = (a >= lo) & jnp.logical_not(gt)
    need = capf - jnp.sum(jnp.where(gt, 1.0, 0.0), axis=1, keepdims=True)

    tri = tri_ref[...]

    def excl_cumsum(mask):
        mb = jnp.where(mask, 1.0, 0.0).astype(BF16)
        off = jnp.zeros((ne, 1), F32)
        outs = []
        for j in range(l // LANES):
            blk = mb[:, j * LANES:(j + 1) * LANES]
            outs.append(jnp.dot(blk, tri, preferred_element_type=F32) + off)
            off = off + jnp.sum(blk.astype(F32), axis=1, keepdims=True)
        return jnp.concatenate(outs, axis=1)

    sel = gt | (eq & (excl_cumsum(eq) < need))
    pos_scr[...] = jnp.where(sel, excl_cumsum(sel), -1.0)

    tc = 512
    slot = lax.broadcasted_iota(jnp.int32, (cap, tc), 0).astype(F32)

    def per_expert(e, _):
        acc = jnp.zeros((cap, LANES), F32)
        for c in range(l // tc):
            pc = pos_scr[pl.ds(e, 1), c * tc:(c + 1) * tc]
            onehot = jnp.where(pc == slot, 1.0, 0.0).astype(BF16)
            acc = acc + jnp.dot(onehot, tok_ref[c * tc:(c + 1) * tc, :], preferred_element_type=F32)
        acc_t = acc.T
        idx = acc_t[0:1, :] * 64.0 + acc_t[1:2, :]
        idx_ref[0, pl.ds(e, 1), :] = idx.astype(jnp.int32)
        return 0

    lax.fori_loop(0, ne, per_expert, 0)


def _route(aff_t, *, cap):
    b, ne, l = aff_t.shape
    tok_np = np.zeros((l, LANES), np.float32)
    tok_np[:, 0] = np.arange(l) >> 6
    tok_np[:, 1] = np.arange(l) & 63
    tok = jnp.asarray(tok_np, dtype=BF16)
    tri = jnp.asarray(np.arange(LANES)[:, None] < np.arange(LANES)[None, :], dtype=BF16)
    return pl.pallas_call(
        functools.partial(_route_kernel, cap=cap),
        grid=(b,),
        in_specs=[pl.BlockSpec((1, ne, l), lambda i: (i, 0, 0)),
                  pl.BlockSpec((l, LANES), lambda i: (0, 0)),
                  pl.BlockSpec((LANES, LANES), lambda i: (0, 0))],
        out_specs=pl.BlockSpec((1, ne, cap), lambda i: (i, 0, 0)),
        out_shape=jax.ShapeDtypeStruct((b, ne, cap), jnp.int32),
        scratch_shapes=[pltpu.VMEM((ne, l), F32)],
        compiler_params=_cparams(("parallel",)),
        name="route",
    )(aff_t, tok, tri)


ROW_UNROLL = 4


def _gather_kernel(idx_ref, h2_ref, xs_ref, xg_scr, *, cap):
    dk = xg_scr.shape[0] // cap

    def gather(s, _):
        t = idx_ref[0, 0, s]
        xg_scr[pl.ds(pl.multiple_of(s * dk, dk), dk), :] = h2_ref[0, pl.ds(pl.multiple_of(t * dk, dk), dk), :]
        return 0

    lax.fori_loop(0, cap, gather, 0, unroll=ROW_UNROLL)
    for k in range(dk):
        xs_ref[0, 0, :, k * LANES:(k + 1) * LANES] = xg_scr[pl.ds(k, cap, stride=dk), :].astype(BF16)


def _moe_gather(idx, h2, *, d):
    b, ne, cap = idx.shape
    rows = h2.shape[1]
    return pl.pallas_call(
        functools.partial(_gather_kernel, cap=cap),
        grid=(b, ne),
        in_specs=[pl.BlockSpec((1, 1, cap), lambda i, e: (i * ne + e, 0, 0), memory_space=pltpu.SMEM),
                  pl.BlockSpec((1, rows, LANES), lambda i, e: (i, 0, 0), pipeline_mode=pl.Buffered(1))],
        out_specs=pl.BlockSpec((1, 1, cap, d), lambda i, e: (i, e, 0, 0)),
        out_shape=jax.ShapeDtypeStruct((b, ne, cap, d), BF16),
        scratch_shapes=[pltpu.VMEM((cap * d // LANES, LANES), F32)],
        compiler_params=_cparams(("arbitrary", "arbitrary")),
        name="moe_gather",
    )(idx.reshape(b * ne, 1, cap), h2)


def _expert_kernel(xs_ref, wg_ref, wu_ref, wd_ref, ys_ref, *, nsplit):
    xs = xs_ref[0, 0]
    tf = wg_ref.shape[2] // nsplit
    y = None
    for j in range(nsplit):
        cols = slice(j * tf, (j + 1) * tf)
        gg = jnp.dot(xs, wg_ref[0, :, cols], preferred_element_type=F32)
        uu = jnp.dot(xs, wu_ref[0, :, cols], preferred_element_type=F32)
        hid = (gg * _sigmoid(gg) * uu).astype(BF16)
        part = jnp.dot(hid, wd_ref[0, cols, :], preferred_element_type=F32)
        y = part if y is None else y + part
    ys_ref[0, 0] = y.astype(BF16)


def _moe_experts(xs, wg, wu, wd):
    b, ne, cap, d = xs.shape
    fdim = wg.shape[2]
    tok = pl.BlockSpec((1, 1, cap, d), lambda e, i: (i, e, 0, 0))
    return pl.pallas_call(
        functools.partial(_expert_kernel, nsplit=2),
        grid=(ne, b),
        in_specs=[tok,
                  pl.BlockSpec((1, d, fdim), lambda e, i: (e, 0, 0)),
                  pl.BlockSpec((1, d, fdim), lambda e, i: (e, 0, 0)),
                  pl.BlockSpec((1, fdim, d), lambda e, i: (e, 0, 0))],
        out_specs=tok,
        out_shape=jax.ShapeDtypeStruct((b, ne, cap, d), BF16),
        compiler_params=_cparams(("arbitrary", "arbitrary")),
        name="moe_experts",
    )(xs, wg, wu, wd)


def _scatter_kernel(idx_ref, ys_ref, aff_ref, f_ref, y2_scr, g_scr, *, cap):
    e = pl.program_id(1)
    dk = y2_scr.shape[0] // cap

    @pl.when(e == 0)
    def _():
        f_ref[...] = jnp.zeros_like(f_ref)

    def gates(s, _):
        g_scr[pl.ds(s, 1), :] = aff_ref[0, pl.ds(idx_ref[0, 0, s], 1), :]
        return 0

    lax.fori_loop(0, cap, gates, 0, unroll=ROW_UNROLL)
    lane = lax.broadcasted_iota(jnp.int32, g_scr.shape, 1)
    gate = jnp.sum(jnp.where(lane == e, g_scr[...], 0.0), axis=1, keepdims=True)
    yg = ys_ref[0, 0].astype(F32) * gate
    for k in range(dk):
        y2_scr[pl.ds(k, cap, stride=dk), :] = yg[:, k * LANES:(k + 1) * LANES]

    def scatter(s0, _):
        rows = []
        vals = []
        for i in range(ROW_UNROLL):
            s = s0 * ROW_UNROLL + i
            r = pl.multiple_of(idx_ref[0, 0, s] * dk, dk)
            rows.append(r)
            vals.append(f_ref[0, pl.ds(r, dk), :] + y2_scr[pl.ds(pl.multiple_of(s * dk, dk), dk), :])
        for r, v in zip(rows, vals):
            f_ref[0, pl.ds(r, dk), :] = v
        return 0

    lax.fori_loop(0, cap // ROW_UNROLL, scatter, 0)


def _moe_scatter(idx, ys, aff):
    b, ne, cap, d = ys.shape
    l = aff.shape[1]
    dk = d // LANES
    return pl.pallas_call(
        functools.partial(_scatter_kernel, cap=cap),
        grid=(b, ne),
        in_specs=[pl.BlockSpec((1, 1, cap), lambda i, e: (i * ne + e, 0, 0), memory_space=pltpu.SMEM),
                  pl.BlockSpec((1, 1, cap, d), lambda i, e: (i, e, 0, 0)),
                  pl.BlockSpec((1, l, LANES), lambda i, e: (i, 0, 0))],
        out_specs=pl.BlockSpec((1, l * dk, LANES), lambda i, e: (i, 0, 0), pipeline_mode=pl.Buffered(1)),
        out_shape=jax.ShapeDtypeStruct((b, l * dk, LANES), F32),
        scratch_shapes=[pltpu.VMEM((cap * dk, LANES), F32), pltpu.VMEM((cap, LANES), F32)],
        compiler_params=_cparams(("arbitrary", "arbitrary")),
        name="moe_scatter",
    )(idx.reshape(b * ne, 1, cap), ys, aff)


def _final_kernel(f_ref, x1_ref, gf_ref, nw_ref, o_ref):
    tm, d = x1_ref.shape[1:]
    dk = d // LANES
    f = jnp.concatenate([f_ref[0, pl.ds(k, tm, stride=dk), :] for k in range(dk)], axis=1)
    o_ref[0] = x1_ref[0] + gf_ref[0] * (_rms(f) * nw_ref[...])


def _final(f, x1, gf, nw, *, tm):
    b, l, d = x1.shape
    dk = d // LANES
    return pl.pallas_call(
        _final_kernel,
        grid=(b, l // tm),
        in_specs=[pl.BlockSpec((1, tm * dk, LANES), lambda i, j: (i, j, 0)),
                  pl.BlockSpec((1, tm, d), lambda i, j: (i, j, 0)),
                  pl.BlockSpec((1, 1, d), lambda i, j: (i, 0, 0)),
                  pl.BlockSpec((1, d), lambda i, j: (0, 0))],
        out_specs=pl.BlockSpec((1, tm, d), lambda i, j: (i, j, 0)),
        out_shape=jax.ShapeDtypeStruct((b, l, d), F32),
        compiler_params=_cparams(("parallel", "parallel")),
        name="final",
    )(f, x1, gf, nw)


def kernel(x, c, ctx, c_ctx, w_ada, b_ada, norm_pre_mix, norm_post_mix, norm_pre_ffn, norm_post_ffn, w_in, s5_lam_re, s5_lam_im, s5_log_dt, s5_b_re, s5_b_im, s5_c_re, s5_c_im, s5_d, w_glu, da_lambda, da_subln, w_proj_a, w_proj_b, w_out, w_router, w_exp_gate, w_exp_up, w_exp_down):
    depth = w_ada.shape[0]
    assert depth == 1, "single trunk layer: the context stream's outputs are never consumed"
    b, l, d = x.shape
    lc = ctx.shape[1]
    assert b == SUBLANES and l % (S5_CHUNK * 32) == 0 and lc % (S5_CHUNK * 16) == 0
    s5w = s5_d.shape[1]
    qkw = DA_HEADS * 2 * DA_HEAD_DIM
    vw = DA_HEADS * DA_V_DIM
    widths = (s5w, qkw, vw, d)
    lam_init = 0.8 - 0.6 * math.exp(-0.3 * 0)

    c_all = jnp.zeros((2 * SUBLANES, d), F32).at[:b].set(c).at[b].set(c_ctx)
    mod = _ada(c_all, w_ada[0], b_ada[0])
    sh_m, sc_m, g_m, sh_f, sc_f, g_f = [mod[:b, i * d:(i + 1) * d].reshape(b, 1, d) for i in range(6)]
    csh_m = mod[b:b + 1, 0:d].reshape(1, 1, d)
    csc_m = mod[b:b + 1, d:2 * d].reshape(1, 1, d)

    w_in_b = w_in[0].astype(BF16)
    cos_t, sin_t = _rope_tables(l)
    npm = norm_pre_mix[0].reshape(1, d)
    u, q, k, v, ga, gb = _inproj(x, npm, sh_m, sc_m, w_in_b, cos_t, sin_t, latent=True, widths=widths, tm=1024)
    uc, kc, vc = _inproj(ctx, npm, csh_m, csc_m, w_in_b, cos_t[:lc], sin_t[:lc], latent=False, widths=widths, tm=lc)

    w_mat, m_mat, v_mat, a_mat = _s5_matrices(s5_lam_re[0], s5_lam_im[0], s5_log_dt[0], s5_b_re[0], s5_b_im[0],
                                              s5_c_re[0], s5_c_im[0], s5_d[0])
    ys_rows = _s5(uc, u, w_mat, m_mat, v_mat, a_mat, n_ctx=lc // S5_CHUNK, n_lat=l // S5_CHUNK)

    lq1, lk1, lq2, lk2 = da_lambda[0].astype(F32)
    lam = jnp.exp(jnp.sum(lq1 * lk1)) - jnp.exp(jnp.sum(lq2 * lk2)) + lam_init
    k_all = jnp.concatenate([kc, k], axis=1)
    kt = k_all.reshape(b, lc + l, DA_HEADS, DA_V_DIM).transpose(0, 2, 3, 1)
    v_all = jnp.concatenate([vc, v], axis=1)
    yb = _attn(lam.reshape(1, 1), q, kt, v_all, da_subln[0].reshape(1, DA_V_DIM), tq=256, out_scale=1.0 - lam_init)

    wr = jnp.zeros((d, LANES), F32).at[:, :N_EXPERTS].set(w_router[0])
    wr_hi = wr.astype(BF16)
    wr_lo = (wr - wr_hi.astype(F32)).astype(BF16)
    wr2 = jnp.concatenate([jnp.concatenate([wr_hi, wr_lo], axis=1),
                           jnp.concatenate([wr_hi, jnp.zeros_like(wr_hi)], axis=1)], axis=0)
    x1, h2, aff, aff_t = _merge(x, ys_rows, yb, ga, gb, g_m, sh_f, sc_f,
                                w_glu[0].astype(BF16), w_proj_a[0].astype(BF16), w_proj_b[0].astype(BF16),
                                w_out[0].astype(BF16), norm_post_mix[0].reshape(1, d), norm_pre_ffn[0].reshape(1, d),
                                wr2, tm=512)

    cap = EC_CAPACITY * l // N_EXPERTS
    idx = _route(aff_t, cap=cap)
    xs = _moe_gather(idx, h2, d=d)
    ys = _moe_experts(xs, w_exp_gate[0].astype(BF16), w_exp_up[0].astype(BF16), w_exp_down[0].astype(BF16))
    f = _moe_scatter(idx, ys, aff)
    return _final(f, x1, g_f, norm_post_ffn[0].reshape(1, d), tm=512)
```

```python
import functools
import math

import numpy as np

import jax
import jax.numpy as jnp
from jax import lax
from jax.experimental import pallas as pl
from jax.experimental.pallas import tpu as pltpu

F32 = jnp.float32
BF16 = jnp.bfloat16

EPS = 1e-6
GRID_W = 64
ROPE_THETA = 10000.0
S5_GROUP = 16
S5_STATE = 64
S5_DT_MAX_RE = -1e-4
S5_CHUNK = 16
DA_HEADS = 4
DA_HEAD_DIM = 64
DA_V_DIM = 128
N_EXPERTS = 16
EC_CAPACITY = 2
LANES = 128
SUBLANES = 8
VMEM_LIMIT = 60 * 1024 * 1024
LOG2E = 1.4426950408889634
F32_TINY = 1e-37


def _cparams(sem):
    return pltpu.CompilerParams(dimension_semantics=sem, vmem_limit_bytes=VMEM_LIMIT)


def _rms(x, eps=EPS):
    return x * lax.rsqrt(jnp.mean(x * x, axis=-1, keepdims=True) + eps)


def _sigmoid(x):
    return 1.0 / (1.0 + jnp.exp(-x))


def _ada_kernel(c_ref, w_ref, b_ref, o_ref):
    c = c_ref[...]
    s = (c * _sigmoid(c)).astype(BF16)
    o_ref[...] = jnp.dot(s, w_ref[...].astype(BF16), preferred_element_type=F32) + b_ref[...]


def _ada(c_all, w_ada, b_ada):
    rows, d = c_all.shape
    n = w_ada.shape[1]
    tn = 512
    return pl.pallas_call(
        _ada_kernel,
        grid=(n // tn,),
        in_specs=[pl.BlockSpec((rows, d), lambda j: (0, 0)),
                  pl.BlockSpec((d, tn), lambda j: (0, j)),
                  pl.BlockSpec((1, tn), lambda j: (0, j))],
        out_specs=pl.BlockSpec((rows, tn), lambda j: (0, j)),
        out_shape=jax.ShapeDtypeStruct((rows, n), F32),
        compiler_params=_cparams(("arbitrary",)),
        name="ada",
    )(c_all, w_ada, b_ada.reshape(1, n))


def _block_transpose8(vs):
    lane_blk = lax.broadcasted_iota(jnp.int32, vs[0].shape, 1) // S5_GROUP
    vs = list(vs)
    for s in (4, 2, 1):
        upper = (lane_blk & s) != 0
        for i in range(8):
            if i & s:
                continue
            a, b = vs[i], vs[i + s]
            vs[i] = jnp.where(upper, pltpu.roll(b, S5_GROUP * s, 1), a)
            vs[i + s] = jnp.where(upper, b, pltpu.roll(a, LANES - S5_GROUP * s, 1))
    return vs


def _tokens_to_chunks(scr, out_ref, row0, nrows):
    nc = nrows // S5_CHUNK
    c0 = row0 // S5_CHUNK
    for gt in range(scr.shape[0]):
        for j in range(S5_CHUNK // 8):
            vs = [scr[gt, pl.ds(row0 + 8 * j + i, nc, stride=S5_CHUNK), :] for i in range(8)]
            vs = _block_transpose8(vs)
            for gi in range(8):
                out_ref[gt * 8 + gi, c0:c0 + nc, j * LANES:(j + 1) * LANES] = vs[gi].astype(out_ref.dtype)


def _chunks_to_tokens(in_ref, scr, row0, nrows):
    nc = nrows // S5_CHUNK
    c0 = row0 // S5_CHUNK
    for gt in range(scr.shape[0]):
        for j in range(S5_CHUNK // 8):
            vs = [in_ref[gt * 8 + gi, c0:c0 + nc, j * LANES:(j + 1) * LANES].astype(F32) for gi in range(8)]
            vs = _block_transpose8(vs)
            for i in range(8):
                scr[gt, pl.ds(row0 + 8 * j + i, nc, stride=S5_CHUNK), :] = vs[i]


def _swap16(x):
    lane = lax.broadcasted_iota(jnp.int32, x.shape, 1)
    return jnp.where((lane & 16) == 0, pltpu.roll(x, LANES - 16, 1), pltpu.roll(x, 16, 1))


def _inproj_kernel(x_ref, nw_ref, sh_ref, sc_ref, w_ref, cos_ref, sin_ref, *refs, latent, widths):
    tm = x_ref.shape[1]
    nparts = 2 if tm % (2 * S5_CHUNK * 16) == 0 else 1
    nr = tm // nparts
    s5w, qkw, vw, dm = widths
    o_u, o_q, o_k, o_v, o_ga = 0, s5w, s5w + qkw, s5w + 2 * qkw, s5w + 2 * qkw + vw
    if latent:
        u_ref, q_ref, k_ref, v_ref, ga_ref, gb_ref, u_scr = refs
    else:
        u_ref, k_ref, v_ref, u_scr = refs
    hs = [None] * nparts

    def rows(part):
        return slice(part * nr, (part + 1) * nr)

    def proj(part, lo, n):
        return jnp.dot(hs[part], w_ref[:, lo:lo + n], preferred_element_type=F32)

    def rope(part, z, scale):
        cos = cos_ref[rows(part), :]
        sin = sin_ref[rows(part), :]
        pieces = []
        for j in range(z.shape[1] // LANES):
            zj = z[:, j * LANES:(j + 1) * LANES]
            pieces.append((zj * cos + _swap16(zj) * sin) * scale)
        return jnp.concatenate(pieces, axis=1)

    def st_norm(part):
        h = _rms(x_ref[0, rows(part), :]) * nw_ref[...]
        hs[part] = (h * (1.0 + sc_ref[0]) + sh_ref[0]).astype(BF16)

    def st_q(part):
        q_ref[0, rows(part), :] = rope(part, proj(part, o_q, qkw), DA_HEAD_DIM ** -0.5 * LOG2E).astype(BF16)

    def st_k(part):
        k = proj(part, o_k, qkw)
        k_ref[0, rows(part), :] = (rope(part, k, 1.0) if latent else k).astype(BF16)

    def st_v(part):
        v_ref[0, rows(part), :] = proj(part, o_v, vw).astype(BF16)

    def st_ga(part):
        ga_ref[0, rows(part), :] = _sigmoid(proj(part, o_ga, dm)).astype(BF16)

    def st_gb(part):
        gb_ref[0, rows(part), :] = _sigmoid(proj(part, o_ga + dm, dm)).astype(BF16)

    def st_u(part):
        u = proj(part, o_u, s5w)
        for gt in range(s5w // LANES):
            u_scr[gt, rows(part), :] = u[:, gt * LANES:(gt + 1) * LANES]
        _tokens_to_chunks(u_scr, u_ref, part * nr, nr)

    stages = (st_norm, st_q, st_k, st_v, st_ga, st_gb, st_u) if latent else (st_norm, st_k, st_v, st_u)
    for step in range(len(stages) + nparts - 1):
        for part in range(nparts):
            if 0 <= step - part < len(stages):
                stages[step - part](part)


def _inproj(x, nw, sh, sc, w_in, cos_t, sin_t, *, latent, widths, tm):
    b, l, d = x.shape
    s5w, qkw, vw, dm = widths
    groups = s5w // S5_GROUP
    nc = tm // S5_CHUNK
    per_b = sh.shape[0] > 1
    mod_spec = pl.BlockSpec((1, 1, d), (lambda i, j: (i, 0, 0)) if per_b else (lambda i, j: (0, 0, 0)))

    def tok_spec(n):
        return pl.BlockSpec((1, tm, n), lambda i, j: (i, j, 0))

    nj = l // tm
    u_spec = pl.BlockSpec((groups, nc, S5_CHUNK * S5_GROUP), lambda i, j: (0, i * nj + j, 0))
    u_shape = jax.ShapeDtypeStruct((groups, b * (l // S5_CHUNK), S5_CHUNK * S5_GROUP), BF16)
    out_w = (qkw, qkw, vw, dm, dm) if latent else (qkw, vw)
    return pl.pallas_call(
        functools.partial(_inproj_kernel, latent=latent, widths=widths),
        grid=(b, nj),
        in_specs=[tok_spec(d),
                  pl.BlockSpec((1, d), lambda i, j: (0, 0)),
                  mod_spec, mod_spec,
                  pl.BlockSpec(w_in.shape, lambda i, j: (0, 0)),
                  pl.BlockSpec((tm, LANES), lambda i, j: (j, 0)),
                  pl.BlockSpec((tm, LANES), lambda i, j: (j, 0))],
        out_specs=[u_spec] + [tok_spec(n) for n in out_w],
        out_shape=[u_shape] + [jax.ShapeDtypeStruct((b, l, n), BF16) for n in out_w],
        scratch_shapes=[pltpu.VMEM((s5w // LANES, tm, LANES), F32)],
        compiler_params=_cparams(("parallel", "parallel")),
        name="inproj_lat" if latent else "inproj_ctx",
    )(x, nw, sh, sc, w_in, cos_t, sin_t)


def _rope_tables(seq_len):
    rows = seq_len // GRID_W
    pairs = DA_HEAD_DIM // 4
    row = np.repeat(np.arange(rows), GRID_W).astype(np.float64)
    col = np.tile(np.arange(GRID_W), rows).astype(np.float64)
    inv_freq = ROPE_THETA ** (-np.arange(pairs, dtype=np.float64) / pairs)
    ra = row[:, None] * inv_freq[None, :]
    ca = col[:, None] * inv_freq[None, :]
    cos64 = np.concatenate([np.cos(ra), np.cos(ra), np.cos(ca), np.cos(ca)], axis=1)
    sin64 = np.concatenate([-np.sin(ra), np.sin(ra), -np.sin(ca), np.sin(ca)], axis=1)
    return (jnp.asarray(np.tile(cos64, (1, 2)), dtype=F32), jnp.asarray(np.tile(sin64, (1, 2)), dtype=F32))


def _s5_matrices(lam_re, lam_im, log_dt, b_re, b_im, c_re, c_im, d_skip):
    hp = lax.Precision.HIGHEST
    t = S5_CHUNK
    lam_re = jnp.minimum(lam_re.astype(F32), S5_DT_MAX_RE)
    lam_im = lam_im.astype(F32)
    dt = jnp.exp(log_dt.astype(F32))[..., None]
    g, p = lam_re.shape[1:]
    hh = S5_GROUP
    mag = jnp.exp(lam_re * dt)
    lb_re = mag * jnp.cos(lam_im * dt)
    lb_im = mag * jnp.sin(lam_im * dt)
    den = lam_re * lam_re + lam_im * lam_im
    num_re = lb_re - 1.0
    co_re = (num_re * lam_re + lb_im * lam_im) / den
    co_im = (lb_im * lam_re - num_re * lam_im) / den
    br = b_re.astype(F32)
    bi = b_im.astype(F32)
    bb_re = co_re[..., None] * br - co_im[..., None] * bi
    bb_im = co_re[..., None] * bi + co_im[..., None] * br
    j = jnp.arange(t + 1, dtype=F32)[None, None, :, None]
    pmag = jnp.exp(lam_re[:, :, None, :] * dt[:, :, None, :] * j)
    pang = lam_im[:, :, None, :] * dt[:, :, None, :] * j
    pw_re = pmag * jnp.cos(pang)
    pw_im = pmag * jnp.sin(pang)
    cr = c_re.astype(F32)
    ci = c_im.astype(F32)
    cp_re = cr[:, :, None] * pw_re[:, :, :t, None, :] - ci[:, :, None] * pw_im[:, :, :t, None, :]
    cp_im = cr[:, :, None] * pw_im[:, :, :t, None, :] + ci[:, :, None] * pw_re[:, :, :t, None, :]
    taps = (jnp.einsum('dgjhp,dgpi->dgjhi', cp_re, bb_re, precision=hp)
            - jnp.einsum('dgjhp,dgpi->dgjhi', cp_im, bb_im, precision=hp))
    sig = np.arange(t)[:, None, None]
    tau = np.arange(t)[None, :, None]
    lag = np.arange(t)[None, None, :]
    place = np.concatenate([tau - sig == lag, sig - tau == lag], axis=2).astype(np.float32)
    taps2 = jnp.concatenate([taps[0], taps[1]], axis=1)
    skip = d_skip.astype(F32).reshape(g, hh)
    diag = (np.eye(t, dtype=np.float32)[None, :, None, :, None]
            * (jnp.eye(hh, dtype=F32)[None] * skip[:, None, :])[:, None, :, None, :])
    m_mat = (jnp.einsum('stj,gjab->gsbta', place, taps2, precision=hp) + diag).reshape(g, t * hh, t * hh)
    pf_re = pw_re[0][:, :t][:, ::-1]
    pf_im = pw_im[0][:, :t][:, ::-1]
    pr_re = pw_re[1][:, :t]
    pr_im = pw_im[1][:, :t]

    def state_in(pr_, pi_, br_, bi_):
        brt = br_.transpose(0, 2, 1)[:, None]
        bit = bi_.transpose(0, 2, 1)[:, None]
        re = pr_[:, :, None, :] * brt - pi_[:, :, None, :] * bit
        im = pr_[:, :, None, :] * bit + pi_[:, :, None, :] * brt
        return re.reshape(g, t * hh, p), im.reshape(g, t * hh, p)

    wf_re, wf_im = state_in(pf_re, pf_im, bb_re[0], bb_im[0])
    wr_re, wr_im = state_in(pr_re, pr_im, bb_re[1], bb_im[1])
    w_mat = jnp.concatenate([wf_re, wr_re, wf_im, wr_im], axis=2)
    ef_re = pw_re[0][:, 1:t + 1]
    ef_im = pw_im[0][:, 1:t + 1]
    er_re = pw_re[1][:, 1:t + 1][:, ::-1]
    er_im = pw_im[1][:, 1:t + 1][:, ::-1]

    def state_out(e_re, e_im, cr_, ci_):
        g_re = cr_[:, None] * e_re[:, :, None, :] - ci_[:, None] * e_im[:, :, None, :]
        g_im = cr_[:, None] * e_im[:, :, None, :] + ci_[:, None] * e_re[:, :, None, :]
        return (g_re.transpose(0, 3, 1, 2).reshape(g, p, t * hh),
                (-g_im).transpose(0, 3, 1, 2).reshape(g, p, t * hh))

    vf_re, vf_im = state_out(ef_re, ef_im, cr[0], ci[0])
    vr_re, vr_im = state_out(er_re, er_im, cr[1], ci[1])
    v_mat = jnp.concatenate([vf_re, vr_re, vf_im, vr_im], axis=1)
    a_mat = jnp.stack([jnp.concatenate([pw_re[0][:, t], pw_re[1][:, t]], axis=1),
                       jnp.concatenate([pw_im[0][:, t], pw_im[1][:, t]], axis=1)], axis=1)
    return w_mat.astype(BF16), m_mat.astype(BF16), v_mat.astype(BF16), a_mat


def _s5_kernel(uc_ref, ul_ref, w_ref, m_ref, v_ref, a_ref, y_ref, sc_scr, sl_scr, hf_scr, hr_scr, *, n_ctx, n_lat):
    nb = SUBLANES
    half = S5_STATE
    ul = ul_ref[0]
    w = w_ref[0]
    s_c = jnp.dot(uc_ref[0], w, preferred_element_type=F32)
    s_l = jnp.dot(ul, w, preferred_element_type=F32)
    for part in range(2):
        for b in range(nb):
            sc_scr[part, pl.ds(b, n_ctx, stride=nb), :] = s_c[b * n_ctx:(b + 1) * n_ctx, part * LANES:(part + 1) * LANES]
            sl_scr[part, pl.ds(b, n_lat, stride=nb), :] = s_l[b * n_lat:(b + 1) * n_lat, part * LANES:(part + 1) * LANES]
    a = a_ref[0]
    a_re = a[0:1, :]
    a_im = a[1:2, :]
    is_fwd = lax.broadcasted_iota(jnp.int32, (nb, LANES), 1) < half

    def tile(j):
        return pl.ds(pl.multiple_of(j * nb, nb), nb)

    def pick(scr, jf, jr):
        return (jnp.where(is_fwd, scr[0, tile(jf), :], scr[0, tile(jr), :]),
                jnp.where(is_fwd, scr[1, tile(jf), :], scr[1, tile(jr), :]))

    def update(h_re, h_im, s_re, s_im):
        return a_re * h_re - a_im * h_im + s_re, a_re * h_im + a_im * h_re + s_im

    def ctx_step(i, carry):
        return update(*carry, *pick(sc_scr, i, n_ctx - 1 - i))

    def lat_step(c, carry):
        h_re, h_im = carry
        cr = n_lat - 1 - c
        hf_scr[0, tile(c), :] = h_re
        hf_scr[1, tile(c), :] = h_im
        hr_scr[0, tile(cr), :] = h_re
        hr_scr[1, tile(cr), :] = h_im
        return update(h_re, h_im, *pick(sl_scr, c, cr))

    zero = jnp.zeros((nb, LANES), F32)
    carry = lax.fori_loop(0, n_ctx, ctx_step, (zero, zero))
    lax.fori_loop(0, n_lat, lat_step, carry, unroll=2)
    lane = lax.broadcasted_iota(jnp.int32, (n_lat, LANES), 1)

    def entry_states(b):
        rows = pl.ds(b, n_lat, stride=nb)
        return jnp.concatenate([jnp.where(lane < half, hf_scr[part, rows, :], hr_scr[part, rows, :])
                                for part in range(2)], axis=1)

    hin = jnp.concatenate([entry_states(b) for b in range(nb)], axis=0).astype(BF16)
    y = jnp.dot(ul, m_ref[0], preferred_element_type=F32)
    y = y + jnp.dot(hin, v_ref[0], preferred_element_type=F32)
    y_ref[0] = y.astype(BF16)


def _s5(u_ctx, u_lat, w_mat, m_mat, v_mat, a_mat, *, n_ctx, n_lat):
    g, rc, k = u_ctx.shape
    rl = u_lat.shape[1]
    mat_spec = pl.BlockSpec((1, k, k), lambda i: (i, 0, 0))
    return pl.pallas_call(
        functools.partial(_s5_kernel, n_ctx=n_ctx, n_lat=n_lat),
        grid=(g,),
        in_specs=[pl.BlockSpec((1, rc, k), lambda i: (i, 0, 0)),
                  pl.BlockSpec((1, rl, k), lambda i: (i, 0, 0)),
                  mat_spec, mat_spec, mat_spec,
                  pl.BlockSpec((1, 2, LANES), lambda i: (i, 0, 0))],
        out_specs=pl.BlockSpec((1, rl, k), lambda i: (i, 0, 0)),
        out_shape=jax.ShapeDtypeStruct((g, rl, k), BF16),
        scratch_shapes=[pltpu.VMEM((2, rc, LANES), F32), pltpu.VMEM((2, rl, LANES), F32),
                        pltpu.VMEM((2, rl, LANES), F32), pltpu.VMEM((2, rl, LANES), F32)],
        compiler_params=_cparams(("parallel",)),
        name="s5",
    )(u_ctx, u_lat, w_mat, m_mat, v_mat, a_mat)


ATTN_KEY_CHUNK = 256
ATTN_ITEM_ROWS = 128


def _attn_kernel(lam_ref, q_ref, ktc_ref, ktl_ref, vc_ref, vl_ref, sw_ref, o_ref, s_a, s_b, p_a, p_b, *, out_scale):
    lam = lam_ref[0, 0]
    sw = sw_ref[...]
    tr = ATTN_ITEM_ROWS
    lk = ktc_ref.shape[3] + ktl_ref.shape[3]
    assert ktc_ref.shape[3] == ATTN_KEY_CHUNK

    def kt_chunk(h, c):
        return ktc_ref[0, h] if c == 0 else ktl_ref[0, h, :, keys(c - 1)]

    def v_chunk(c, lanes):
        return vc_ref[0, :, lanes] if c == 0 else vl_ref[0, keys(c - 1), lanes]

    kc = ATTN_KEY_CHUNK
    nch = lk // kc
    s_bufs = (s_a, s_b)
    p_bufs = (p_a, p_b)
    lane = lax.broadcasted_iota(jnp.int32, (tr, DA_V_DIM), 1)
    work = [(r, h) for r in range(q_ref.shape[1] // tr) for h in range(DA_HEADS)]
    items = [dict() for _ in work]

    def keys(c):
        return slice(c * kc, (c + 1) * kc)

    def rows(i):
        return slice(work[i][0] * tr, (work[i][0] + 1) * tr)

    def cols(i):
        return slice(work[i][1] * DA_V_DIM, (work[i][1] + 1) * DA_V_DIM)

    def stage_a(i, c):
        it = items[i]
        if c == 0:
            q = q_ref[0, rows(i), cols(i)]
            zero = jnp.zeros_like(q)
            it['q2'] = jnp.concatenate([jnp.where(lane < DA_HEAD_DIM, q, zero),
                                        jnp.where(lane >= DA_HEAD_DIM, q, zero)], axis=0)
        s = jnp.dot(it['q2'], kt_chunk(work[i][1], c), preferred_element_type=F32)
        s_bufs[i % 2][:, keys(c)] = s
        mx = jnp.maximum(s[:, :LANES], s[:, LANES:])
        it['mx'] = mx if c == 0 else jnp.maximum(it['mx'], mx)
        if c == nch - 1:
            it['m'] = jnp.broadcast_to(jnp.max(it['mx'], axis=-1, keepdims=True), (2 * tr, kc))

    def stage_b(i, c):
        it = items[i]
        p = jnp.exp2(s_bufs[i % 2][:, keys(c)] - it['m'])
        ps = p[:, :LANES] + p[:, LANES:]
        it['ls'] = ps if c == 0 else it['ls'] + ps
        p_bufs[i % 2][:, keys(c)] = p.astype(BF16)
        if c == nch - 1:
            l = jnp.sum(it['ls'], axis=-1, keepdims=True)
            it['r1'] = jnp.broadcast_to((1.0 / l[:tr]).astype(BF16), (tr, kc))
            it['r2'] = jnp.broadcast_to((lam / l[tr:]).astype(BF16), (tr, kc))

    def stage_c(i, c):
        it = items[i]
        pb = p_bufs[i % 2]
        pd = pb[0:tr, keys(c)] * it['r1'] - pb[tr:2 * tr, keys(c)] * it['r2']
        part = jnp.dot(pd, v_chunk(c, cols(i)), preferred_element_type=F32)
        it['acc'] = part if c == 0 else it['acc'] + part
        if c == nch - 1:
            o_ref[0, rows(i), cols(i)] = (_rms(it['acc']) * sw * out_scale).astype(BF16)

    n = len(work)
    for slot in range(n + 2):
        for c in range(nch):
            if slot < n:
                stage_a(slot, c)
            if 0 <= slot - 1 < n:
                stage_b(slot - 1, c)
            if 0 <= slot - 2 < n:
                stage_c(slot - 2, c)


def _attn(lam, q, kt_ctx, kt_lat, v_ctx, v_lat, subln, *, tq, out_scale):
    b, l, w = q.shape
    lc = v_ctx.shape[1]
    lk = lc + l
    assert l % ATTN_KEY_CHUNK == 0 and tq % ATTN_ITEM_ROWS == 0
    tr = ATTN_ITEM_ROWS
    return pl.pallas_call(
        functools.partial(_attn_kernel, out_scale=out_scale),
        scratch_shapes=[pltpu.VMEM((2 * tr, lk), F32), pltpu.VMEM((2 * tr, lk), F32),
                        pltpu.VMEM((2 * tr, lk), BF16), pltpu.VMEM((2 * tr, lk), BF16)],
        grid=(b, l // tq),
        in_specs=[pl.BlockSpec(memory_space=pltpu.SMEM),
                  pl.BlockSpec((1, tq, w), lambda i, j: (i, j, 0)),
                  pl.BlockSpec((1, DA_HEADS, DA_V_DIM, lc), lambda i, j: (i, 0, 0, 0)),
                  pl.BlockSpec((1, DA_HEADS, DA_V_DIM, l), lambda i, j: (i, 0, 0, 0)),
                  pl.BlockSpec((1, lc, w), lambda i, j: (i, 0, 0)),
                  pl.BlockSpec((1, l, w), lambda i, j: (i, 0, 0)),
                  pl.BlockSpec((1, DA_V_DIM), lambda i, j: (0, 0))],
        out_specs=pl.BlockSpec((1, tq, w), lambda i, j: (i, j, 0)),
        out_shape=jax.ShapeDtypeStruct((b, l, w), BF16),
        compiler_params=_cparams(("parallel", "parallel")),
        name="attn",
    )(lam, q, kt_ctx, kt_lat, v_ctx, v_lat, subln)


def _gelu_tanh(x):
    return 0.5 * x * (1.0 + jnp.tanh(math.sqrt(2.0 / math.pi) * (x + 0.044715 * (x * x * x))))


def _merge_kernel(x_ref, ys_ref, yb_ref, ga_ref, gb_ref, gm_ref, shf_ref, scf_ref,
                  wglu_ref, wpa_ref, wpb_ref, wout_ref, npost_ref, npre_ref, wr2_ref,
                  x1_ref, h2_ref, aff_ref, afft_ref, ys_scr):
    tm, d = x_ref.shape[1:]
    dk = d // LANES
    nr = tm // 2
    state = [dict(), dict()]

    def rows(hf):
        return slice(hf * nr, (hf + 1) * nr)

    def st_relayout(hf):
        _chunks_to_tokens(ys_ref, ys_scr, hf * nr, nr)

    def st_glu(hf):
        st = state[hf]
        st['ya'] = _gelu_tanh(jnp.concatenate([ys_scr[gt, rows(hf), :] for gt in range(ys_scr.shape[0])], axis=1))
        st['glu'] = jnp.dot(st['ya'].astype(BF16), wglu_ref[...], preferred_element_type=F32)

    def st_proj(hf):
        st = state[hf]
        ya = st['ya'] * _sigmoid(st['glu'])
        st['pa'] = jnp.dot(ya.astype(BF16), wpa_ref[...], preferred_element_type=F32)
        st['pb'] = jnp.dot(yb_ref[0, rows(hf), :], wpb_ref[...], preferred_element_type=F32)

    def st_out(hf):
        st = state[hf]
        mix = ga_ref[0, rows(hf), :].astype(F32) * st['pa'] + gb_ref[0, rows(hf), :].astype(F32) * st['pb']
        st['o'] = jnp.dot(mix.astype(BF16), wout_ref[...], preferred_element_type=F32)

    def st_residual(hf):
        st = state[hf]
        x1 = x_ref[0, rows(hf), :] + gm_ref[0] * (_rms(st['o']) * npost_ref[...])
        x1_ref[0, rows(hf), :] = x1
        h = _rms(x1) * npre_ref[...]
        h = h * (1.0 + scf_ref[0]) + shf_ref[0]
        for k in range(dk):
            h2_ref[0, pl.ds(hf * nr * dk + k, nr, stride=dk), :] = h[:, k * LANES:(k + 1) * LANES]
        st['h'] = h

    def st_router(hf):
        st = state[hf]
        h = st['h']
        h_hi = h.astype(BF16)
        h_lo = (h - h_hi.astype(F32)).astype(BF16)
        st['lg'] = jnp.dot(jnp.concatenate([h_hi, h_lo], axis=1), wr2_ref[...], preferred_element_type=F32)

    def st_softmax(hf):
        lg = state[hf]['lg']
        logits = lg[:, :LANES] + lg[:, LANES:]
        lane = lax.broadcasted_iota(jnp.int32, logits.shape, 1)
        logits = jnp.where(lane < N_EXPERTS, logits, -1e30)
        ex = jnp.exp(logits - jnp.max(logits, axis=-1, keepdims=True))
        aff = ex / jnp.sum(ex, axis=-1, keepdims=True)
        aff_ref[0, rows(hf), :] = aff
        afft_ref[0, :, rows(hf)] = aff.T[:N_EXPERTS, :]

    stages = (st_relayout, st_glu, st_proj, st_out, st_residual, st_router, st_softmax)
    for k in range(len(stages) + 1):
        if k < len(stages):
            stages[k](0)
        if k >= 1:
            stages[k - 1](1)


def _merge(x, ys_rows, yb, ga, gb, gm, shf, scf, wglu, wpa, wpb, wout, npost, npre, wr2, *, tm):
    b, l, d = x.shape
    dk = d // LANES
    groups, _, ck = ys_rows.shape
    nj = l // tm

    def tok(n):
        return pl.BlockSpec((1, tm, n), lambda i, j: (i, j, 0))

    def full(a):
        return pl.BlockSpec(a.shape, lambda i, j: (0,) * a.ndim)

    mod = pl.BlockSpec((1, 1, d), lambda i, j: (i, 0, 0))
    return pl.pallas_call(
        _merge_kernel,
        grid=(b, nj),
        in_specs=[tok(d),
                  pl.BlockSpec((groups, tm // S5_CHUNK, ck), lambda i, j: (0, i * nj + j, 0)),
                  tok(yb.shape[2]), tok(d), tok(d), mod, mod, mod,
                  full(wglu), full(wpa), full(wpb), full(wout), full(npost), full(npre), full(wr2)],
        out_specs=[tok(d),
                   pl.BlockSpec((1, tm * dk, LANES), lambda i, j: (i, j, 0)),
                   tok(LANES),
                   pl.BlockSpec((1, N_EXPERTS, tm), lambda i, j: (i, 0, j))],
        out_shape=[jax.ShapeDtypeStruct((b, l, d), F32),
                   jax.ShapeDtypeStruct((b, l * dk, LANES), F32),
                   jax.ShapeDtypeStruct((b, l, LANES), F32),
                   jax.ShapeDtypeStruct((b, N_EXPERTS, l), F32)],
        scratch_shapes=[pltpu.VMEM((groups * S5_GROUP // LANES, tm, LANES), F32)],
        compiler_params=_cparams(("parallel", "parallel")),
        name="merge",
    )(x, ys_rows, yb, ga, gb, gm, shf, scf, wglu, wpa, wpb, wout, npost, npre, wr2)


def _route_kernel(a_ref, tok_ref, tri_ref, idx_ref, pos_scr, *, cap):
    a = a_ref[0]
    ne, l = a.shape
    capf = float(cap)

    def count_ge(thr):
        return jnp.sum(jnp.where(a >= thr, 1.0, 0.0), axis=1, keepdims=True)

    def narrow(mid, lo, hi):
        mid = jnp.minimum(jnp.maximum(mid, lo), hi)
        ok = count_ge(mid) >= capf
        return jnp.where(ok, mid, lo), jnp.where(ok, hi, mid)

    def geo(_, lohi):
        lo, hi = lohi
        return narrow(jnp.sqrt(jnp.maximum(lo, F32_TINY) * hi), lo, hi)

    def ari(_, lohi):
        lo, hi = lohi
        return narrow(0.5 * lo + 0.5 * hi, lo, hi)

    lohi = (jnp.zeros((ne, 1), F32), jnp.full((ne, 1), 2.0, F32))
    lohi = lax.fori_loop(0, 34, geo, lohi)
    lo, hi = lax.fori_loop(0, 8, ari, lohi)
    gt = a >= hi
    eq = (a >= lo) & jnp.logical_not(gt)
    need = capf - jnp.sum(jnp.where(gt, 1.0, 0.0), axis=1, keepdims=True)

    tri = tri_ref[...]

    def excl_cumsum(mask):
        mb = jnp.where(mask, 1.0, 0.0).astype(BF16)
        off = jnp.zeros((ne, 1), F32)
        outs = []
        for j in range(l // LANES):
            blk = mb[:, j * LANES:(j + 1) * LANES]
            outs.append(jnp.dot(blk, tri, preferred_element_type=F32) + off)
            off = off + jnp.sum(blk.astype(F32), axis=1, keepdims=True)
        return jnp.concatenate(outs, axis=1)

    sel = gt | (eq & (excl_cumsum(eq) < need))
    pos_scr[...] = jnp.where(sel, excl_cumsum(sel), -1.0)

    tc = 512
    slot = lax.broadcasted_iota(jnp.int32, (cap, tc), 0).astype(F32)

    def per_expert(e, _):
        acc = jnp.zeros((cap, LANES), F32)
        for c in range(l // tc):
            pc = pos_scr[pl.ds(e, 1), c * tc:(c + 1) * tc]
            onehot = jnp.where(pc == slot, 1.0, 0.0).astype(BF16)
            acc = acc + jnp.dot(onehot, tok_ref[c * tc:(c + 1) * tc, :], preferred_element_type=F32)
        acc_t = acc.T
        idx = acc_t[0:1, :] * 64.0 + acc_t[1:2, :]
        idx_ref[0, pl.ds(e, 1), :] = idx.astype(jnp.int32)
        return 0

    lax.fori_loop(0, ne, per_expert, 0)


def _route(aff_t, *, cap):
    b, ne, l = aff_t.shape
    tok_np = np.zeros((l, LANES), np.float32)
    tok_np[:, 0] = np.arange(l) >> 6
    tok_np[:, 1] = np.arange(l) & 63
    tok = jnp.asarray(tok_np, dtype=BF16)
    tri = jnp.asarray(np.arange(LANES)[:, None] < np.arange(LANES)[None, :], dtype=BF16)
    return pl.pallas_call(
        functools.partial(_route_kernel, cap=cap),
        grid=(b,),
        in_specs=[pl.BlockSpec((1, ne, l), lambda i: (i, 0, 0)),
                  pl.BlockSpec((l, LANES), lambda i: (0, 0)),
                  pl.BlockSpec((LANES, LANES), lambda i: (0, 0))],
        out_specs=pl.BlockSpec((1, ne, cap), lambda i: (i, 0, 0)),
        out_shape=jax.ShapeDtypeStruct((b, ne, cap), jnp.int32),
        scratch_shapes=[pltpu.VMEM((ne, l), F32)],
        compiler_params=_cparams(("parallel",)),
        name="route",
    )(aff_t, tok, tri)


ROW_UNROLL = 4


def _gather_kernel(idx_ref, h2_ref, xs_ref, xg_scr, *, cap):
    dk = xg_scr.shape[0] // cap

    def gather(s, _):
        t = idx_ref[0, 0, s]
        xg_scr[pl.ds(pl.multiple_of(s * dk, dk), dk), :] = h2_ref[0, pl.ds(pl.multiple_of(t * dk, dk), dk), :]
        return 0

    lax.fori_loop(0, cap, gather, 0, unroll=ROW_UNROLL)
    for k in range(dk):
        xs_ref[0, 0, :, k * LANES:(k + 1) * LANES] = xg_scr[pl.ds(k, cap, stride=dk), :].astype(BF16)


def _moe_gather(idx, h2, *, d):
    b, ne, cap = idx.shape
    rows = h2.shape[1]
    return pl.pallas_call(
        functools.partial(_gather_kernel, cap=cap),
        grid=(b, ne),
        in_specs=[pl.BlockSpec((1, 1, cap), lambda i, e: (i * ne + e, 0, 0), memory_space=pltpu.SMEM),
                  pl.BlockSpec((1, rows, LANES), lambda i, e: (i, 0, 0), pipeline_mode=pl.Buffered(1))],
        out_specs=pl.BlockSpec((1, 1, cap, d), lambda i, e: (i, e, 0, 0)),
        out_shape=jax.ShapeDtypeStruct((b, ne, cap, d), BF16),
        scratch_shapes=[pltpu.VMEM((cap * d // LANES, LANES), F32)],
        compiler_params=_cparams(("arbitrary", "arbitrary")),
        name="moe_gather",
    )(idx.reshape(b * ne, 1, cap), h2)


def _expert_kernel(xs_ref, wg_ref, wu_ref, wd_ref, ys_ref, *, nsplit):
    xs = xs_ref[0, 0]
    tf = wg_ref.shape[2] // nsplit
    y = None
    for j in range(nsplit):
        cols = slice(j * tf, (j + 1) * tf)
        gg = jnp.dot(xs, wg_ref[0, :, cols], preferred_element_type=F32)
        uu = jnp.dot(xs, wu_ref[0, :, cols], preferred_element_type=F32)
        hid = (gg * _sigmoid(gg) * uu).astype(BF16)
        part = jnp.dot(hid, wd_ref[0, cols, :], preferred_element_type=F32)
        y = part if y is None else y + part
    ys_ref[0, 0] = y.astype(BF16)


def _moe_experts(xs, wg, wu, wd):
    b, ne, cap, d = xs.shape
    fdim = wg.shape[2]
    tok = pl.BlockSpec((1, 1, cap, d), lambda e, i: (i, e, 0, 0))
    return pl.pallas_call(
        functools.partial(_expert_kernel, nsplit=2),
        grid=(ne, b),
        in_specs=[tok,
                  pl.BlockSpec((1, d, fdim), lambda e, i: (e, 0, 0)),
                  pl.BlockSpec((1, d, fdim), lambda e, i: (e, 0, 0)),
                  pl.BlockSpec((1, fdim, d), lambda e, i: (e, 0, 0))],
        out_specs=tok,
        out_shape=jax.ShapeDtypeStruct((b, ne, cap, d), BF16),
        compiler_params=_cparams(("arbitrary", "arbitrary")),
        name="moe_experts",
    )(xs, wg, wu, wd)


def _scatter_kernel(idx_ref, ys_ref, aff_ref, x1_ref, gf_ref, nw_ref, o_ref, f_scr, y2_scr, g_scr, *, cap, ne):
    e = pl.program_id(1)
    dk = y2_scr.shape[0] // cap
    tm = x1_ref.shape[1]

    @pl.when(e == 0)
    def _():
        f_scr[...] = jnp.zeros_like(f_scr)

    @pl.when(e < ne)
    def _():
        def gates(s, _):
            g_scr[pl.ds(s, 1), :] = aff_ref[0, pl.ds(idx_ref[0, 0, s], 1), :]
            return 0

        lax.fori_loop(0, cap, gates, 0, unroll=ROW_UNROLL)
        lane = lax.broadcasted_iota(jnp.int32, g_scr.shape, 1)
        gate = jnp.sum(jnp.where(lane == e, g_scr[...], 0.0), axis=1, keepdims=True)
        yg = ys_ref[0, 0].astype(F32) * gate
        for k in range(dk):
            y2_scr[pl.ds(k, cap, stride=dk), :] = yg[:, k * LANES:(k + 1) * LANES]

        def scatter(s0, _):
            rows = []
            vals = []
            for i in range(ROW_UNROLL):
                s = s0 * ROW_UNROLL + i
                r = pl.multiple_of(idx_ref[0, 0, s] * dk, dk)
                rows.append(r)
                vals.append(f_scr[pl.ds(r, dk), :] + y2_scr[pl.ds(pl.multiple_of(s * dk, dk), dk), :])
            for r, v in zip(rows, vals):
                f_scr[pl.ds(r, dk), :] = v
            return 0

        lax.fori_loop(0, cap // ROW_UNROLL, scatter, 0)

    @pl.when(e >= ne)
    def _():
        base = (e - ne) * (tm * dk)
        f = jnp.concatenate([f_scr[pl.ds(base + k, tm, stride=dk), :] for k in range(dk)], axis=1)
        o_ref[0] = x1_ref[0] + gf_ref[0] * (_rms(f) * nw_ref[...])


def _moe_scatter_final(idx, ys, aff, x1, gf, nw, *, tm):
    b, ne, cap, d = ys.shape
    l = aff.shape[1]
    dk = d // LANES
    nt = l // tm

    def expert(e):
        return jnp.minimum(e, ne - 1)

    def tile(e):
        return jnp.maximum(e - ne, 0)

    return pl.pallas_call(
        functools.partial(_scatter_kernel, cap=cap, ne=ne),
        grid=(b, ne + nt),
        in_specs=[pl.BlockSpec((1, 1, cap), lambda i, e: (i * ne + expert(e), 0, 0), memory_space=pltpu.SMEM),
                  pl.BlockSpec((1, 1, cap, d), lambda i, e: (i, expert(e), 0, 0)),
                  pl.BlockSpec((1, l, LANES), lambda i, e: (i, 0, 0)),
                  pl.BlockSpec((1, tm, d), lambda i, e: (i, tile(e), 0)),
                  pl.BlockSpec((1, 1, d), lambda i, e: (i, 0, 0)),
                  pl.BlockSpec((1, d), lambda i, e: (0, 0))],
        out_specs=pl.BlockSpec((1, tm, d), lambda i, e: (i, tile(e), 0)),
        out_shape=jax.ShapeDtypeStruct((b, l, d), F32),
        scratch_shapes=[pltpu.VMEM((l * dk, LANES), F32),
                        pltpu.VMEM((cap * dk, LANES), F32),
                        pltpu.VMEM((cap, LANES), F32)],
        compiler_params=_cparams(("arbitrary", "arbitrary")),
        name="moe_scatter_final",
    )(idx.reshape(b * ne, 1, cap), ys, aff, x1, gf, nw)


def kernel(x, c, ctx, c_ctx, w_ada, b_ada, norm_pre_mix, norm_post_mix, norm_pre_ffn, norm_post_ffn, w_in, s5_lam_re, s5_lam_im, s5_log_dt, s5_b_re, s5_b_im, s5_c_re, s5_c_im, s5_d, w_glu, da_lambda, da_subln, w_proj_a, w_proj_b, w_out, w_router, w_exp_gate, w_exp_up, w_exp_down):
    depth = w_ada.shape[0]
    assert depth == 1, "single trunk layer: the context stream's outputs are never consumed"
    b, l, d = x.shape
    lc = ctx.shape[1]
    assert b == SUBLANES and l % (S5_CHUNK * 32) == 0 and lc % (S5_CHUNK * 16) == 0
    s5w = s5_d.shape[1]
    qkw = DA_HEADS * 2 * DA_HEAD_DIM
    vw = DA_HEADS * DA_V_DIM
    widths = (s5w, qkw, vw, d)
    lam_init = 0.8 - 0.6 * math.exp(-0.3 * 0)

    c_all = jnp.zeros((2 * SUBLANES, d), F32).at[:b].set(c).at[b].set(c_ctx)
    mod = _ada(c_all, w_ada[0], b_ada[0])
    sh_m, sc_m, g_m, sh_f, sc_f, g_f = [mod[:b, i * d:(i + 1) * d].reshape(b, 1, d) for i in range(6)]
    csh_m = mod[b:b + 1, 0:d].reshape(1, 1, d)
    csc_m = mod[b:b + 1, d:2 * d].reshape(1, 1, d)

    w_in_b = w_in[0].astype(BF16)
    cos_t, sin_t = _rope_tables(l)
    npm = norm_pre_mix[0].reshape(1, d)
    u, q, k, v, ga, gb = _inproj(x, npm, sh_m, sc_m, w_in_b, cos_t, sin_t, latent=True, widths=widths, tm=1024)
    uc, kc, vc = _inproj(ctx, npm, csh_m, csc_m, w_in_b, cos_t[:lc], sin_t[:lc], latent=False, widths=widths, tm=lc)

    w_mat, m_mat, v_mat, a_mat = _s5_matrices(s5_lam_re[0], s5_lam_im[0], s5_log_dt[0], s5_b_re[0], s5_b_im[0],
                                              s5_c_re[0], s5_c_im[0], s5_d[0])
    ys_rows = _s5(uc, u, w_mat, m_mat, v_mat, a_mat, n_ctx=lc // S5_CHUNK, n_lat=l // S5_CHUNK)

    lq1, lk1, lq2, lk2 = da_lambda[0].astype(F32)
    lam = jnp.exp(jnp.sum(lq1 * lk1)) - jnp.exp(jnp.sum(lq2 * lk2)) + lam_init
    kt_ctx = kc.reshape(b, lc, DA_HEADS, DA_V_DIM).transpose(0, 2, 3, 1)
    kt_lat = k.reshape(b, l, DA_HEADS, DA_V_DIM).transpose(0, 2, 3, 1)
    yb = _attn(lam.reshape(1, 1), q, kt_ctx, kt_lat, vc, v, da_subln[0].reshape(1, DA_V_DIM),
               tq=256, out_scale=1.0 - lam_init)

    wr = jnp.zeros((d, LANES), F32).at[:, :N_EXPERTS].set(w_router[0])
    wr_hi = wr.astype(BF16)
    wr_lo = (wr - wr_hi.astype(F32)).astype(BF16)
    wr2 = jnp.concatenate([jnp.concatenate([wr_hi, wr_lo], axis=1),
                           jnp.concatenate([wr_hi, jnp.zeros_like(wr_hi)], axis=1)], axis=0)
    x1, h2, aff, aff_t = _merge(x, ys_rows, yb, ga, gb, g_m, sh_f, sc_f,
                                w_glu[0].astype(BF16), w_proj_a[0].astype(BF16), w_proj_b[0].astype(BF16),
                                w_out[0].astype(BF16), norm_post_mix[0].reshape(1, d), norm_pre_ffn[0].reshape(1, d),
                                wr2, tm=512)

    cap = EC_CAPACITY * l // N_EXPERTS
    idx = _route(aff_t, cap=cap)
    xs = _moe_gather(idx, h2, d=d)
    ys = _moe_experts(xs, w_exp_gate[0].astype(BF16), w_exp_up[0].astype(BF16), w_exp_down[0].astype(BF16))
    return _moe_scatter_final(idx, ys, aff, x1, g_f, norm_post_ffn[0].reshape(1, d), tm=512)
```

```python
import functools
import math

import numpy as np

import jax
import jax.numpy as jnp
from jax import lax
from jax.experimental import pallas as pl
from jax.experimental.pallas import tpu as pltpu

F32 = jnp.float32
BF16 = jnp.bfloat16

EPS = 1e-6
GRID_W = 64
ROPE_THETA = 10000.0
S5_GROUP = 16
S5_STATE = 64
S5_DT_MAX_RE = -1e-4
S5_CHUNK = 16
DA_HEADS = 4
DA_HEAD_DIM = 64
DA_V_DIM = 128
N_EXPERTS = 16
EC_CAPACITY = 2
LANES = 128
SUBLANES = 8
VMEM_LIMIT = 60 * 1024 * 1024
LOG2E = 1.4426950408889634
F32_TINY = 1e-37


def _cparams(sem):
    return pltpu.CompilerParams(dimension_semantics=sem, vmem_limit_bytes=VMEM_LIMIT)


def _rms(x, eps=EPS):
    return x * lax.rsqrt(jnp.mean(x * x, axis=-1, keepdims=True) + eps)


def _sigmoid(x):
    return 1.0 / (1.0 + jnp.exp(-x))


def _ada_kernel(c_ref, w_ref, b_ref, o_ref):
    c = c_ref[...]
    s = (c * _sigmoid(c)).astype(BF16)
    o_ref[...] = jnp.dot(s, w_ref[...].astype(BF16), preferred_element_type=F32) + b_ref[...]


def _ada(c_all, w_ada, b_ada):
    rows, d = c_all.shape
    n = w_ada.shape[1]
    tn = 512
    return pl.pallas_call(
        _ada_kernel,
        grid=(n // tn,),
        in_specs=[pl.BlockSpec((rows, d), lambda j: (0, 0)),
                  pl.BlockSpec((d, tn), lambda j: (0, j)),
                  pl.BlockSpec((1, tn), lambda j: (0, j))],
        out_specs=pl.BlockSpec((rows, tn), lambda j: (0, j)),
        out_shape=jax.ShapeDtypeStruct((rows, n), F32),
        compiler_params=_cparams(("arbitrary",)),
        name="ada",
    )(c_all, w_ada, b_ada.reshape(1, n))


def _block_transpose8(vs):
    lane_blk = lax.broadcasted_iota(jnp.int32, vs[0].shape, 1) // S5_GROUP
    vs = list(vs)
    for s in (4, 2, 1):
        upper = (lane_blk & s) != 0
        for i in range(8):
            if i & s:
                continue
            a, b = vs[i], vs[i + s]
            vs[i] = jnp.where(upper, pltpu.roll(b, S5_GROUP * s, 1), a)
            vs[i + s] = jnp.where(upper, b, pltpu.roll(a, LANES - S5_GROUP * s, 1))
    return vs


def _tokens_to_chunks(scr, out_ref, row0, nrows):
    nc = nrows // S5_CHUNK
    c0 = row0 // S5_CHUNK
    for gt in range(scr.shape[0]):
        for j in range(S5_CHUNK // 8):
            vs = [scr[gt, pl.ds(row0 + 8 * j + i, nc, stride=S5_CHUNK), :] for i in range(8)]
            vs = _block_transpose8(vs)
            for gi in range(8):
                out_ref[gt * 8 + gi, c0:c0 + nc, j * LANES:(j + 1) * LANES] = vs[gi].astype(out_ref.dtype)


def _chunks_to_tokens(in_ref, scr, row0, nrows):
    nc = nrows // S5_CHUNK
    c0 = row0 // S5_CHUNK
    for gt in range(scr.shape[0]):
        for j in range(S5_CHUNK // 8):
            vs = [in_ref[gt * 8 + gi, c0:c0 + nc, j * LANES:(j + 1) * LANES].astype(F32) for gi in range(8)]
            vs = _block_transpose8(vs)
            for i in range(8):
                scr[gt, pl.ds(row0 + 8 * j + i, nc, stride=S5_CHUNK), :] = vs[i]


def _swap16(x):
    lane = lax.broadcasted_iota(jnp.int32, x.shape, 1)
    return jnp.where((lane & 16) == 0, pltpu.roll(x, LANES - 16, 1), pltpu.roll(x, 16, 1))


def _inproj_kernel(x_ref, nw_ref, sh_ref, sc_ref, w_ref, cos_ref, sin_ref, *refs, latent, widths):
    tm = x_ref.shape[1]
    nparts = 2 if tm % (2 * S5_CHUNK * 16) == 0 else 1
    nr = tm // nparts
    s5w, qkw, vw, dm = widths
    o_u, o_q, o_k, o_v, o_ga = 0, s5w, s5w + qkw, s5w + 2 * qkw, s5w + 2 * qkw + vw
    if latent:
        u_ref, q_ref, k_ref, v_ref, ga_ref, gb_ref, u_scr = refs
    else:
        u_ref, k_ref, v_ref, u_scr = refs
    hs = [None] * nparts

    def rows(part):
        return slice(part * nr, (part + 1) * nr)

    def proj(part, lo, n):
        return jnp.dot(hs[part], w_ref[:, lo:lo + n], preferred_element_type=F32)

    def rope(part, z, scale):
        cos = cos_ref[rows(part), :]
        sin = sin_ref[rows(part), :]
        pieces = []
        for j in range(z.shape[1] // LANES):
            zj = z[:, j * LANES:(j + 1) * LANES]
            pieces.append((zj * cos + _swap16(zj) * sin) * scale)
        return jnp.concatenate(pieces, axis=1)

    def st_norm(part):
        h = _rms(x_ref[0, rows(part), :]) * nw_ref[...]
        hs[part] = (h * (1.0 + sc_ref[0]) + sh_ref[0]).astype(BF16)

    def st_q(part):
        q_ref[0, rows(part), :] = rope(part, proj(part, o_q, qkw), DA_HEAD_DIM ** -0.5 * LOG2E).astype(BF16)

    def st_k(part):
        k = proj(part, o_k, qkw)
        k_ref[0, rows(part), :] = (rope(part, k, 1.0) if latent else k).astype(BF16)

    def st_v(part):
        v_ref[0, rows(part), :] = proj(part, o_v, vw).astype(BF16)

    def st_ga(part):
        ga_ref[0, rows(part), :] = _sigmoid(proj(part, o_ga, dm)).astype(BF16)

    def st_gb(part):
        gb_ref[0, rows(part), :] = _sigmoid(proj(part, o_ga + dm, dm)).astype(BF16)

    def st_u(part):
        u = proj(part, o_u, s5w)
        for gt in range(s5w // LANES):
            u_scr[gt, rows(part), :] = u[:, gt * LANES:(gt + 1) * LANES]
        _tokens_to_chunks(u_scr, u_ref, part * nr, nr)

    stages = (st_norm, st_q, st_k, st_v, st_ga, st_gb, st_u) if latent else (st_norm, st_k, st_v, st_u)
    for step in range(len(stages) + nparts - 1):
        for part in range(nparts):
            if 0 <= step - part < len(stages):
                stages[step - part](part)


def _inproj(x, nw, sh, sc, w_in, cos_t, sin_t, *, latent, widths, tm):
    b, l, d = x.shape
    s5w, qkw, vw, dm = widths
    groups = s5w // S5_GROUP
    nc = tm // S5_CHUNK
    per_b = sh.shape[0] > 1
    mod_spec = pl.BlockSpec((1, 1, d), (lambda i, j: (i, 0, 0)) if per_b else (lambda i, j: (0, 0, 0)))

    def tok_spec(n):
        return pl.BlockSpec((1, tm, n), lambda i, j: (i, j, 0))

    nj = l // tm
    u_spec = pl.BlockSpec((groups, nc, S5_CHUNK * S5_GROUP), lambda i, j: (0, i * nj + j, 0))
    u_shape = jax.ShapeDtypeStruct((groups, b * (l // S5_CHUNK), S5_CHUNK * S5_GROUP), BF16)
    out_w = (qkw, qkw, vw, dm, dm) if latent else (qkw, vw)
    return pl.pallas_call(
        functools.partial(_inproj_kernel, latent=latent, widths=widths),
        grid=(b, nj),
        in_specs=[tok_spec(d),
                  pl.BlockSpec((1, d), lambda i, j: (0, 0)),
                  mod_spec, mod_spec,
                  pl.BlockSpec(w_in.shape, lambda i, j: (0, 0)),
                  pl.BlockSpec((tm, LANES), lambda i, j: (j, 0)),
                  pl.BlockSpec((tm, LANES), lambda i, j: (j, 0))],
        out_specs=[u_spec] + [tok_spec(n) for n in out_w],
        out_shape=[u_shape] + [jax.ShapeDtypeStruct((b, l, n), BF16) for n in out_w],
        scratch_shapes=[pltpu.VMEM((s5w // LANES, tm, LANES), F32)],
        compiler_params=_cparams(("parallel", "parallel")),
        name="inproj_lat" if latent else "inproj_ctx",
    )(x, nw, sh, sc, w_in, cos_t, sin_t)


def _rope_tables(seq_len):
    rows = seq_len // GRID_W
    pairs = DA_HEAD_DIM // 4
    row = np.repeat(np.arange(rows), GRID_W).astype(np.float64)
    col = np.tile(np.arange(GRID_W), rows).astype(np.float64)
    inv_freq = ROPE_THETA ** (-np.arange(pairs, dtype=np.float64) / pairs)
    ra = row[:, None] * inv_freq[None, :]
    ca = col[:, None] * inv_freq[None, :]
    cos64 = np.concatenate([np.cos(ra), np.cos(ra), np.cos(ca), np.cos(ca)], axis=1)
    sin64 = np.concatenate([-np.sin(ra), np.sin(ra), -np.sin(ca), np.sin(ca)], axis=1)
    return (jnp.asarray(np.tile(cos64, (1, 2)), dtype=F32), jnp.asarray(np.tile(sin64, (1, 2)), dtype=F32))


def _s5_matrices(lam_re, lam_im, log_dt, b_re, b_im, c_re, c_im, d_skip):
    hp = lax.Precision.HIGHEST
    t = S5_CHUNK
    lam_re = jnp.minimum(lam_re.astype(F32), S5_DT_MAX_RE)
    lam_im = lam_im.astype(F32)
    dt = jnp.exp(log_dt.astype(F32))[..., None]
    g, p = lam_re.shape[1:]
    hh = S5_GROUP
    mag = jnp.exp(lam_re * dt)
    lb_re = mag * jnp.cos(lam_im * dt)
    lb_im = mag * jnp.sin(lam_im * dt)
    den = lam_re * lam_re + lam_im * lam_im
    num_re = lb_re - 1.0
    co_re = (num_re * lam_re + lb_im * lam_im) / den
    co_im = (lb_im * lam_re - num_re * lam_im) / den
    br = b_re.astype(F32)
    bi = b_im.astype(F32)
    bb_re = co_re[..., None] * br - co_im[..., None] * bi
    bb_im = co_re[..., None] * bi + co_im[..., None] * br
    j = jnp.arange(t + 1, dtype=F32)[None, None, :, None]
    pmag = jnp.exp(lam_re[:, :, None, :] * dt[:, :, None, :] * j)
    pang = lam_im[:, :, None, :] * dt[:, :, None, :] * j
    pw_re = pmag * jnp.cos(pang)
    pw_im = pmag * jnp.sin(pang)
    cr = c_re.astype(F32)
    ci = c_im.astype(F32)
    cp_re = cr[:, :, None] * pw_re[:, :, :t, None, :] - ci[:, :, None] * pw_im[:, :, :t, None, :]
    cp_im = cr[:, :, None] * pw_im[:, :, :t, None, :] + ci[:, :, None] * pw_re[:, :, :t, None, :]
    taps = (jnp.einsum('dgjhp,dgpi->dgjhi', cp_re, bb_re, precision=hp)
            - jnp.einsum('dgjhp,dgpi->dgjhi', cp_im, bb_im, precision=hp))
    sig = np.arange(t)[:, None, None]
    tau = np.arange(t)[None, :, None]
    lag = np.arange(t)[None, None, :]
    place = np.concatenate([tau - sig == lag, sig - tau == lag], axis=2).astype(np.float32)
    taps2 = jnp.concatenate([taps[0], taps[1]], axis=1)
    skip = d_skip.astype(F32).reshape(g, hh)
    diag = (np.eye(t, dtype=np.float32)[None, :, None, :, None]
            * (jnp.eye(hh, dtype=F32)[None] * skip[:, None, :])[:, None, :, None, :])
    m_mat = (jnp.einsum('stj,gjab->gsbta', place, taps2, precision=hp) + diag).reshape(g, t * hh, t * hh)
    pf_re = pw_re[0][:, :t][:, ::-1]
    pf_im = pw_im[0][:, :t][:, ::-1]
    pr_re = pw_re[1][:, :t]
    pr_im = pw_im[1][:, :t]

    def state_in(pr_, pi_, br_, bi_):
        brt = br_.transpose(0, 2, 1)[:, None]
        bit = bi_.transpose(0, 2, 1)[:, None]
        re = pr_[:, :, None, :] * brt - pi_[:, :, None, :] * bit
        im = pr_[:, :, None, :] * bit + pi_[:, :, None, :] * brt
        return re.reshape(g, t * hh, p), im.reshape(g, t * hh, p)

    wf_re, wf_im = state_in(pf_re, pf_im, bb_re[0], bb_im[0])
    wr_re, wr_im = state_in(pr_re, pr_im, bb_re[1], bb_im[1])
    w_mat = jnp.concatenate([wf_re, wr_re, wf_im, wr_im], axis=2)
    ef_re = pw_re[0][:, 1:t + 1]
    ef_im = pw_im[0][:, 1:t + 1]
    er_re = pw_re[1][:, 1:t + 1][:, ::-1]
    er_im = pw_im[1][:, 1:t + 1][:, ::-1]

    def state_out(e_re, e_im, cr_, ci_):
        g_re = cr_[:, None] * e_re[:, :, None, :] - ci_[:, None] * e_im[:, :, None, :]
        g_im = cr_[:, None] * e_im[:, :, None, :] + ci_[:, None] * e_re[:, :, None, :]
        return (g_re.transpose(0, 3, 1, 2).reshape(g, p, t * hh),
                (-g_im).transpose(0, 3, 1, 2).reshape(g, p, t * hh))

    vf_re, vf_im = state_out(ef_re, ef_im, cr[0], ci[0])
    vr_re, vr_im = state_out(er_re, er_im, cr[1], ci[1])
    v_mat = jnp.concatenate([vf_re, vr_re, vf_im, vr_im], axis=1)
    a_mat = jnp.stack([jnp.concatenate([pw_re[0][:, t], pw_re[1][:, t]], axis=1),
                       jnp.concatenate([pw_im[0][:, t], pw_im[1][:, t]], axis=1)], axis=1)
    return w_mat.astype(BF16), m_mat.astype(BF16), v_mat.astype(BF16), a_mat


def _s5_kernel(uc_ref, ul_ref, w_ref, m_ref, v_ref, a_ref, y_ref, sc_scr, sl_scr, hf_scr, hr_scr, *, n_ctx, n_lat):
    nb = SUBLANES
    half = S5_STATE
    ul = ul_ref[0]
    w = w_ref[0]
    s_c = jnp.dot(uc_ref[0], w, preferred_element_type=F32)
    s_l = jnp.dot(ul, w, preferred_element_type=F32)
    for part in range(2):
        for b in range(nb):
            sc_scr[part, pl.ds(b, n_ctx, stride=nb), :] = s_c[b * n_ctx:(b + 1) * n_ctx, part * LANES:(part + 1) * LANES]
            sl_scr[part, pl.ds(b, n_lat, stride=nb), :] = s_l[b * n_lat:(b + 1) * n_lat, part * LANES:(part + 1) * LANES]
    a = a_ref[0]
    a_re = a[0:1, :]
    a_im = a[1:2, :]
    is_fwd = lax.broadcasted_iota(jnp.int32, (nb, LANES), 1) < half

    def tile(j):
        return pl.ds(pl.multiple_of(j * nb, nb), nb)

    def pick(scr, jf, jr):
        return (jnp.where(is_fwd, scr[0, tile(jf), :], scr[0, tile(jr), :]),
                jnp.where(is_fwd, scr[1, tile(jf), :], scr[1, tile(jr), :]))

    def update(h_re, h_im, s_re, s_im):
        return a_re * h_re - a_im * h_im + s_re, a_re * h_im + a_im * h_re + s_im

    def ctx_step(i, carry):
        return update(*carry, *pick(sc_scr, i, n_ctx - 1 - i))

    def lat_step(c, carry):
        h_re, h_im = carry
        cr = n_lat - 1 - c
        hf_scr[0, tile(c), :] = h_re
        hf_scr[1, tile(c), :] = h_im
        hr_scr[0, tile(cr), :] = h_re
        hr_scr[1, tile(cr), :] = h_im
        return update(h_re, h_im, *pick(sl_scr, c, cr))

    zero = jnp.zeros((nb, LANES), F32)
    carry = lax.fori_loop(0, n_ctx, ctx_step, (zero, zero))
    lax.fori_loop(0, n_lat, lat_step, carry, unroll=2)
    lane = lax.broadcasted_iota(jnp.int32, (n_lat, LANES), 1)

    def entry_states(b):
        rows = pl.ds(b, n_lat, stride=nb)
        return jnp.concatenate([jnp.where(lane < half, hf_scr[part, rows, :], hr_scr[part, rows, :])
                                for part in range(2)], axis=1)

    hin = jnp.concatenate([entry_states(b) for b in range(nb)], axis=0).astype(BF16)
    y = jnp.dot(ul, m_ref[0], preferred_element_type=F32)
    y = y + jnp.dot(hin, v_ref[0], preferred_element_type=F32)
    y_ref[0] = y.astype(BF16)


def _s5(u_ctx, u_lat, w_mat, m_mat, v_mat, a_mat, *, n_ctx, n_lat):
    g, rc, k = u_ctx.shape
    rl = u_lat.shape[1]
    mat_spec = pl.BlockSpec((1, k, k), lambda i: (i, 0, 0))
    return pl.pallas_call(
        functools.partial(_s5_kernel, n_ctx=n_ctx, n_lat=n_lat),
        grid=(g,),
        in_specs=[pl.BlockSpec((1, rc, k), lambda i: (i, 0, 0)),
                  pl.BlockSpec((1, rl, k), lambda i: (i, 0, 0)),
                  mat_spec, mat_spec, mat_spec,
                  pl.BlockSpec((1, 2, LANES), lambda i: (i, 0, 0))],
        out_specs=pl.BlockSpec((1, rl, k), lambda i: (i, 0, 0)),
        out_shape=jax.ShapeDtypeStruct((g, rl, k), BF16),
        scratch_shapes=[pltpu.VMEM((2, rc, LANES), F32), pltpu.VMEM((2, rl, LANES), F32),
                        pltpu.VMEM((2, rl, LANES), F32), pltpu.VMEM((2, rl, LANES), F32)],
        compiler_params=_cparams(("parallel",)),
        name="s5",
    )(u_ctx, u_lat, w_mat, m_mat, v_mat, a_mat)


ATTN_KEY_CHUNK = 256
ATTN_ITEM_ROWS = 128


def _attn_kernel(lam_ref, q_ref, ktc_ref, ktl_ref, vc_ref, vl_ref, sw_ref, o_ref, s_a, s_b, p_a, p_b, *, out_scale):
    lam = lam_ref[0, 0]
    sw = sw_ref[...]
    tr = ATTN_ITEM_ROWS
    lk = ktc_ref.shape[3] + ktl_ref.shape[3]
    assert ktc_ref.shape[3] == ATTN_KEY_CHUNK

    def kt_chunk(h, c):
        return ktc_ref[0, h] if c == 0 else ktl_ref[0, h, :, keys(c - 1)]

    def v_chunk(c, lanes):
        return vc_ref[0, :, lanes] if c == 0 else vl_ref[0, keys(c - 1), lanes]

    kc = ATTN_KEY_CHUNK
    nch = lk // kc
    s_bufs = (s_a, s_b)
    p_bufs = (p_a, p_b)
    lane = lax.broadcasted_iota(jnp.int32, (tr, DA_V_DIM), 1)
    work = [(r, h) for r in range(q_ref.shape[1] // tr) for h in range(DA_HEADS)]
    items = [dict() for _ in work]

    def keys(c):
        return slice(c * kc, (c + 1) * kc)

    def rows(i):
        return slice(work[i][0] * tr, (work[i][0] + 1) * tr)

    def cols(i):
        return slice(work[i][1] * DA_V_DIM, (work[i][1] + 1) * DA_V_DIM)

    def stage_a(i, c):
        it = items[i]
        if c == 0:
            q = q_ref[0, rows(i), cols(i)]
            zero = jnp.zeros_like(q)
            it['q2'] = jnp.concatenate([jnp.where(lane < DA_HEAD_DIM, q, zero),
                                        jnp.where(lane >= DA_HEAD_DIM, q, zero)], axis=0)
        s = jnp.dot(it['q2'], kt_chunk(work[i][1], c), preferred_element_type=F32)
        s_bufs[i % 2][:, keys(c)] = s
        mx = jnp.maximum(s[:, :LANES], s[:, LANES:])
        it['mx'] = mx if c == 0 else jnp.maximum(it['mx'], mx)
        if c == nch - 1:
            it['m'] = jnp.broadcast_to(jnp.max(it['mx'], axis=-1, keepdims=True), (2 * tr, kc))

    def stage_b(i, c):
        it = items[i]
        p = jnp.exp2(s_bufs[i % 2][:, keys(c)] - it['m'])
        ps = p[:, :LANES] + p[:, LANES:]
        it['ls'] = ps if c == 0 else it['ls'] + ps
        p_bufs[i % 2][:, keys(c)] = p.astype(BF16)
        if c == nch - 1:
            l = jnp.sum(it['ls'], axis=-1, keepdims=True)
            it['r1'] = jnp.broadcast_to((1.0 / l[:tr]).astype(BF16), (tr, kc))
            it['r2'] = jnp.broadcast_to((lam / l[tr:]).astype(BF16), (tr, kc))

    def stage_c(i, c):
        it = items[i]
        pb = p_bufs[i % 2]
        pd = pb[0:tr, keys(c)] * it['r1'] - pb[tr:2 * tr, keys(c)] * it['r2']
        part = jnp.dot(pd, v_chunk(c, cols(i)), preferred_element_type=F32)
        it['acc'] = part if c == 0 else it['acc'] + part
        if c == nch - 1:
            o_ref[0, rows(i), cols(i)] = (_rms(it['acc']) * sw * out_scale).astype(BF16)

    n = len(work)
    for slot in range(n + 2):
        for c in range(nch):
            if slot < n:
                stage_a(slot, c)
            if 0 <= slot - 1 < n:
                stage_b(slot - 1, c)
            if 0 <= slot - 2 < n:
                stage_c(slot - 2, c)


def _attn(lam, q, kt_ctx, kt_lat, v_ctx, v_lat, subln, *, tq, out_scale):
    b, l, w = q.shape
    lc = v_ctx.shape[1]
    lk = lc + l
    assert l % ATTN_KEY_CHUNK == 0 and tq % ATTN_ITEM_ROWS == 0
    tr = ATTN_ITEM_ROWS
    return pl.pallas_call(
        functools.partial(_attn_kernel, out_scale=out_scale),
        scratch_shapes=[pltpu.VMEM((2 * tr, lk), F32), pltpu.VMEM((2 * tr, lk), F32),
                        pltpu.VMEM((2 * tr, lk), BF16), pltpu.VMEM((2 * tr, lk), BF16)],
        grid=(b, l // tq),
        in_specs=[pl.BlockSpec(memory_space=pltpu.SMEM),
                  pl.BlockSpec((1, tq, w), lambda i, j: (i, j, 0)),
                  pl.BlockSpec((1, DA_HEADS, DA_V_DIM, lc), lambda i, j: (i, 0, 0, 0)),
                  pl.BlockSpec((1, DA_HEADS, DA_V_DIM, l), lambda i, j: (i, 0, 0, 0)),
                  pl.BlockSpec((1, lc, w), lambda i, j: (i, 0, 0)),
                  pl.BlockSpec((1, l, w), lambda i, j: (i, 0, 0)),
                  pl.BlockSpec((1, DA_V_DIM), lambda i, j: (0, 0))],
        out_specs=pl.BlockSpec((1, tq, w), lambda i, j: (i, j, 0)),
        out_shape=jax.ShapeDtypeStruct((b, l, w), BF16),
        compiler_params=_cparams(("parallel", "parallel")),
        name="attn",
    )(lam, q, kt_ctx, kt_lat, v_ctx, v_lat, subln)


def _gelu_tanh(x):
    return 0.5 * x * (1.0 + jnp.tanh(math.sqrt(2.0 / math.pi) * (x + 0.044715 * (x * x * x))))


def _merge_kernel(x_ref, ys_ref, yb_ref, ga_ref, gb_ref, gm_ref, shf_ref, scf_ref,
                  wglu_ref, wpa_ref, wpb_ref, wout_ref, npost_ref, npre_ref, wr2_ref,
                  x1_ref, h2_ref, aff_ref, afft_ref, ys_scr):
    tm, d = x_ref.shape[1:]
    dk = d // LANES
    nr = tm // 2
    state = [dict(), dict()]

    def rows(hf):
        return slice(hf * nr, (hf + 1) * nr)

    def st_relayout(hf):
        _chunks_to_tokens(ys_ref, ys_scr, hf * nr, nr)

    def st_glu(hf):
        st = state[hf]
        st['ya'] = _gelu_tanh(jnp.concatenate([ys_scr[gt, rows(hf), :] for gt in range(ys_scr.shape[0])], axis=1))
        st['glu'] = jnp.dot(st['ya'].astype(BF16), wglu_ref[...], preferred_element_type=F32)

    def st_proj(hf):
        st = state[hf]
        ya = st['ya'] * _sigmoid(st['glu'])
        st['pa'] = jnp.dot(ya.astype(BF16), wpa_ref[...], preferred_element_type=F32)
        st['pb'] = jnp.dot(yb_ref[0, rows(hf), :], wpb_ref[...], preferred_element_type=F32)

    def st_out(hf):
        st = state[hf]
        mix = ga_ref[0, rows(hf), :].astype(F32) * st['pa'] + gb_ref[0, rows(hf), :].astype(F32) * st['pb']
        st['o'] = jnp.dot(mix.astype(BF16), wout_ref[...], preferred_element_type=F32)

    def st_residual(hf):
        st = state[hf]
        x1 = x_ref[0, rows(hf), :] + gm_ref[0] * (_rms(st['o']) * npost_ref[...])
        x1_ref[0, rows(hf), :] = x1
        h = _rms(x1) * npre_ref[...]
        h = h * (1.0 + scf_ref[0]) + shf_ref[0]
        for k in range(dk):
            h2_ref[0, pl.ds(hf * nr * dk + k, nr, stride=dk), :] = h[:, k * LANES:(k + 1) * LANES]
        st['h'] = h

    def st_router(hf):
        st = state[hf]
        h = st['h']
        h_hi = h.astype(BF16)
        h_lo = (h - h_hi.astype(F32)).astype(BF16)
        st['lg'] = jnp.dot(jnp.concatenate([h_hi, h_lo], axis=1), wr2_ref[...], preferred_element_type=F32)

    def st_softmax(hf):
        lg = state[hf]['lg']
        logits = lg[:, :LANES] + lg[:, LANES:]
        lane = lax.broadcasted_iota(jnp.int32, logits.shape, 1)
        logits = jnp.where(lane < N_EXPERTS, logits, -1e30)
        ex = jnp.exp(logits - jnp.max(logits, axis=-1, keepdims=True))
        aff = ex / jnp.sum(ex, axis=-1, keepdims=True)
        aff_ref[0, rows(hf), :] = aff
        afft_ref[0, :, rows(hf)] = aff.T[:N_EXPERTS, :]

    stages = (st_relayout, st_glu, st_proj, st_out, st_residual, st_router, st_softmax)
    for k in range(len(stages) + 1):
        if k < len(stages):
            stages[k](0)
        if k >= 1:
            stages[k - 1](1)


def _merge(x, ys_rows, yb, ga, gb, gm, shf, scf, wglu, wpa, wpb, wout, npost, npre, wr2, *, tm):
    b, l, d = x.shape
    dk = d // LANES
    groups, _, ck = ys_rows.shape
    nj = l // tm

    def tok(n):
        return pl.BlockSpec((1, tm, n), lambda i, j: (i, j, 0))

    def full(a):
        return pl.BlockSpec(a.shape, lambda i, j: (0,) * a.ndim)

    mod = pl.BlockSpec((1, 1, d), lambda i, j: (i, 0, 0))
    return pl.pallas_call(
        _merge_kernel,
        grid=(b, nj),
        in_specs=[tok(d),
                  pl.BlockSpec((groups, tm // S5_CHUNK, ck), lambda i, j: (0, i * nj + j, 0)),
                  tok(yb.shape[2]), tok(d), tok(d), mod, mod, mod,
                  full(wglu), full(wpa), full(wpb), full(wout), full(npost), full(npre), full(wr2)],
        out_specs=[tok(d),
                   pl.BlockSpec((1, tm * dk, LANES), lambda i, j: (i, j, 0)),
                   tok(LANES),
                   pl.BlockSpec((1, N_EXPERTS, tm), lambda i, j: (i, 0, j))],
        out_shape=[jax.ShapeDtypeStruct((b, l, d), F32),
                   jax.ShapeDtypeStruct((b, l * dk, LANES), F32),
                   jax.ShapeDtypeStruct((b, l, LANES), F32),
                   jax.ShapeDtypeStruct((b, N_EXPERTS, l), F32)],
        scratch_shapes=[pltpu.VMEM((groups * S5_GROUP // LANES, tm, LANES), F32)],
        compiler_params=_cparams(("parallel", "parallel")),
        name="merge",
    )(x, ys_rows, yb, ga, gb, gm, shf, scf, wglu, wpa, wpb, wout, npost, npre, wr2)


def _route_kernel(a_ref, tok_ref, tri_ref, idx_ref, pos_scr, *, cap):
    a = a_ref[0]
    ne, l = a.shape
    capf = float(cap)

    def count_ge(thr):
        return jnp.sum(jnp.where(a >= thr, 1.0, 0.0), axis=1, keepdims=True)

    def narrow(mid, lo, hi):
        mid = jnp.minimum(jnp.maximum(mid, lo), hi)
        ok = count_ge(mid) >= capf
        return jnp.where(ok, mid, lo), jnp.where(ok, hi, mid)

    def geo(_, lohi):
        lo, hi = lohi
        return narrow(jnp.sqrt(jnp.maximum(lo, F32_TINY) * hi), lo, hi)

    def ari(_, lohi):
        lo, hi = lohi
        return narrow(0.5 * lo + 0.5 * hi, lo, hi)

    lohi = (jnp.zeros((ne, 1), F32), jnp.full((ne, 1), 2.0, F32))
    lohi = lax.fori_loop(0, 34, geo, lohi)
    lo, hi = lax.fori_loop(0, 8, ari, lohi)
    gt = a >= hi
    eq = (a >= lo) & jnp.logical_not(gt)
    need = capf - jnp.sum(jnp.where(gt, 1.0, 0.0), axis=1, keepdims=True)

    tri = tri_ref[...]

    def excl_cumsum(mask):
        mb = jnp.where(mask, 1.0, 0.0).astype(BF16)
        off = jnp.zeros((ne, 1), F32)
        outs = []
        for j in range(l // LANES):
            blk = mb[:, j * LANES:(j + 1) * LANES]
            outs.append(jnp.dot(blk, tri, preferred_element_type=F32) + off)
            off = off + jnp.sum(blk.astype(F32), axis=1, keepdims=True)
        return jnp.concatenate(outs, axis=1)

    sel = gt | (eq & (excl_cumsum(eq) < need))
    pos_scr[...] = jnp.where(sel, excl_cumsum(sel), -1.0)

    tc = 512
    slot = lax.broadcasted_iota(jnp.int32, (cap, tc), 0).astype(F32)

    def per_expert(e, _):
        acc = jnp.zeros((cap, LANES), F32)
        for c in range(l // tc):
            pc = pos_scr[pl.ds(e, 1), c * tc:(c + 1) * tc]
            onehot = jnp.where(pc == slot, 1.0, 0.0).astype(BF16)
            acc = acc + jnp.dot(onehot, tok_ref[c * tc:(c + 1) * tc, :], preferred_element_type=F32)
        acc_t = acc.T
        idx = acc_t[0:1, :] * 64.0 + acc_t[1:2, :]
        idx_ref[0, pl.ds(e, 1), :] = idx.astype(jnp.int32)
        return 0

    lax.fori_loop(0, ne, per_expert, 0)


def _route(aff_t, *, cap):
    b, ne, l = aff_t.shape
    tok_np = np.zeros((l, LANES), np.float32)
    tok_np[:, 0] = np.arange(l) >> 6
    tok_np[:, 1] = np.arange(l) & 63
    tok = jnp.asarray(tok_np, dtype=BF16)
    tri = jnp.asarray(np.arange(LANES)[:, None] < np.arange(LANES)[None, :], dtype=BF16)
    return pl.pallas_call(
        functools.partial(_route_kernel, cap=cap),
        grid=(b,),
        in_specs=[pl.BlockSpec((1, ne, l), lambda i: (i, 0, 0)),
                  pl.BlockSpec((l, LANES), lambda i: (0, 0)),
                  pl.BlockSpec((LANES, LANES), lambda i: (0, 0))],
        out_specs=pl.BlockSpec((1, ne, cap), lambda i: (i, 0, 0)),
        out_shape=jax.ShapeDtypeStruct((b, ne, cap), jnp.int32),
        scratch_shapes=[pltpu.VMEM((ne, l), F32)],
        compiler_params=_cparams(("parallel",)),
        name="route",
    )(aff_t, tok, tri)


ROW_UNROLL = 4


def _gather_kernel(idx_ref, h2_ref, aff_ref, xs_ref, g_ref, *, cap, dk):
    def gather(s, _):
        t = idx_ref[0, 0, s]
        xs_ref[0, 0, pl.ds(pl.multiple_of(s * dk, dk), dk), :] = h2_ref[0, pl.ds(pl.multiple_of(t * dk, dk), dk), :]
        g_ref[0, 0, pl.ds(s, 1), :] = aff_ref[0, pl.ds(t, 1), :]
        return 0

    lax.fori_loop(0, cap, gather, 0, unroll=ROW_UNROLL)


def _moe_gather(idx, h2, aff, *, d):
    b, ne, cap = idx.shape
    l = aff.shape[1]
    dk = d // LANES
    return pl.pallas_call(
        functools.partial(_gather_kernel, cap=cap, dk=dk),
        grid=(b, ne),
        in_specs=[pl.BlockSpec((1, 1, cap), lambda i, e: (i * ne + e, 0, 0), memory_space=pltpu.SMEM),
                  pl.BlockSpec((1, l * dk, LANES), lambda i, e: (i, 0, 0), pipeline_mode=pl.Buffered(1)),
                  pl.BlockSpec((1, l, LANES), lambda i, e: (i, 0, 0))],
        out_specs=[pl.BlockSpec((1, 1, cap * dk, LANES), lambda i, e: (i, e, 0, 0)),
                   pl.BlockSpec((1, 1, cap, LANES), lambda i, e: (i, e, 0, 0))],
        out_shape=[jax.ShapeDtypeStruct((b, ne, cap * dk, LANES), F32),
                   jax.ShapeDtypeStruct((b, ne, cap, LANES), F32)],
        compiler_params=_cparams(("arbitrary", "arbitrary")),
        name="moe_gather",
    )(idx.reshape(b * ne, 1, cap), h2, aff)


def _expert_kernel(xs_ref, g_ref, wg_ref, wu_ref, wd_ref, ys_ref, *, nsplit):
    e = pl.program_id(0)
    cap = g_ref.shape[2]
    dk = xs_ref.shape[2] // cap
    xs = jnp.concatenate([xs_ref[0, 0, pl.ds(k, cap, stride=dk), :] for k in range(dk)], axis=1).astype(BF16)
    tf = wg_ref.shape[2] // nsplit
    y = None
    for j in range(nsplit):
        cols = slice(j * tf, (j + 1) * tf)
        gg = jnp.dot(xs, wg_ref[0, :, cols], preferred_element_type=F32)
        uu = jnp.dot(xs, wu_ref[0, :, cols], preferred_element_type=F32)
        hid = (gg * _sigmoid(gg) * uu).astype(BF16)
        part = jnp.dot(hid, wd_ref[0, cols, :], preferred_element_type=F32)
        y = part if y is None else y + part
    g = g_ref[0, 0]
    lane = lax.broadcasted_iota(jnp.int32, g.shape, 1)
    gate = jnp.sum(jnp.where(lane == e, g, 0.0), axis=1, keepdims=True)
    yg = y * gate
    for k in range(dk):
        ys_ref[0, 0, pl.ds(k, cap, stride=dk), :] = yg[:, k * LANES:(k + 1) * LANES]


def _moe_experts(xs, gates, wg, wu, wd):
    b, ne, rows, _ = xs.shape
    cap = gates.shape[2]
    d, fdim = wg.shape[1:]
    tok = pl.BlockSpec((1, 1, rows, LANES), lambda e, i: (i, e, 0, 0))
    return pl.pallas_call(
        functools.partial(_expert_kernel, nsplit=2),
        grid=(ne, b),
        in_specs=[tok,
                  pl.BlockSpec((1, 1, cap, LANES), lambda e, i: (i, e, 0, 0)),
                  pl.BlockSpec((1, d, fdim), lambda e, i: (e, 0, 0)),
                  pl.BlockSpec((1, d, fdim), lambda e, i: (e, 0, 0)),
                  pl.BlockSpec((1, fdim, d), lambda e, i: (e, 0, 0))],
        out_specs=tok,
        out_shape=jax.ShapeDtypeStruct((b, ne, rows, LANES), F32),
        compiler_params=_cparams(("arbitrary", "arbitrary")),
        name="moe_experts",
    )(xs, gates, wg, wu, wd)


def _scatter_kernel(idx_ref, ys_ref, x1_ref, gf_ref, nw_ref, o_ref, f_scr, *, cap, ne):
    e = pl.program_id(1)
    dk = ys_ref.shape[2] // cap
    tm = x1_ref.shape[1]

    @pl.when(e == 0)
    def _():
        f_scr[...] = jnp.zeros_like(f_scr)

    @pl.when(e < ne)
    def _():
        def scatter(s0, _):
            rows = []
            vals = []
            for i in range(ROW_UNROLL):
                s = s0 * ROW_UNROLL + i
                r = pl.multiple_of(idx_ref[0, 0, s] * dk, dk)
                rows.append(r)
                vals.append(f_scr[pl.ds(r, dk), :] + ys_ref[0, 0, pl.ds(pl.multiple_of(s * dk, dk), dk), :])
            for r, v in zip(rows, vals):
                f_scr[pl.ds(r, dk), :] = v
            return 0

        lax.fori_loop(0, cap // ROW_UNROLL, scatter, 0)

    @pl.when(e >= ne)
    def _():
        base = (e - ne) * (tm * dk)
        f = jnp.concatenate([f_scr[pl.ds(base + k, tm, stride=dk), :] for k in range(dk)], axis=1)
        o_ref[0] = x1_ref[0] + gf_ref[0] * (_rms(f) * nw_ref[...])


def _moe_scatter_final(idx, ys, x1, gf, nw, *, tm):
    b, ne, cap = idx.shape
    l, d = x1.shape[1:]
    dk = d // LANES
    nt = l // tm

    def expert(e):
        return jnp.minimum(e, ne - 1)

    def tile(e):
        return jnp.maximum(e - ne, 0)

    return pl.pallas_call(
        functools.partial(_scatter_kernel, cap=cap, ne=ne),
        grid=(b, ne + nt),
        in_specs=[pl.BlockSpec((1, 1, cap), lambda i, e: (i * ne + expert(e), 0, 0), memory_space=pltpu.SMEM),
                  pl.BlockSpec((1, 1, cap * dk, LANES), lambda i, e: (i, expert(e), 0, 0)),
                  pl.BlockSpec((1, tm, d), lambda i, e: (i, tile(e), 0)),
                  pl.BlockSpec((1, 1, d), lambda i, e: (i, 0, 0)),
                  pl.BlockSpec((1, d), lambda i, e: (0, 0))],
        out_specs=pl.BlockSpec((1, tm, d), lambda i, e: (i, tile(e), 0)),
        out_shape=jax.ShapeDtypeStruct((b, l, d), F32),
        scratch_shapes=[pltpu.VMEM((l * dk, LANES), F32)],
        compiler_params=_cparams(("arbitrary", "arbitrary")),
        name="moe_scatter_final",
    )(idx.reshape(b * ne, 1, cap), ys, x1, gf, nw)


def kernel(x, c, ctx, c_ctx, w_ada, b_ada, norm_pre_mix, norm_post_mix, norm_pre_ffn, norm_post_ffn, w_in, s5_lam_re, s5_lam_im, s5_log_dt, s5_b_re, s5_b_im, s5_c_re, s5_c_im, s5_d, w_glu, da_lambda, da_subln, w_proj_a, w_proj_b, w_out, w_router, w_exp_gate, w_exp_up, w_exp_down):
    depth = w_ada.shape[0]
    assert depth == 1, "single trunk layer: the context stream's outputs are never consumed"
    b, l, d = x.shape
    lc = ctx.shape[1]
    assert b == SUBLANES and l % (S5_CHUNK * 32) == 0 and lc % (S5_CHUNK * 16) == 0
    s5w = s5_d.shape[1]
    qkw = DA_HEADS * 2 * DA_HEAD_DIM
    vw = DA_HEADS * DA_V_DIM
    widths = (s5w, qkw, vw, d)
    lam_init = 0.8 - 0.6 * math.exp(-0.3 * 0)

    c_all = jnp.zeros((2 * SUBLANES, d), F32).at[:b].set(c).at[b].set(c_ctx)
    mod = _ada(c_all, w_ada[0], b_ada[0])
    sh_m, sc_m, g_m, sh_f, sc_f, g_f = [mod[:b, i * d:(i + 1) * d].reshape(b, 1, d) for i in range(6)]
    csh_m = mod[b:b + 1, 0:d].reshape(1, 1, d)
    csc_m = mod[b:b + 1, d:2 * d].reshape(1, 1, d)

    w_in_b = w_in[0].astype(BF16)
    cos_t, sin_t = _rope_tables(l)
    npm = norm_pre_mix[0].reshape(1, d)
    u, q, k, v, ga, gb = _inproj(x, npm, sh_m, sc_m, w_in_b, cos_t, sin_t, latent=True, widths=widths, tm=1024)
    uc, kc, vc = _inproj(ctx, npm, csh_m, csc_m, w_in_b, cos_t[:lc], sin_t[:lc], latent=False, widths=widths, tm=lc)

    w_mat, m_mat, v_mat, a_mat = _s5_matrices(s5_lam_re[0], s5_lam_im[0], s5_log_dt[0], s5_b_re[0], s5_b_im[0],
                                              s5_c_re[0], s5_c_im[0], s5_d[0])
    ys_rows = _s5(uc, u, w_mat, m_mat, v_mat, a_mat, n_ctx=lc // S5_CHUNK, n_lat=l // S5_CHUNK)

    lq1, lk1, lq2, lk2 = da_lambda[0].astype(F32)
    lam = jnp.exp(jnp.sum(lq1 * lk1)) - jnp.exp(jnp.sum(lq2 * lk2)) + lam_init
    kt_ctx = kc.reshape(b, lc, DA_HEADS, DA_V_DIM).transpose(0, 2, 3, 1)
    kt_lat = k.reshape(b, l, DA_HEADS, DA_V_DIM).transpose(0, 2, 3, 1)
    yb = _attn(lam.reshape(1, 1), q, kt_ctx, kt_lat, vc, v, da_subln[0].reshape(1, DA_V_DIM),
               tq=256, out_scale=1.0 - lam_init)

    wr = jnp.zeros((d, LANES), F32).at[:, :N_EXPERTS].set(w_router[0])
    wr_hi = wr.astype(BF16)
    wr_lo = (wr - wr_hi.astype(F32)).astype(BF16)
    wr2 = jnp.concatenate([jnp.concatenate([wr_hi, wr_lo], axis=1),
                           jnp.concatenate([wr_hi, jnp.zeros_like(wr_hi)], axis=1)], axis=0)
    x1, h2, aff, aff_t = _merge(x, ys_rows, yb, ga, gb, g_m, sh_f, sc_f,
                                w_glu[0].astype(BF16), w_proj_a[0].astype(BF16), w_proj_b[0].astype(BF16),
                                w_out[0].astype(BF16), norm_post_mix[0].reshape(1, d), norm_pre_ffn[0].reshape(1, d),
                                wr2, tm=512)

    cap = EC_CAPACITY * l // N_EXPERTS
    idx = _route(aff_t, cap=cap)
    xs, gates = _moe_gather(idx, h2, aff, d=d)
    ys = _moe_experts(xs, gates, w_exp_gate[0].astype(BF16), w_exp_up[0].astype(BF16), w_exp_down[0].astype(BF16))
    return _moe_scatter_final(idx, ys, x1, g_f, norm_post_ffn[0].reshape(1, d), tm=512)
```

```python
import functools
import math

import numpy as np

import jax
import jax.numpy as jnp
from jax import lax
from jax.experimental import pallas as pl
from jax.experimental.pallas import tpu as pltpu

F32 = jnp.float32
BF16 = jnp.bfloat16

EPS = 1e-6
GRID_W = 64
ROPE_THETA = 10000.0
S5_GROUP = 16
S5_STATE = 64
S5_DT_MAX_RE = -1e-4
S5_CHUNK = 16
DA_HEADS = 4
DA_HEAD_DIM = 64
DA_V_DIM = 128
N_EXPERTS = 16
EC_CAPACITY = 2
LANES = 128
SUBLANES = 8
VMEM_LIMIT = 60 * 1024 * 1024
LOG2E = 1.4426950408889634
F32_TINY = 1e-37


def _cparams(sem):
    return pltpu.CompilerParams(dimension_semantics=sem, vmem_limit_bytes=VMEM_LIMIT)


def _rms(x, eps=EPS):
    return x * lax.rsqrt(jnp.mean(x * x, axis=-1, keepdims=True) + eps)


def _sigmoid(x):
    return 1.0 / (1.0 + jnp.exp(-x))


def _ada_kernel(c_ref, w_ref, b_ref, o_ref):
    c = c_ref[...]
    s = (c * _sigmoid(c)).astype(BF16)
    o_ref[...] = jnp.dot(s, w_ref[...].astype(BF16), preferred_element_type=F32) + b_ref[...]


def _ada(c_all, w_ada, b_ada):
    rows, d = c_all.shape
    n = w_ada.shape[1]
    tn = 512
    return pl.pallas_call(
        _ada_kernel,
        grid=(n // tn,),
        in_specs=[pl.BlockSpec((rows, d), lambda j: (0, 0)),
                  pl.BlockSpec((d, tn), lambda j: (0, j)),
                  pl.BlockSpec((1, tn), lambda j: (0, j))],
        out_specs=pl.BlockSpec((rows, tn), lambda j: (0, j)),
        out_shape=jax.ShapeDtypeStruct((rows, n), F32),
        compiler_params=_cparams(("arbitrary",)),
        name="ada",
    )(c_all, w_ada, b_ada.reshape(1, n))


def _block_transpose8(vs):
    lane_blk = lax.broadcasted_iota(jnp.int32, vs[0].shape, 1) // S5_GROUP
    vs = list(vs)
    for s in (4, 2, 1):
        upper = (lane_blk & s) != 0
        for i in range(8):
            if i & s:
                continue
            a, b = vs[i], vs[i + s]
            vs[i] = jnp.where(upper, pltpu.roll(b, S5_GROUP * s, 1), a)
            vs[i + s] = jnp.where(upper, b, pltpu.roll(a, LANES - S5_GROUP * s, 1))
    return vs


def _tokens_to_chunks(scr, out_ref, row0, nrows):
    nc = nrows // S5_CHUNK
    c0 = row0 // S5_CHUNK
    for gt in range(scr.shape[0]):
        for j in range(S5_CHUNK // 8):
            vs = [scr[gt, pl.ds(row0 + 8 * j + i, nc, stride=S5_CHUNK), :] for i in range(8)]
            vs = _block_transpose8(vs)
            for gi in range(8):
                out_ref[gt * 8 + gi, c0:c0 + nc, j * LANES:(j + 1) * LANES] = vs[gi].astype(out_ref.dtype)


def _chunks_to_tokens(in_ref, scr, row0, nrows):
    nc = nrows // S5_CHUNK
    c0 = row0 // S5_CHUNK
    for gt in range(scr.shape[0]):
        for j in range(S5_CHUNK // 8):
            vs = [in_ref[gt * 8 + gi, c0:c0 + nc, j * LANES:(j + 1) * LANES].astype(F32) for gi in range(8)]
            vs = _block_transpose8(vs)
            for i in range(8):
                scr[gt, pl.ds(row0 + 8 * j + i, nc, stride=S5_CHUNK), :] = vs[i]


def _swap16(x):
    lane = lax.broadcasted_iota(jnp.int32, x.shape, 1)
    return jnp.where((lane & 16) == 0, pltpu.roll(x, LANES - 16, 1), pltpu.roll(x, 16, 1))


def _inproj_kernel(x_ref, nw_ref, sh_ref, sc_ref, w_ref, cos_ref, sin_ref, *refs, latent, widths):
    tm = x_ref.shape[1]
    nparts = 2 if tm % (2 * S5_CHUNK * 16) == 0 else 1
    nr = tm // nparts
    s5w, qkw, vw, dm = widths
    o_u, o_q, o_k, o_v, o_ga = 0, s5w, s5w + qkw, s5w + 2 * qkw, s5w + 2 * qkw + vw
    if latent:
        u_ref, q_ref, k_ref, v_ref, ga_ref, gb_ref, u_scr = refs
    else:
        u_ref, k_ref, v_ref, u_scr = refs
    hs = [None] * nparts

    def rows(part):
        return slice(part * nr, (part + 1) * nr)

    def proj(part, lo, n):
        return jnp.dot(hs[part], w_ref[:, lo:lo + n], preferred_element_type=F32)

    def rope(part, z, scale):
        cos = cos_ref[rows(part), :]
        sin = sin_ref[rows(part), :]
        pieces = []
        for j in range(z.shape[1] // LANES):
            zj = z[:, j * LANES:(j + 1) * LANES]
            pieces.append((zj * cos + _swap16(zj) * sin) * scale)
        return jnp.concatenate(pieces, axis=1)

    def st_norm(part):
        h = _rms(x_ref[0, rows(part), :]) * nw_ref[...]
        hs[part] = (h * (1.0 + sc_ref[0]) + sh_ref[0]).astype(BF16)

    def st_q(part):
        q_ref[0, rows(part), :] = rope(part, proj(part, o_q, qkw), DA_HEAD_DIM ** -0.5 * LOG2E).astype(BF16)

    def st_k(part):
        k = proj(part, o_k, qkw)
        k_ref[0, rows(part), :] = (rope(part, k, 1.0) if latent else k).astype(BF16)

    def st_v(part):
        v_ref[0, rows(part), :] = proj(part, o_v, vw).astype(BF16)

    def st_ga(part):
        ga_ref[0, rows(part), :] = _sigmoid(proj(part, o_ga, dm)).astype(BF16)

    def st_gb(part):
        gb_ref[0, rows(part), :] = _sigmoid(proj(part, o_ga + dm, dm)).astype(BF16)

    def st_u(part):
        u = proj(part, o_u, s5w)
        for gt in range(s5w // LANES):
            u_scr[gt, rows(part), :] = u[:, gt * LANES:(gt + 1) * LANES]
        _tokens_to_chunks(u_scr, u_ref, part * nr, nr)

    stages = (st_norm, st_q, st_k, st_v, st_ga, st_gb, st_u) if latent else (st_norm, st_k, st_v, st_u)
    for step in range(len(stages) + nparts - 1):
        for part in range(nparts):
            if 0 <= step - part < len(stages):
                stages[step - part](part)


def _inproj(x, nw, sh, sc, w_in, cos_t, sin_t, *, latent, widths, tm):
    b, l, d = x.shape
    s5w, qkw, vw, dm = widths
    groups = s5w // S5_GROUP
    nc = tm // S5_CHUNK
    per_b = sh.shape[0] > 1
    mod_spec = pl.BlockSpec((1, 1, d), (lambda i, j: (i, 0, 0)) if per_b else (lambda i, j: (0, 0, 0)))

    def tok_spec(n):
        return pl.BlockSpec((1, tm, n), lambda i, j: (i, j, 0))

    nj = l // tm
    u_spec = pl.BlockSpec((groups, nc, S5_CHUNK * S5_GROUP), lambda i, j: (0, i * nj + j, 0))
    u_shape = jax.ShapeDtypeStruct((groups, b * (l // S5_CHUNK), S5_CHUNK * S5_GROUP), BF16)
    out_w = (qkw, qkw, vw, dm, dm) if latent else (qkw, vw)
    return pl.pallas_call(
        functools.partial(_inproj_kernel, latent=latent, widths=widths),
        grid=(b, nj),
        in_specs=[tok_spec(d),
                  pl.BlockSpec((1, d), lambda i, j: (0, 0)),
                  mod_spec, mod_spec,
                  pl.BlockSpec(w_in.shape, lambda i, j: (0, 0)),
                  pl.BlockSpec((tm, LANES), lambda i, j: (j, 0)),
                  pl.BlockSpec((tm, LANES), lambda i, j: (j, 0))],
        out_specs=[u_spec] + [tok_spec(n) for n in out_w],
        out_shape=[u_shape] + [jax.ShapeDtypeStruct((b, l, n), BF16) for n in out_w],
        scratch_shapes=[pltpu.VMEM((s5w // LANES, tm, LANES), F32)],
        compiler_params=_cparams(("parallel", "parallel")),
        name="inproj_lat" if latent else "inproj_ctx",
    )(x, nw, sh, sc, w_in, cos_t, sin_t)


def _rope_tables(seq_len):
    rows = seq_len // GRID_W
    pairs = DA_HEAD_DIM // 4
    row = np.repeat(np.arange(rows), GRID_W).astype(np.float64)
    col = np.tile(np.arange(GRID_W), rows).astype(np.float64)
    inv_freq = ROPE_THETA ** (-np.arange(pairs, dtype=np.float64) / pairs)
    ra = row[:, None] * inv_freq[None, :]
    ca = col[:, None] * inv_freq[None, :]
    cos64 = np.concatenate([np.cos(ra), np.cos(ra), np.cos(ca), np.cos(ca)], axis=1)
    sin64 = np.concatenate([-np.sin(ra), np.sin(ra), -np.sin(ca), np.sin(ca)], axis=1)
    return (jnp.asarray(np.tile(cos64, (1, 2)), dtype=F32), jnp.asarray(np.tile(sin64, (1, 2)), dtype=F32))


def _s5_matrices(lam_re, lam_im, log_dt, b_re, b_im, c_re, c_im, d_skip):
    hp = lax.Precision.HIGHEST
    t = S5_CHUNK
    lam_re = jnp.minimum(lam_re.astype(F32), S5_DT_MAX_RE)
    lam_im = lam_im.astype(F32)
    dt = jnp.exp(log_dt.astype(F32))[..., None]
    g, p = lam_re.shape[1:]
    hh = S5_GROUP
    mag = jnp.exp(lam_re * dt)
    lb_re = mag * jnp.cos(lam_im * dt)
    lb_im = mag * jnp.sin(lam_im * dt)
    den = lam_re * lam_re + lam_im * lam_im
    num_re = lb_re - 1.0
    co_re = (num_re * lam_re + lb_im * lam_im) / den
    co_im = (lb_im * lam_re - num_re * lam_im) / den
    br = b_re.astype(F32)
    bi = b_im.astype(F32)
    bb_re = co_re[..., None] * br - co_im[..., None] * bi
    bb_im = co_re[..., None] * bi + co_im[..., None] * br
    j = jnp.arange(t + 1, dtype=F32)[None, None, :, None]
    pmag = jnp.exp(lam_re[:, :, None, :] * dt[:, :, None, :] * j)
    pang = lam_im[:, :, None, :] * dt[:, :, None, :] * j
    pw_re = pmag * jnp.cos(pang)
    pw_im = pmag * jnp.sin(pang)
    cr = c_re.astype(F32)
    ci = c_im.astype(F32)
    cp_re = cr[:, :, None] * pw_re[:, :, :t, None, :] - ci[:, :, None] * pw_im[:, :, :t, None, :]
    cp_im = cr[:, :, None] * pw_im[:, :, :t, None, :] + ci[:, :, None] * pw_re[:, :, :t, None, :]
    taps = (jnp.einsum('dgjhp,dgpi->dgjhi', cp_re, bb_re, precision=hp)
            - jnp.einsum('dgjhp,dgpi->dgjhi', cp_im, bb_im, precision=hp))
    sig = np.arange(t)[:, None, None]
    tau = np.arange(t)[None, :, None]
    lag = np.arange(t)[None, None, :]
    place = np.concatenate([tau - sig == lag, sig - tau == lag], axis=2).astype(np.float32)
    taps2 = jnp.concatenate([taps[0], taps[1]], axis=1)
    skip = d_skip.astype(F32).reshape(g, hh)
    diag = (np.eye(t, dtype=np.float32)[None, :, None, :, None]
            * (jnp.eye(hh, dtype=F32)[None] * skip[:, None, :])[:, None, :, None, :])
    m_mat = (jnp.einsum('stj,gjab->gsbta', place, taps2, precision=hp) + diag).reshape(g, t * hh, t * hh)
    pf_re = pw_re[0][:, :t][:, ::-1]
    pf_im = pw_im[0][:, :t][:, ::-1]
    pr_re = pw_re[1][:, :t]
    pr_im = pw_im[1][:, :t]

    def state_in(pr_, pi_, br_, bi_):
        brt = br_.transpose(0, 2, 1)[:, None]
        bit = bi_.transpose(0, 2, 1)[:, None]
        re = pr_[:, :, None, :] * brt - pi_[:, :, None, :] * bit
        im = pr_[:, :, None, :] * bit + pi_[:, :, None, :] * brt
        return re.reshape(g, t * hh, p), im.reshape(g, t * hh, p)

    wf_re, wf_im = state_in(pf_re, pf_im, bb_re[0], bb_im[0])
    wr_re, wr_im = state_in(pr_re, pr_im, bb_re[1], bb_im[1])
    w_mat = jnp.concatenate([wf_re, wr_re, wf_im, wr_im], axis=2)
    ef_re = pw_re[0][:, 1:t + 1]
    ef_im = pw_im[0][:, 1:t + 1]
    er_re = pw_re[1][:, 1:t + 1][:, ::-1]
    er_im = pw_im[1][:, 1:t + 1][:, ::-1]

    def state_out(e_re, e_im, cr_, ci_):
        g_re = cr_[:, None] * e_re[:, :, None, :] - ci_[:, None] * e_im[:, :, None, :]
        g_im = cr_[:, None] * e_im[:, :, None, :] + ci_[:, None] * e_re[:, :, None, :]
        return (g_re.transpose(0, 3, 1, 2).reshape(g, p, t * hh),
                (-g_im).transpose(0, 3, 1, 2).reshape(g, p, t * hh))

    vf_re, vf_im = state_out(ef_re, ef_im, cr[0], ci[0])
    vr_re, vr_im = state_out(er_re, er_im, cr[1], ci[1])
    v_mat = jnp.concatenate([vf_re, vr_re, vf_im, vr_im], axis=1)
    a_mat = jnp.stack([jnp.concatenate([pw_re[0][:, t], pw_re[1][:, t]], axis=1),
                       jnp.concatenate([pw_im[0][:, t], pw_im[1][:, t]], axis=1)], axis=1)
    return w_mat.astype(BF16), m_mat.astype(BF16), v_mat.astype(BF16), a_mat


def _s5_kernel(uc_ref, ul_ref, w_ref, m_ref, v_ref, a_ref, y_ref, sc_scr, sl_scr, hf_scr, hr_scr, *, n_ctx, n_lat):
    nb = SUBLANES
    half = S5_STATE
    ul = ul_ref[0]
    w = w_ref[0]
    s_c = jnp.dot(uc_ref[0], w, preferred_element_type=F32)
    s_l = jnp.dot(ul, w, preferred_element_type=F32)
    for part in range(2):
        for b in range(nb):
            sc_scr[part, pl.ds(b, n_ctx, stride=nb), :] = s_c[b * n_ctx:(b + 1) * n_ctx, part * LANES:(part + 1) * LANES]
            sl_scr[part, pl.ds(b, n_lat, stride=nb), :] = s_l[b * n_lat:(b + 1) * n_lat, part * LANES:(part + 1) * LANES]
    a = a_ref[0]
    a_re = a[0:1, :]
    a_im = a[1:2, :]
    is_fwd = lax.broadcasted_iota(jnp.int32, (nb, LANES), 1) < half

    def tile(j):
        return pl.ds(pl.multiple_of(j * nb, nb), nb)

    def pick(scr, jf, jr):
        return (jnp.where(is_fwd, scr[0, tile(jf), :], scr[0, tile(jr), :]),
                jnp.where(is_fwd, scr[1, tile(jf), :], scr[1, tile(jr), :]))

    def update(h_re, h_im, s_re, s_im):
        return a_re * h_re - a_im * h_im + s_re, a_re * h_im + a_im * h_re + s_im

    def ctx_step(i, carry):
        return update(*carry, *pick(sc_scr, i, n_ctx - 1 - i))

    def lat_step(c, carry):
        h_re, h_im = carry
        cr = n_lat - 1 - c
        hf_scr[0, tile(c), :] = h_re
        hf_scr[1, tile(c), :] = h_im
        hr_scr[0, tile(cr), :] = h_re
        hr_scr[1, tile(cr), :] = h_im
        return update(h_re, h_im, *pick(sl_scr, c, cr))

    zero = jnp.zeros((nb, LANES), F32)
    carry = lax.fori_loop(0, n_ctx, ctx_step, (zero, zero))
    lax.fori_loop(0, n_lat, lat_step, carry, unroll=2)
    lane = lax.broadcasted_iota(jnp.int32, (n_lat, LANES), 1)

    def entry_states(b):
        rows = pl.ds(b, n_lat, stride=nb)
        return jnp.concatenate([jnp.where(lane < half, hf_scr[part, rows, :], hr_scr[part, rows, :])
                                for part in range(2)], axis=1)

    hin = jnp.concatenate([entry_states(b) for b in range(nb)], axis=0).astype(BF16)
    y = jnp.dot(ul, m_ref[0], preferred_element_type=F32)
    y = y + jnp.dot(hin, v_ref[0], preferred_element_type=F32)
    y_ref[0] = y.astype(BF16)


def _s5(u_ctx, u_lat, w_mat, m_mat, v_mat, a_mat, *, n_ctx, n_lat):
    g, rc, k = u_ctx.shape
    rl = u_lat.shape[1]
    mat_spec = pl.BlockSpec((1, k, k), lambda i: (i, 0, 0))
    return pl.pallas_call(
        functools.partial(_s5_kernel, n_ctx=n_ctx, n_lat=n_lat),
        grid=(g,),
        in_specs=[pl.BlockSpec((1, rc, k), lambda i: (i, 0, 0)),
                  pl.BlockSpec((1, rl, k), lambda i: (i, 0, 0)),
                  mat_spec, mat_spec, mat_spec,
                  pl.BlockSpec((1, 2, LANES), lambda i: (i, 0, 0))],
        out_specs=pl.BlockSpec((1, rl, k), lambda i: (i, 0, 0)),
        out_shape=jax.ShapeDtypeStruct((g, rl, k), BF16),
        scratch_shapes=[pltpu.VMEM((2, rc, LANES), F32), pltpu.VMEM((2, rl, LANES), F32),
                        pltpu.VMEM((2, rl, LANES), F32), pltpu.VMEM((2, rl, LANES), F32)],
        compiler_params=_cparams(("parallel",)),
        name="s5",
    )(u_ctx, u_lat, w_mat, m_mat, v_mat, a_mat)


ATTN_KEY_CHUNK = 256
ATTN_ITEM_ROWS = 128


def _attn_kernel(lam_ref, q_ref, ktc_ref, ktl_ref, vc_ref, vl_ref, sw_ref, o_ref, s_a, s_b, p_a, p_b, *, out_scale):
    lam = lam_ref[0, 0]
    sw = sw_ref[...]
    tr = ATTN_ITEM_ROWS
    lk = ktc_ref.shape[3] + ktl_ref.shape[3]
    assert ktc_ref.shape[3] == ATTN_KEY_CHUNK

    def kt_chunk(h, c):
        return ktc_ref[0, h] if c == 0 else ktl_ref[0, h, :, keys(c - 1)]

    def v_chunk(c, lanes):
        return vc_ref[0, :, lanes] if c == 0 else vl_ref[0, keys(c - 1), lanes]

    kc = ATTN_KEY_CHUNK
    nch = lk // kc
    s_bufs = (s_a, s_b)
    p_bufs = (p_a, p_b)
    lane = lax.broadcasted_iota(jnp.int32, (tr, DA_V_DIM), 1)
    work = [(r, h) for r in range(q_ref.shape[1] // tr) for h in range(DA_HEADS)]
    items = [dict() for _ in work]

    def keys(c):
        return slice(c * kc, (c + 1) * kc)

    def rows(i):
        return slice(work[i][0] * tr, (work[i][0] + 1) * tr)

    def cols(i):
        return slice(work[i][1] * DA_V_DIM, (work[i][1] + 1) * DA_V_DIM)

    def stage_a(i, c):
        it = items[i]
        if c == 0:
            q = q_ref[0, rows(i), cols(i)]
            zero = jnp.zeros_like(q)
            it['q2'] = jnp.concatenate([jnp.where(lane < DA_HEAD_DIM, q, zero),
                                        jnp.where(lane >= DA_HEAD_DIM, q, zero)], axis=0)
        s = jnp.dot(it['q2'], kt_chunk(work[i][1], c), preferred_element_type=F32)
        s_bufs[i % 2][:, keys(c)] = s
        mx = jnp.maximum(s[:, :LANES], s[:, LANES:])
        it['mx'] = mx if c == 0 else jnp.maximum(it['mx'], mx)
        if c == nch - 1:
            it['m'] = jnp.broadcast_to(jnp.max(it['mx'], axis=-1, keepdims=True), (2 * tr, kc))

    def stage_b(i, c):
        it = items[i]
        p = jnp.exp2(s_bufs[i % 2][:, keys(c)] - it['m'])
        ps = p[:, :LANES] + p[:, LANES:]
        it['ls'] = ps if c == 0 else it['ls'] + ps
        p_bufs[i % 2][:, keys(c)] = p.astype(BF16)
        if c == nch - 1:
            l = jnp.sum(it['ls'], axis=-1, keepdims=True)
            it['r1'] = jnp.broadcast_to((1.0 / l[:tr]).astype(BF16), (tr, kc))
            it['r2'] = jnp.broadcast_to((lam / l[tr:]).astype(BF16), (tr, kc))

    def stage_c(i, c):
        it = items[i]
        pb = p_bufs[i % 2]
        pd = pb[0:tr, keys(c)] * it['r1'] - pb[tr:2 * tr, keys(c)] * it['r2']
        part = jnp.dot(pd, v_chunk(c, cols(i)), preferred_element_type=F32)
        it['acc'] = part if c == 0 else it['acc'] + part
        if c == nch - 1:
            o_ref[0, rows(i), cols(i)] = (_rms(it['acc']) * sw * out_scale).astype(BF16)

    n = len(work)
    for slot in range(n + 2):
        for c in range(nch):
            if slot < n:
                stage_a(slot, c)
            if 0 <= slot - 1 < n:
                stage_b(slot - 1, c)
            if 0 <= slot - 2 < n:
                stage_c(slot - 2, c)


def _attn(lam, q, kt_ctx, kt_lat, v_ctx, v_lat, subln, *, tq, out_scale):
    b, l, w = q.shape
    lc = v_ctx.shape[1]
    lk = lc + l
    assert l % ATTN_KEY_CHUNK == 0 and tq % ATTN_ITEM_ROWS == 0
    tr = ATTN_ITEM_ROWS
    return pl.pallas_call(
        functools.partial(_attn_kernel, out_scale=out_scale),
        scratch_shapes=[pltpu.VMEM((2 * tr, lk), F32), pltpu.VMEM((2 * tr, lk), F32),
                        pltpu.VMEM((2 * tr, lk), BF16), pltpu.VMEM((2 * tr, lk), BF16)],
        grid=(b, l // tq),
        in_specs=[pl.BlockSpec(memory_space=pltpu.SMEM),
                  pl.BlockSpec((1, tq, w), lambda i, j: (i, j, 0)),
                  pl.BlockSpec((1, DA_HEADS, DA_V_DIM, lc), lambda i, j: (i, 0, 0, 0)),
                  pl.BlockSpec((1, DA_HEADS, DA_V_DIM, l), lambda i, j: (i, 0, 0, 0)),
                  pl.BlockSpec((1, lc, w), lambda i, j: (i, 0, 0)),
                  pl.BlockSpec((1, l, w), lambda i, j: (i, 0, 0)),
                  pl.BlockSpec((1, DA_V_DIM), lambda i, j: (0, 0))],
        out_specs=pl.BlockSpec((1, tq, w), lambda i, j: (i, j, 0)),
        out_shape=jax.ShapeDtypeStruct((b, l, w), BF16),
        compiler_params=_cparams(("parallel", "parallel")),
        name="attn",
    )(lam, q, kt_ctx, kt_lat, v_ctx, v_lat, subln)


def _gelu_tanh(x):
    return 0.5 * x * (1.0 + jnp.tanh(math.sqrt(2.0 / math.pi) * (x + 0.044715 * (x * x * x))))


def _merge_kernel(x_ref, ys_ref, yb_ref, ga_ref, gb_ref, gm_ref, shf_ref, scf_ref,
                  wglu_ref, wpa_ref, wpb_ref, wout_ref, npost_ref, npre_ref, wr2_ref,
                  x1_ref, h2_ref, aff_ref, afft_ref, ys_scr):
    tm, d = x_ref.shape[1:]
    dk = d // LANES
    nr = tm // 2
    state = [dict(), dict()]

    def rows(hf):
        return slice(hf * nr, (hf + 1) * nr)

    def st_relayout(hf):
        _chunks_to_tokens(ys_ref, ys_scr, hf * nr, nr)

    def st_glu(hf):
        st = state[hf]
        st['ya'] = _gelu_tanh(jnp.concatenate([ys_scr[gt, rows(hf), :] for gt in range(ys_scr.shape[0])], axis=1))
        st['glu'] = jnp.dot(st['ya'].astype(BF16), wglu_ref[...], preferred_element_type=F32)

    def st_proj(hf):
        st = state[hf]
        ya = st['ya'] * _sigmoid(st['glu'])
        st['pa'] = jnp.dot(ya.astype(BF16), wpa_ref[...], preferred_element_type=F32)
        st['pb'] = jnp.dot(yb_ref[0, rows(hf), :], wpb_ref[...], preferred_element_type=F32)

    def st_out(hf):
        st = state[hf]
        mix = ga_ref[0, rows(hf), :].astype(F32) * st['pa'] + gb_ref[0, rows(hf), :].astype(F32) * st['pb']
        st['o'] = jnp.dot(mix.astype(BF16), wout_ref[...], preferred_element_type=F32)

    def st_residual(hf):
        st = state[hf]
        x1 = x_ref[0, rows(hf), :] + gm_ref[0] * (_rms(st['o']) * npost_ref[...])
        x1_ref[0, rows(hf), :] = x1
        h = _rms(x1) * npre_ref[...]
        h = h * (1.0 + scf_ref[0]) + shf_ref[0]
        for k in range(dk):
            h2_ref[0, pl.ds(hf * nr * dk + k, nr, stride=dk), :] = h[:, k * LANES:(k + 1) * LANES]
        st['h'] = h

    def st_router(hf):
        st = state[hf]
        h = st['h']
        h_hi = h.astype(BF16)
        h_lo = (h - h_hi.astype(F32)).astype(BF16)
        st['lg'] = jnp.dot(jnp.concatenate([h_hi, h_lo], axis=1), wr2_ref[...], preferred_element_type=F32)

    def st_softmax(hf):
        lg = state[hf]['lg']
        logits = lg[:, :LANES] + lg[:, LANES:]
        lane = lax.broadcasted_iota(jnp.int32, logits.shape, 1)
        logits = jnp.where(lane < N_EXPERTS, logits, -1e30)
        ex = jnp.exp(logits - jnp.max(logits, axis=-1, keepdims=True))
        aff = ex / jnp.sum(ex, axis=-1, keepdims=True)
        aff_ref[0, rows(hf), :] = aff
        afft_ref[0, :, rows(hf)] = aff.T[:N_EXPERTS, :]

    stages = (st_relayout, st_glu, st_proj, st_out, st_residual, st_router, st_softmax)
    for k in range(len(stages) + 1):
        if k < len(stages):
            stages[k](0)
        if k >= 1:
            stages[k - 1](1)


def _merge(x, ys_rows, yb, ga, gb, gm, shf, scf, wglu, wpa, wpb, wout, npost, npre, wr2, *, tm):
    b, l, d = x.shape
    dk = d // LANES
    groups, _, ck = ys_rows.shape
    nj = l // tm

    def tok(n):
        return pl.BlockSpec((1, tm, n), lambda i, j: (i, j, 0))

    def full(a):
        return pl.BlockSpec(a.shape, lambda i, j: (0,) * a.ndim)

    mod = pl.BlockSpec((1, 1, d), lambda i, j: (i, 0, 0))
    return pl.pallas_call(
        _merge_kernel,
        grid=(b, nj),
        in_specs=[tok(d),
                  pl.BlockSpec((groups, tm // S5_CHUNK, ck), lambda i, j: (0, i * nj + j, 0)),
                  tok(yb.shape[2]), tok(d), tok(d), mod, mod, mod,
                  full(wglu), full(wpa), full(wpb), full(wout), full(npost), full(npre), full(wr2)],
        out_specs=[tok(d),
                   pl.BlockSpec((1, tm * dk, LANES), lambda i, j: (i, j, 0)),
                   tok(LANES),
                   pl.BlockSpec((1, N_EXPERTS, tm), lambda i, j: (i, 0, j))],
        out_shape=[jax.ShapeDtypeStruct((b, l, d), F32),
                   jax.ShapeDtypeStruct((b, l * dk, LANES), F32),
                   jax.ShapeDtypeStruct((b, l, LANES), F32),
                   jax.ShapeDtypeStruct((b, N_EXPERTS, l), F32)],
        scratch_shapes=[pltpu.VMEM((groups * S5_GROUP // LANES, tm, LANES), F32)],
        compiler_params=_cparams(("parallel", "parallel")),
        name="merge",
    )(x, ys_rows, yb, ga, gb, gm, shf, scf, wglu, wpa, wpb, wout, npost, npre, wr2)


def _route_kernel(a_ref, tok_ref, tri_ref, idx_ref, pos_scr, *, cap):
    a = a_ref[0]
    ne, l = a.shape
    capf = float(cap)

    def count_ge(thr):
        return jnp.sum(jnp.where(a >= thr, 1.0, 0.0), axis=1, keepdims=True)

    def narrow(mid, lo, hi):
        mid = jnp.minimum(jnp.maximum(mid, lo), hi)
        ok = count_ge(mid) >= capf
        return jnp.where(ok, mid, lo), jnp.where(ok, hi, mid)

    def geo(_, lohi):
        lo, hi = lohi
        return narrow(jnp.sqrt(jnp.maximum(lo, F32_TINY) * hi), lo, hi)

    def ari(_, lohi):
        lo, hi = lohi
        return narrow(0.5 * lo + 0.5 * hi, lo, hi)

    lohi = (jnp.zeros((ne, 1), F32), jnp.full((ne, 1), 2.0, F32))
    lohi = lax.fori_loop(0, 34, geo, lohi)
    lo, hi = lax.fori_loop(0, 8, ari, lohi)
    gt = a >= hi
    eq = (a >= lo) & jnp.logical_not(gt)
    need = capf - jnp.sum(jnp.where(gt, 1.0, 0.0), axis=1, keepdims=True)

    tri = tri_ref[...]

    def excl_cumsum(mask):
        mb = jnp.where(mask, 1.0, 0.0).astype(BF16)
        off = jnp.zeros((ne, 1), F32)
        outs = []
        for j in range(l // LANES):
            blk = mb[:, j * LANES:(j + 1) * LANES]
            outs.append(jnp.dot(blk, tri, preferred_element_type=F32) + off)
            off = off + jnp.sum(blk.astype(F32), axis=1, keepdims=True)
        return jnp.concatenate(outs, axis=1)

    sel = gt | (eq & (excl_cumsum(eq) < need))
    pos_scr[...] = jnp.where(sel, excl_cumsum(sel), -1.0)

    tc = 512
    slot = lax.broadcasted_iota(jnp.int32, (cap, tc), 0).astype(F32)

    def per_expert(e, _):
        acc = jnp.zeros((cap, LANES), F32)
        for c in range(l // tc):
            pc = pos_scr[pl.ds(e, 1), c * tc:(c + 1) * tc]
            onehot = jnp.where(pc == slot, 1.0, 0.0).astype(BF16)
            acc = acc + jnp.dot(onehot, tok_ref[c * tc:(c + 1) * tc, :], preferred_element_type=F32)
        acc_t = acc.T
        idx = acc_t[0:1, :] * 64.0 + acc_t[1:2, :]
        idx_ref[0, pl.ds(e, 1), :] = idx.astype(jnp.int32)
        return 0

    lax.fori_loop(0, ne, per_expert, 0)


def _route(aff_t, *, cap):
    b, ne, l = aff_t.shape
    tok_np = np.zeros((l, LANES), np.float32)
    tok_np[:, 0] = np.arange(l) >> 6
    tok_np[:, 1] = np.arange(l) & 63
    tok = jnp.asarray(tok_np, dtype=BF16)
    tri = jnp.asarray(np.arange(LANES)[:, None] < np.arange(LANES)[None, :], dtype=BF16)
    return pl.pallas_call(
        functools.partial(_route_kernel, cap=cap),
        grid=(b,),
        in_specs=[pl.BlockSpec((1, ne, l), lambda i: (i, 0, 0)),
                  pl.BlockSpec((l, LANES), lambda i: (0, 0)),
                  pl.BlockSpec((LANES, LANES), lambda i: (0, 0))],
        out_specs=pl.BlockSpec((1, ne, cap), lambda i: (i, 0, 0)),
        out_shape=jax.ShapeDtypeStruct((b, ne, cap), jnp.int32),
        scratch_shapes=[pltpu.VMEM((ne, l), F32)],
        compiler_params=_cparams(("parallel",)),
        name="route",
    )(aff_t, tok, tri)


ROW_UNROLL = 16


def _gather_kernel(idx_ref, h2_ref, aff_ref, xs_ref, g_ref, *, cap, dk):
    def gather(s, _):
        t = idx_ref[0, 0, s]
        xs_ref[0, 0, pl.ds(pl.multiple_of(s * dk, dk), dk), :] = h2_ref[0, pl.ds(pl.multiple_of(t * dk, dk), dk), :]
        g_ref[0, 0, pl.ds(s, 1), :] = aff_ref[0, pl.ds(t, 1), :]
        return 0

    lax.fori_loop(0, cap, gather, 0, unroll=ROW_UNROLL)


def _moe_gather(idx, h2, aff, *, d):
    b, ne, cap = idx.shape
    l = aff.shape[1]
    dk = d // LANES
    return pl.pallas_call(
        functools.partial(_gather_kernel, cap=cap, dk=dk),
        grid=(b, ne),
        in_specs=[pl.BlockSpec((1, 1, cap), lambda i, e: (i * ne + e, 0, 0), memory_space=pltpu.SMEM),
                  pl.BlockSpec((1, l * dk, LANES), lambda i, e: (i, 0, 0), pipeline_mode=pl.Buffered(1)),
                  pl.BlockSpec((1, l, LANES), lambda i, e: (i, 0, 0))],
        out_specs=[pl.BlockSpec((1, 1, cap * dk, LANES), lambda i, e: (i, e, 0, 0)),
                   pl.BlockSpec((1, 1, cap, LANES), lambda i, e: (i, e, 0, 0))],
        out_shape=[jax.ShapeDtypeStruct((b, ne, cap * dk, LANES), F32),
                   jax.ShapeDtypeStruct((b, ne, cap, LANES), F32)],
        compiler_params=_cparams(("arbitrary", "arbitrary")),
        name="moe_gather",
    )(idx.reshape(b * ne, 1, cap), h2, aff)


def _expert_kernel(xs_ref, g_ref, wg_ref, wu_ref, wd_ref, ys_ref, *, nsplit):
    e = pl.program_id(0)
    cap = g_ref.shape[2]
    dk = xs_ref.shape[2] // cap
    xs = jnp.concatenate([xs_ref[0, 0, pl.ds(k, cap, stride=dk), :] for k in range(dk)], axis=1).astype(BF16)
    tf = wg_ref.shape[2] // nsplit
    y = None
    for j in range(nsplit):
        cols = slice(j * tf, (j + 1) * tf)
        gg = jnp.dot(xs, wg_ref[0, :, cols], preferred_element_type=F32)
        uu = jnp.dot(xs, wu_ref[0, :, cols], preferred_element_type=F32)
        hid = (gg * _sigmoid(gg) * uu).astype(BF16)
        part = jnp.dot(hid, wd_ref[0, cols, :], preferred_element_type=F32)
        y = part if y is None else y + part
    g = g_ref[0, 0]
    lane = lax.broadcasted_iota(jnp.int32, g.shape, 1)
    gate = jnp.sum(jnp.where(lane == e, g, 0.0), axis=1, keepdims=True)
    yg = y * gate
    for k in range(dk):
        ys_ref[0, 0, pl.ds(k, cap, stride=dk), :] = yg[:, k * LANES:(k + 1) * LANES]


def _moe_experts(xs, gates, wg, wu, wd):
    b, ne, rows, _ = xs.shape
    cap = gates.shape[2]
    d, fdim = wg.shape[1:]
    tok = pl.BlockSpec((1, 1, rows, LANES), lambda e, i: (i, e, 0, 0))
    return pl.pallas_call(
        functools.partial(_expert_kernel, nsplit=2),
        grid=(ne, b),
        in_specs=[tok,
                  pl.BlockSpec((1, 1, cap, LANES), lambda e, i: (i, e, 0, 0)),
                  pl.BlockSpec((1, d, fdim), lambda e, i: (e, 0, 0)),
                  pl.BlockSpec((1, d, fdim), lambda e, i: (e, 0, 0)),
                  pl.BlockSpec((1, fdim, d), lambda e, i: (e, 0, 0))],
        out_specs=tok,
        out_shape=jax.ShapeDtypeStruct((b, ne, rows, LANES), F32),
        compiler_params=_cparams(("arbitrary", "arbitrary")),
        name="moe_experts",
    )(xs, gates, wg, wu, wd)


def _scatter_kernel(idx_ref, ys_ref, x1_ref, gf_ref, nw_ref, o_ref, f_scr, *, cap, ne):
    e = pl.program_id(1)
    dk = ys_ref.shape[2] // cap
    tm = x1_ref.shape[1]

    @pl.when(e == 0)
    def _():
        f_scr[...] = jnp.zeros_like(f_scr)

    @pl.when(e < ne)
    def _():
        def scatter(s0, _):
            rows = []
            vals = []
            for i in range(ROW_UNROLL):
                s = s0 * ROW_UNROLL + i
                r = pl.multiple_of(idx_ref[0, 0, s] * dk, dk)
                rows.append(r)
                vals.append(f_scr[pl.ds(r, dk), :] + ys_ref[0, 0, pl.ds(pl.multiple_of(s * dk, dk), dk), :])
            for r, v in zip(rows, vals):
                f_scr[pl.ds(r, dk), :] = v
            return 0

        lax.fori_loop(0, cap // ROW_UNROLL, scatter, 0)

    @pl.when(e >= ne)
    def _():
        base = (e - ne) * (tm * dk)
        f = jnp.concatenate([f_scr[pl.ds(base + k, tm, stride=dk), :] for k in range(dk)], axis=1)
        o_ref[0] = x1_ref[0] + gf_ref[0] * (_rms(f) * nw_ref[...])


def _moe_scatter_final(idx, ys, x1, gf, nw, *, tm):
    b, ne, cap = idx.shape
    l, d = x1.shape[1:]
    dk = d // LANES
    nt = l // tm

    def expert(e):
        return jnp.minimum(e, ne - 1)

    def tile(e):
        return jnp.maximum(e - ne, 0)

    return pl.pallas_call(
        functools.partial(_scatter_kernel, cap=cap, ne=ne),
        grid=(b, ne + nt),
        in_specs=[pl.BlockSpec((1, 1, cap), lambda i, e: (i * ne + expert(e), 0, 0), memory_space=pltpu.SMEM),
                  pl.BlockSpec((1, 1, cap * dk, LANES), lambda i, e: (i, expert(e), 0, 0)),
                  pl.BlockSpec((1, tm, d), lambda i, e: (i, tile(e), 0)),
                  pl.BlockSpec((1, 1, d), lambda i, e: (i, 0, 0)),
                  pl.BlockSpec((1, d), lambda i, e: (0, 0))],
        out_specs=pl.BlockSpec((1, tm, d), lambda i, e: (i, tile(e), 0)),
        out_shape=jax.ShapeDtypeStruct((b, l, d), F32),
        scratch_shapes=[pltpu.VMEM((l * dk, LANES), F32)],
        compiler_params=_cparams(("arbitrary", "arbitrary")),
        name="moe_scatter_final",
    )(idx.reshape(b * ne, 1, cap), ys, x1, gf, nw)


def kernel(x, c, ctx, c_ctx, w_ada, b_ada, norm_pre_mix, norm_post_mix, norm_pre_ffn, norm_post_ffn, w_in, s5_lam_re, s5_lam_im, s5_log_dt, s5_b_re, s5_b_im, s5_c_re, s5_c_im, s5_d, w_glu, da_lambda, da_subln, w_proj_a, w_proj_b, w_out, w_router, w_exp_gate, w_exp_up, w_exp_down):
    depth = w_ada.shape[0]
    assert depth == 1, "single trunk layer: the context stream's outputs are never consumed"
    b, l, d = x.shape
    lc = ctx.shape[1]
    assert b == SUBLANES and l % (S5_CHUNK * 32) == 0 and lc % (S5_CHUNK * 16) == 0
    s5w = s5_d.shape[1]
    qkw = DA_HEADS * 2 * DA_HEAD_DIM
    vw = DA_HEADS * DA_V_DIM
    widths = (s5w, qkw, vw, d)
    lam_init = 0.8 - 0.6 * math.exp(-0.3 * 0)

    c_all = jnp.zeros((2 * SUBLANES, d), F32).at[:b].set(c).at[b].set(c_ctx)
    mod = _ada(c_all, w_ada[0], b_ada[0])
    sh_m, sc_m, g_m, sh_f, sc_f, g_f = [mod[:b, i * d:(i + 1) * d].reshape(b, 1, d) for i in range(6)]
    csh_m = mod[b:b + 1, 0:d].reshape(1, 1, d)
    csc_m = mod[b:b + 1, d:2 * d].reshape(1, 1, d)

    w_in_b = w_in[0].astype(BF16)
    cos_t, sin_t = _rope_tables(l)
    npm = norm_pre_mix[0].reshape(1, d)
    u, q, k, v, ga, gb = _inproj(x, npm, sh_m, sc_m, w_in_b, cos_t, sin_t, latent=True, widths=widths, tm=1024)
    uc, kc, vc = _inproj(ctx, npm, csh_m, csc_m, w_in_b, cos_t[:lc], sin_t[:lc], latent=False, widths=widths, tm=lc)

    w_mat, m_mat, v_mat, a_mat = _s5_matrices(s5_lam_re[0], s5_lam_im[0], s5_log_dt[0], s5_b_re[0], s5_b_im[0],
                                              s5_c_re[0], s5_c_im[0], s5_d[0])
    ys_rows = _s5(uc, u, w_mat, m_mat, v_mat, a_mat, n_ctx=lc // S5_CHUNK, n_lat=l // S5_CHUNK)

    lq1, lk1, lq2, lk2 = da_lambda[0].astype(F32)
    lam = jnp.exp(jnp.sum(lq1 * lk1)) - jnp.exp(jnp.sum(lq2 * lk2)) + lam_init
    kt_ctx = kc.reshape(b, lc, DA_HEADS, DA_V_DIM).transpose(0, 2, 3, 1)
    kt_lat = k.reshape(b, l, DA_HEADS, DA_V_DIM).transpose(0, 2, 3, 1)
    yb = _attn(lam.reshape(1, 1), q, kt_ctx, kt_lat, vc, v, da_subln[0].reshape(1, DA_V_DIM),
               tq=256, out_scale=1.0 - lam_init)

    wr = jnp.zeros((d, LANES), F32).at[:, :N_EXPERTS].set(w_router[0])
    wr_hi = wr.astype(BF16)
    wr_lo = (wr - wr_hi.astype(F32)).astype(BF16)
    wr2 = jnp.concatenate([jnp.concatenate([wr_hi, wr_lo], axis=1),
                           jnp.concatenate([wr_hi, jnp.zeros_like(wr_hi)], axis=1)], axis=0)
    x1, h2, aff, aff_t = _merge(x, ys_rows, yb, ga, gb, g_m, sh_f, sc_f,
                                w_glu[0].astype(BF16), w_proj_a[0].astype(BF16), w_proj_b[0].astype(BF16),
                                w_out[0].astype(BF16), norm_post_mix[0].reshape(1, d), norm_pre_ffn[0].reshape(1, d),
                                wr2, tm=512)

    cap = EC_CAPACITY * l // N_EXPERTS
    idx = _route(aff_t, cap=cap)
    xs, gates = _moe_gather(idx, h2, aff, d=d)
    ys = _moe_experts(xs, gates, w_exp_gate[0].astype(BF16), w_exp_up[0].astype(BF16), w_exp_down[0].astype(BF16))
    return _moe_scatter_final(idx, ys, x1, g_f, norm_post_ffn[0].reshape(1, d), tm=512)
```

```python
import functools
import math

import numpy as np

import jax
import jax.numpy as jnp
from jax import lax
from jax.experimental import pallas as pl
from jax.experimental.pallas import tpu as pltpu

F32 = jnp.float32
BF16 = jnp.bfloat16

EPS = 1e-6
GRID_W = 64
ROPE_THETA = 10000.0
S5_GROUP = 16
S5_STATE = 64
S5_DT_MAX_RE = -1e-4
S5_CHUNK = 16
DA_HEADS = 4
DA_HEAD_DIM = 64
DA_V_DIM = 128
N_EXPERTS = 16
EC_CAPACITY = 2
LANES = 128
SUBLANES = 8
VMEM_LIMIT = 60 * 1024 * 1024
LOG2E = 1.4426950408889634
F32_TINY = 1e-37


def _cparams(sem):
    return pltpu.CompilerParams(dimension_semantics=sem, vmem_limit_bytes=VMEM_LIMIT)


def _rms(x, eps=EPS):
    return x * lax.rsqrt(jnp.mean(x * x, axis=-1, keepdims=True) + eps)


def _sigmoid(x):
    return 1.0 / (1.0 + jnp.exp(-x))


def _ada_kernel(c_ref, w_ref, b_ref, o_ref):
    c = c_ref[...]
    s = (c * _sigmoid(c)).astype(BF16)
    o_ref[...] = jnp.dot(s, w_ref[...].astype(BF16), preferred_element_type=F32) + b_ref[...]


def _ada(c_all, w_ada, b_ada):
    rows, d = c_all.shape
    n = w_ada.shape[1]
    tn = 512
    return pl.pallas_call(
        _ada_kernel,
        grid=(n // tn,),
        in_specs=[pl.BlockSpec((rows, d), lambda j: (0, 0)),
                  pl.BlockSpec((d, tn), lambda j: (0, j)),
                  pl.BlockSpec((1, tn), lambda j: (0, j))],
        out_specs=pl.BlockSpec((rows, tn), lambda j: (0, j)),
        out_shape=jax.ShapeDtypeStruct((rows, n), F32),
        compiler_params=_cparams(("arbitrary",)),
        name="ada",
    )(c_all, w_ada, b_ada.reshape(1, n))


def _block_transpose8(vs):
    lane_blk = lax.broadcasted_iota(jnp.int32, vs[0].shape, 1) // S5_GROUP
    vs = list(vs)
    for s in (4, 2, 1):
        upper = (lane_blk & s) != 0
        for i in range(8):
            if i & s:
                continue
            a, b = vs[i], vs[i + s]
            vs[i] = jnp.where(upper, pltpu.roll(b, S5_GROUP * s, 1), a)
            vs[i + s] = jnp.where(upper, b, pltpu.roll(a, LANES - S5_GROUP * s, 1))
    return vs


def _tokens_to_chunks(scr, out_ref, row0, nrows):
    nc = nrows // S5_CHUNK
    c0 = row0 // S5_CHUNK
    for gt in range(scr.shape[0]):
        for j in range(S5_CHUNK // 8):
            vs = [scr[gt, pl.ds(row0 + 8 * j + i, nc, stride=S5_CHUNK), :] for i in range(8)]
            vs = _block_transpose8(vs)
            for gi in range(8):
                out_ref[gt * 8 + gi, c0:c0 + nc, j * LANES:(j + 1) * LANES] = vs[gi].astype(out_ref.dtype)


def _chunks_to_tokens(in_ref, scr, row0, nrows):
    nc = nrows // S5_CHUNK
    c0 = row0 // S5_CHUNK
    for gt in range(scr.shape[0]):
        for j in range(S5_CHUNK // 8):
            vs = [in_ref[gt * 8 + gi, c0:c0 + nc, j * LANES:(j + 1) * LANES].astype(F32) for gi in range(8)]
            vs = _block_transpose8(vs)
            for i in range(8):
                scr[gt, pl.ds(row0 + 8 * j + i, nc, stride=S5_CHUNK), :] = vs[i]


def _swap16(x):
    lane = lax.broadcasted_iota(jnp.int32, x.shape, 1)
    return jnp.where((lane & 16) == 0, pltpu.roll(x, LANES - 16, 1), pltpu.roll(x, 16, 1))


def _inproj_kernel(x_ref, nw_ref, sh_ref, sc_ref, w_ref, cos_ref, sin_ref, *refs, latent, widths):
    tm = x_ref.shape[1]
    nparts = 2 if tm % (2 * S5_CHUNK * 16) == 0 else 1
    nr = tm // nparts
    s5w, qkw, vw, dm = widths
    o_u, o_q, o_k, o_v, o_ga = 0, s5w, s5w + qkw, s5w + 2 * qkw, s5w + 2 * qkw + vw
    if latent:
        u_ref, q_ref, k_ref, v_ref, ga_ref, gb_ref, u_scr = refs
    else:
        u_ref, k_ref, v_ref, u_scr = refs
    hs = [None] * nparts

    def rows(part):
        return slice(part * nr, (part + 1) * nr)

    def proj(part, lo, n):
        return jnp.dot(hs[part], w_ref[:, lo:lo + n], preferred_element_type=F32)

    def rope(part, z, scale):
        cos = cos_ref[rows(part), :]
        sin = sin_ref[rows(part), :]
        pieces = []
        for j in range(z.shape[1] // LANES):
            zj = z[:, j * LANES:(j + 1) * LANES]
            pieces.append((zj * cos + _swap16(zj) * sin) * scale)
        return jnp.concatenate(pieces, axis=1)

    def st_norm(part):
        h = _rms(x_ref[0, rows(part), :]) * nw_ref[...]
        hs[part] = (h * (1.0 + sc_ref[0]) + sh_ref[0]).astype(BF16)

    def st_q(part):
        q_ref[0, rows(part), :] = rope(part, proj(part, o_q, qkw), DA_HEAD_DIM ** -0.5 * LOG2E).astype(BF16)

    def st_k(part):
        k = proj(part, o_k, qkw)
        k_ref[0, rows(part), :] = (rope(part, k, 1.0) if latent else k).astype(BF16)

    def st_v(part):
        v_ref[0, rows(part), :] = proj(part, o_v, vw).astype(BF16)

    def st_ga(part):
        ga_ref[0, rows(part), :] = _sigmoid(proj(part, o_ga, dm)).astype(BF16)

    def st_gb(part):
        gb_ref[0, rows(part), :] = _sigmoid(proj(part, o_ga + dm, dm)).astype(BF16)

    def st_u(part):
        u = proj(part, o_u, s5w)
        for gt in range(s5w // LANES):
            u_scr[gt, rows(part), :] = u[:, gt * LANES:(gt + 1) * LANES]
        _tokens_to_chunks(u_scr, u_ref, part * nr, nr)

    stages = (st_norm, st_q, st_k, st_v, st_ga, st_gb, st_u) if latent else (st_norm, st_k, st_v, st_u)
    for step in range(len(stages) + nparts - 1):
        for part in range(nparts):
            if 0 <= step - part < len(stages):
                stages[step - part](part)


def _inproj(x, nw, sh, sc, w_in, cos_t, sin_t, *, latent, widths, tm):
    b, l, d = x.shape
    s5w, qkw, vw, dm = widths
    groups = s5w // S5_GROUP
    nc = tm // S5_CHUNK
    per_b = sh.shape[0] > 1
    mod_spec = pl.BlockSpec((1, 1, d), (lambda i, j: (i, 0, 0)) if per_b else (lambda i, j: (0, 0, 0)))

    def tok_spec(n):
        return pl.BlockSpec((1, tm, n), lambda i, j: (i, j, 0))

    nj = l // tm
    u_spec = pl.BlockSpec((groups, nc, S5_CHUNK * S5_GROUP), lambda i, j: (0, i * nj + j, 0))
    u_shape = jax.ShapeDtypeStruct((groups, b * (l // S5_CHUNK), S5_CHUNK * S5_GROUP), BF16)
    out_w = (qkw, qkw, vw, dm, dm) if latent else (qkw, vw)
    return pl.pallas_call(
        functools.partial(_inproj_kernel, latent=latent, widths=widths),
        grid=(b, nj),
        in_specs=[tok_spec(d),
                  pl.BlockSpec((1, d), lambda i, j: (0, 0)),
                  mod_spec, mod_spec,
                  pl.BlockSpec(w_in.shape, lambda i, j: (0, 0)),
                  pl.BlockSpec((tm, LANES), lambda i, j: (j, 0)),
                  pl.BlockSpec((tm, LANES), lambda i, j: (j, 0))],
        out_specs=[u_spec] + [tok_spec(n) for n in out_w],
        out_shape=[u_shape] + [jax.ShapeDtypeStruct((b, l, n), BF16) for n in out_w],
        scratch_shapes=[pltpu.VMEM((s5w // LANES, tm, LANES), F32)],
        compiler_params=_cparams(("parallel", "parallel")),
        name="inproj_lat" if latent else "inproj_ctx",
    )(x, nw, sh, sc, w_in, cos_t, sin_t)


def _rope_tables(seq_len):
    rows = seq_len // GRID_W
    pairs = DA_HEAD_DIM // 4
    row = np.repeat(np.arange(rows), GRID_W).astype(np.float64)
    col = np.tile(np.arange(GRID_W), rows).astype(np.float64)
    inv_freq = ROPE_THETA ** (-np.arange(pairs, dtype=np.float64) / pairs)
    ra = row[:, None] * inv_freq[None, :]
    ca = col[:, None] * inv_freq[None, :]
    cos64 = np.concatenate([np.cos(ra), np.cos(ra), np.cos(ca), np.cos(ca)], axis=1)
    sin64 = np.concatenate([-np.sin(ra), np.sin(ra), -np.sin(ca), np.sin(ca)], axis=1)
    return (jnp.asarray(np.tile(cos64, (1, 2)), dtype=F32), jnp.asarray(np.tile(sin64, (1, 2)), dtype=F32))


def _s5_matrices(lam_re, lam_im, log_dt, b_re, b_im, c_re, c_im, d_skip):
    hp = lax.Precision.HIGHEST
    t = S5_CHUNK
    lam_re = jnp.minimum(lam_re.astype(F32), S5_DT_MAX_RE)
    lam_im = lam_im.astype(F32)
    dt = jnp.exp(log_dt.astype(F32))[..., None]
    g, p = lam_re.shape[1:]
    hh = S5_GROUP
    mag = jnp.exp(lam_re * dt)
    lb_re = mag * jnp.cos(lam_im * dt)
    lb_im = mag * jnp.sin(lam_im * dt)
    den = lam_re * lam_re + lam_im * lam_im
    num_re = lb_re - 1.0
    co_re = (num_re * lam_re + lb_im * lam_im) / den
    co_im = (lb_im * lam_re - num_re * lam_im) / den
    br = b_re.astype(F32)
    bi = b_im.astype(F32)
    bb_re = co_re[..., None] * br - co_im[..., None] * bi
    bb_im = co_re[..., None] * bi + co_im[..., None] * br
    j = jnp.arange(t + 1, dtype=F32)[None, None, :, None]
    pmag = jnp.exp(lam_re[:, :, None, :] * dt[:, :, None, :] * j)
    pang = lam_im[:, :, None, :] * dt[:, :, None, :] * j
    pw_re = pmag * jnp.cos(pang)
    pw_im = pmag * jnp.sin(pang)
    cr = c_re.astype(F32)
    ci = c_im.astype(F32)
    cp_re = cr[:, :, None] * pw_re[:, :, :t, None, :] - ci[:, :, None] * pw_im[:, :, :t, None, :]
    cp_im = cr[:, :, None] * pw_im[:, :, :t, None, :] + ci[:, :, None] * pw_re[:, :, :t, None, :]
    taps = (jnp.einsum('dgjhp,dgpi->dgjhi', cp_re, bb_re, precision=hp)
            - jnp.einsum('dgjhp,dgpi->dgjhi', cp_im, bb_im, precision=hp))
    sig = np.arange(t)[:, None, None]
    tau = np.arange(t)[None, :, None]
    lag = np.arange(t)[None, None, :]
    place = np.concatenate([tau - sig == lag, sig - tau == lag], axis=2).astype(np.float32)
    taps2 = jnp.concatenate([taps[0], taps[1]], axis=1)
    skip = d_skip.astype(F32).reshape(g, hh)
    diag = (np.eye(t, dtype=np.float32)[None, :, None, :, None]
            * (jnp.eye(hh, dtype=F32)[None] * skip[:, None, :])[:, None, :, None, :])
    m_mat = (jnp.einsum('stj,gjab->gsbta', place, taps2, precision=hp) + diag).reshape(g, t * hh, t * hh)
    pf_re = pw_re[0][:, :t][:, ::-1]
    pf_im = pw_im[0][:, :t][:, ::-1]
    pr_re = pw_re[1][:, :t]
    pr_im = pw_im[1][:, :t]

    def state_in(pr_, pi_, br_, bi_):
        brt = br_.transpose(0, 2, 1)[:, None]
        bit = bi_.transpose(0, 2, 1)[:, None]
        re = pr_[:, :, None, :] * brt - pi_[:, :, None, :] * bit
        im = pr_[:, :, None, :] * bit + pi_[:, :, None, :] * brt
        return re.reshape(g, t * hh, p), im.reshape(g, t * hh, p)

    wf_re, wf_im = state_in(pf_re, pf_im, bb_re[0], bb_im[0])
    wr_re, wr_im = state_in(pr_re, pr_im, bb_re[1], bb_im[1])
    w_mat = jnp.concatenate([wf_re, wr_re, wf_im, wr_im], axis=2)
    ef_re = pw_re[0][:, 1:t + 1]
    ef_im = pw_im[0][:, 1:t + 1]
    er_re = pw_re[1][:, 1:t + 1][:, ::-1]
    er_im = pw_im[1][:, 1:t + 1][:, ::-1]

    def state_out(e_re, e_im, cr_, ci_):
        g_re = cr_[:, None] * e_re[:, :, None, :] - ci_[:, None] * e_im[:, :, None, :]
        g_im = cr_[:, None] * e_im[:, :, None, :] + ci_[:, None] * e_re[:, :, None, :]
        return (g_re.transpose(0, 3, 1, 2).reshape(g, p, t * hh),
                (-g_im).transpose(0, 3, 1, 2).reshape(g, p, t * hh))

    vf_re, vf_im = state_out(ef_re, ef_im, cr[0], ci[0])
    vr_re, vr_im = state_out(er_re, er_im, cr[1], ci[1])
    v_mat = jnp.concatenate([vf_re, vr_re, vf_im, vr_im], axis=1)
    a_mat = jnp.stack([jnp.concatenate([pw_re[0][:, t], pw_re[1][:, t]], axis=1),
                       jnp.concatenate([pw_im[0][:, t], pw_im[1][:, t]], axis=1)], axis=1)
    return w_mat.astype(BF16), m_mat.astype(BF16), v_mat.astype(BF16), a_mat


def _s5_kernel(uc_ref, ul_ref, w_ref, m_ref, v_ref, a_ref, y_ref, sc_scr, sl_scr, hf_scr, hr_scr, *, n_ctx, n_lat):
    nb = SUBLANES
    half = S5_STATE
    ul = ul_ref[0]
    w = w_ref[0]
    s_c = jnp.dot(uc_ref[0], w, preferred_element_type=F32)
    s_l = jnp.dot(ul, w, preferred_element_type=F32)
    for part in range(2):
        for b in range(nb):
            sc_scr[part, pl.ds(b, n_ctx, stride=nb), :] = s_c[b * n_ctx:(b + 1) * n_ctx, part * LANES:(part + 1) * LANES]
            sl_scr[part, pl.ds(b, n_lat, stride=nb), :] = s_l[b * n_lat:(b + 1) * n_lat, part * LANES:(part + 1) * LANES]
    a = a_ref[0]
    a_re = a[0:1, :]
    a_im = a[1:2, :]
    is_fwd = lax.broadcasted_iota(jnp.int32, (nb, LANES), 1) < half

    def tile(j):
        return pl.ds(pl.multiple_of(j * nb, nb), nb)

    def pick(scr, jf, jr):
        return (jnp.where(is_fwd, scr[0, tile(jf), :], scr[0, tile(jr), :]),
                jnp.where(is_fwd, scr[1, tile(jf), :], scr[1, tile(jr), :]))

    def update(h_re, h_im, s_re, s_im):
        return a_re * h_re - a_im * h_im + s_re, a_re * h_im + a_im * h_re + s_im

    def ctx_step(i, carry):
        return update(*carry, *pick(sc_scr, i, n_ctx - 1 - i))

    def lat_step(c, carry):
        h_re, h_im = carry
        cr = n_lat - 1 - c
        hf_scr[0, tile(c), :] = h_re
        hf_scr[1, tile(c), :] = h_im
        hr_scr[0, tile(cr), :] = h_re
        hr_scr[1, tile(cr), :] = h_im
        return update(h_re, h_im, *pick(sl_scr, c, cr))

    zero = jnp.zeros((nb, LANES), F32)
    carry = lax.fori_loop(0, n_ctx, ctx_step, (zero, zero))
    lax.fori_loop(0, n_lat, lat_step, carry, unroll=2)
    lane = lax.broadcasted_iota(jnp.int32, (n_lat, LANES), 1)

    def entry_states(b):
        rows = pl.ds(b, n_lat, stride=nb)
        return jnp.concatenate([jnp.where(lane < half, hf_scr[part, rows, :], hr_scr[part, rows, :])
                                for part in range(2)], axis=1)

    hin = jnp.concatenate([entry_states(b) for b in range(nb)], axis=0).astype(BF16)
    y = jnp.dot(ul, m_ref[0], preferred_element_type=F32)
    y = y + jnp.dot(hin, v_ref[0], preferred_element_type=F32)
    y_ref[0] = y.astype(BF16)


def _s5(u_ctx, u_lat, w_mat, m_mat, v_mat, a_mat, *, n_ctx, n_lat):
    g, rc, k = u_ctx.shape
    rl = u_lat.shape[1]
    mat_spec = pl.BlockSpec((1, k, k), lambda i: (i, 0, 0))
    return pl.pallas_call(
        functools.partial(_s5_kernel, n_ctx=n_ctx, n_lat=n_lat),
        grid=(g,),
        in_specs=[pl.BlockSpec((1, rc, k), lambda i: (i, 0, 0)),
                  pl.BlockSpec((1, rl, k), lambda i: (i, 0, 0)),
                  mat_spec, mat_spec, mat_spec,
                  pl.BlockSpec((1, 2, LANES), lambda i: (i, 0, 0))],
        out_specs=pl.BlockSpec((1, rl, k), lambda i: (i, 0, 0)),
        out_shape=jax.ShapeDtypeStruct((g, rl, k), BF16),
        scratch_shapes=[pltpu.VMEM((2, rc, LANES), F32), pltpu.VMEM((2, rl, LANES), F32),
                        pltpu.VMEM((2, rl, LANES), F32), pltpu.VMEM((2, rl, LANES), F32)],
        compiler_params=_cparams(("parallel",)),
        name="s5",
    )(u_ctx, u_lat, w_mat, m_mat, v_mat, a_mat)


ATTN_KEY_CHUNK = 256
ATTN_ITEM_ROWS = 128


def _attn_kernel(lam_ref, q_ref, kc_ref, kl_ref, vc_ref, vl_ref, sw_ref, o_ref, s_a, s_b, p_a, p_b, *, out_scale):
    lam = lam_ref[0, 0]
    sw = sw_ref[...]
    tr = ATTN_ITEM_ROWS
    lc = kc_ref.shape[1]
    kc = ATTN_KEY_CHUNK
    nch = 1 + kl_ref.shape[1] // kc

    def lat(c):
        return slice((c - 1) * kc, c * kc)

    def k_chunk(c, lanes):
        return kc_ref[0, :, lanes] if c == 0 else kl_ref[0, lat(c), lanes]

    def v_chunk(c, lanes):
        return vc_ref[0, :, lanes] if c == 0 else vl_ref[0, lat(c), lanes]

    def keys(c):
        return slice(0, lc) if c == 0 else slice(lc + (c - 1) * kc, lc + c * kc)

    def lane_tiles(x):
        return [x[:, j * LANES:(j + 1) * LANES] for j in range(x.shape[1] // LANES)]

    def widen(col, width):
        return {w: jnp.broadcast_to(col, (col.shape[0], w)) for w in {lc, width}}

    s_bufs = (s_a, s_b)
    p_bufs = (p_a, p_b)
    lane = lax.broadcasted_iota(jnp.int32, (tr, DA_V_DIM), 1)
    work = [(r, h) for r in range(q_ref.shape[1] // tr) for h in range(DA_HEADS)]
    items = [dict() for _ in work]

    def rows(i):
        return slice(work[i][0] * tr, (work[i][0] + 1) * tr)

    def cols(i):
        return slice(work[i][1] * DA_V_DIM, (work[i][1] + 1) * DA_V_DIM)

    def stage_a(i, c):
        it = items[i]
        if c == 0:
            q = q_ref[0, rows(i), cols(i)]
            zero = jnp.zeros_like(q)
            it['q2'] = jnp.concatenate([jnp.where(lane < DA_HEAD_DIM, q, zero),
                                        jnp.where(lane >= DA_HEAD_DIM, q, zero)], axis=0)
        s = lax.dot_general(it['q2'], k_chunk(c, cols(i)), (((1,), (1,)), ((), ())),
                            preferred_element_type=F32)
        s_bufs[i % 2][:, keys(c)] = s
        mx = functools.reduce(jnp.maximum, lane_tiles(s))
        it['mx'] = mx if c == 0 else jnp.maximum(it['mx'], mx)
        if c == nch - 1:
            it['m'] = widen(jnp.max(it['mx'], axis=-1, keepdims=True), kc)

    def stage_b(i, c):
        it = items[i]
        s = s_bufs[i % 2][:, keys(c)]
        p = jnp.exp2(s - it['m'][s.shape[1]])
        ps = functools.reduce(jnp.add, lane_tiles(p))
        it['ls'] = ps if c == 0 else it['ls'] + ps
        p_bufs[i % 2][:, keys(c)] = p.astype(BF16)
        if c == nch - 1:
            l = jnp.sum(it['ls'], axis=-1, keepdims=True)
            it['r1'] = widen((1.0 / l[:tr]).astype(BF16), kc)
            it['r2'] = widen((lam / l[tr:]).astype(BF16), kc)

    def stage_c(i, c):
        it = items[i]
        pb = p_bufs[i % 2]
        p1 = pb[0:tr, keys(c)]
        pd = p1 * it['r1'][p1.shape[1]] - pb[tr:2 * tr, keys(c)] * it['r2'][p1.shape[1]]
        part = jnp.dot(pd, v_chunk(c, cols(i)), preferred_element_type=F32)
        it['acc'] = part if c == 0 else it['acc'] + part
        if c == nch - 1:
            o_ref[0, rows(i), cols(i)] = (_rms(it['acc']) * sw * out_scale).astype(BF16)

    n = len(work)
    for slot in range(n + 2):
        for c in range(nch):
            if slot < n:
                stage_a(slot, c)
            if 0 <= slot - 1 < n:
                stage_b(slot - 1, c)
            if 0 <= slot - 2 < n:
                stage_c(slot - 2, c)


def _attn(lam, q, k_ctx, k_lat, v_ctx, v_lat, subln, *, tq, out_scale):
    b, l, w = q.shape
    lc = v_ctx.shape[1]
    lk = lc + l
    assert l % ATTN_KEY_CHUNK == 0 and tq % ATTN_ITEM_ROWS == 0
    tr = ATTN_ITEM_ROWS
    return pl.pallas_call(
        functools.partial(_attn_kernel, out_scale=out_scale),
        scratch_shapes=[pltpu.VMEM((2 * tr, lk), F32), pltpu.VMEM((2 * tr, lk), F32),
                        pltpu.VMEM((2 * tr, lk), BF16), pltpu.VMEM((2 * tr, lk), BF16)],
        grid=(b, l // tq),
        in_specs=[pl.BlockSpec(memory_space=pltpu.SMEM),
                  pl.BlockSpec((1, tq, w), lambda i, j: (i, j, 0)),
                  pl.BlockSpec((1, lc, w), lambda i, j: (i, 0, 0)),
                  pl.BlockSpec((1, l, w), lambda i, j: (i, 0, 0)),
                  pl.BlockSpec((1, lc, w), lambda i, j: (i, 0, 0)),
                  pl.BlockSpec((1, l, w), lambda i, j: (i, 0, 0)),
                  pl.BlockSpec((1, DA_V_DIM), lambda i, j: (0, 0))],
        out_specs=pl.BlockSpec((1, tq, w), lambda i, j: (i, j, 0)),
        out_shape=jax.ShapeDtypeStruct((b, l, w), BF16),
        compiler_params=_cparams(("parallel", "parallel")),
        name="attn",
    )(lam, q, k_ctx, k_lat, v_ctx, v_lat, subln)


def _gelu_tanh(x):
    return 0.5 * x * (1.0 + jnp.tanh(math.sqrt(2.0 / math.pi) * (x + 0.044715 * (x * x * x))))


def _merge_kernel(x_ref, ys_ref, yb_ref, ga_ref, gb_ref, gm_ref, shf_ref, scf_ref,
                  wglu_ref, wpa_ref, wpb_ref, wout_ref, npost_ref, npre_ref, wr2_ref,
                  x1_ref, h2_ref, aff_ref, afft_ref, ys_scr):
    tm, d = x_ref.shape[1:]
    dk = d // LANES
    nr = tm // 2
    state = [dict(), dict()]

    def rows(hf):
        return slice(hf * nr, (hf + 1) * nr)

    def st_relayout(hf):
        _chunks_to_tokens(ys_ref, ys_scr, hf * nr, nr)

    def st_glu(hf):
        st = state[hf]
        st['ya'] = _gelu_tanh(jnp.concatenate([ys_scr[gt, rows(hf), :] for gt in range(ys_scr.shape[0])], axis=1))
        st['glu'] = jnp.dot(st['ya'].astype(BF16), wglu_ref[...], preferred_element_type=F32)

    def st_proj(hf):
        st = state[hf]
        ya = st['ya'] * _sigmoid(st['glu'])
        st['pa'] = jnp.dot(ya.astype(BF16), wpa_ref[...], preferred_element_type=F32)
        st['pb'] = jnp.dot(yb_ref[0, rows(hf), :], wpb_ref[...], preferred_element_type=F32)

    def st_out(hf):
        st = state[hf]
        mix = ga_ref[0, rows(hf), :].astype(F32) * st['pa'] + gb_ref[0, rows(hf), :].astype(F32) * st['pb']
        st['o'] = jnp.dot(mix.astype(BF16), wout_ref[...], preferred_element_type=F32)

    def st_residual(hf):
        st = state[hf]
        x1 = x_ref[0, rows(hf), :] + gm_ref[0] * (_rms(st['o']) * npost_ref[...])
        x1_ref[0, rows(hf), :] = x1
        h = _rms(x1) * npre_ref[...]
        h = h * (1.0 + scf_ref[0]) + shf_ref[0]
        for k in range(dk):
            h2_ref[0, pl.ds(hf * nr * dk + k, nr, stride=dk), :] = h[:, k * LANES:(k + 1) * LANES]
        st['h'] = h

    def st_router(hf):
        st = state[hf]
        h = st['h']
        h_hi = h.astype(BF16)
        h_lo = (h - h_hi.astype(F32)).astype(BF16)
        st['lg'] = jnp.dot(jnp.concatenate([h_hi, h_lo], axis=1), wr2_ref[...], preferred_element_type=F32)

    def st_softmax(hf):
        lg = state[hf]['lg']
        logits = lg[:, :LANES] + lg[:, LANES:]
        lane = lax.broadcasted_iota(jnp.int32, logits.shape, 1)
        logits = jnp.where(lane < N_EXPERTS, logits, -1e30)
        ex = jnp.exp(logits - jnp.max(logits, axis=-1, keepdims=True))
        aff = ex / jnp.sum(ex, axis=-1, keepdims=True)
        aff_ref[0, rows(hf), :] = aff
        afft_ref[0, :, rows(hf)] = aff.T[:N_EXPERTS, :]

    stages = (st_relayout, st_glu, st_proj, st_out, st_residual, st_router, st_softmax)
    for k in range(len(stages) + 1):
        if k < len(stages):
            stages[k](0)
        if k >= 1:
            stages[k - 1](1)


def _merge(x, ys_rows, yb, ga, gb, gm, shf, scf, wglu, wpa, wpb, wout, npost, npre, wr2, *, tm):
    b, l, d = x.shape
    dk = d // LANES
    groups, _, ck = ys_rows.shape
    nj = l // tm

    def tok(n):
        return pl.BlockSpec((1, tm, n), lambda i, j: (i, j, 0))

    def full(a):
        return pl.BlockSpec(a.shape, lambda i, j: (0,) * a.ndim)

    mod = pl.BlockSpec((1, 1, d), lambda i, j: (i, 0, 0))
    return pl.pallas_call(
        _merge_kernel,
        grid=(b, nj),
        in_specs=[tok(d),
                  pl.BlockSpec((groups, tm // S5_CHUNK, ck), lambda i, j: (0, i * nj + j, 0)),
                  tok(yb.shape[2]), tok(d), tok(d), mod, mod, mod,
                  full(wglu), full(wpa), full(wpb), full(wout), full(npost), full(npre), full(wr2)],
        out_specs=[tok(d),
                   pl.BlockSpec((1, tm * dk, LANES), lambda i, j: (i, j, 0)),
                   tok(LANES),
                   pl.BlockSpec((1, N_EXPERTS, tm), lambda i, j: (i, 0, j))],
        out_shape=[jax.ShapeDtypeStruct((b, l, d), F32),
                   jax.ShapeDtypeStruct((b, l * dk, LANES), F32),
                   jax.ShapeDtypeStruct((b, l, LANES), F32),
                   jax.ShapeDtypeStruct((b, N_EXPERTS, l), F32)],
        scratch_shapes=[pltpu.VMEM((groups * S5_GROUP // LANES, tm, LANES), F32)],
        compiler_params=_cparams(("parallel", "parallel")),
        name="merge",
    )(x, ys_rows, yb, ga, gb, gm, shf, scf, wglu, wpa, wpb, wout, npost, npre, wr2)


def _route_kernel(a_ref, tok_ref, tri_ref, idx_ref, pos_scr, *, cap):
    a = a_ref[0]
    ne, l = a.shape
    capf = float(cap)

    def count_ge(thr):
        return jnp.sum(jnp.where(a >= thr, 1.0, 0.0), axis=1, keepdims=True)

    def narrow(mid, lo, hi):
        mid = jnp.minimum(jnp.maximum(mid, lo), hi)
        ok = count_ge(mid) >= capf
        return jnp.where(ok, mid, lo), jnp.where(ok, hi, mid)

    def geo(_, lohi):
        lo, hi = lohi
        return narrow(jnp.sqrt(jnp.maximum(lo, F32_TINY) * hi), lo, hi)

    def ari(_, lohi):
        lo, hi = lohi
        return narrow(0.5 * lo + 0.5 * hi, lo, hi)

    lohi = (jnp.zeros((ne, 1), F32), jnp.full((ne, 1), 2.0, F32))
    lohi = lax.fori_loop(0, 34, geo, lohi)
    lo, hi = lax.fori_loop(0, 8, ari, lohi)
    gt = a >= hi
    eq = (a >= lo) & jnp.logical_not(gt)
    need = capf - jnp.sum(jnp.where(gt, 1.0, 0.0), axis=1, keepdims=True)

    tri = tri_ref[...]

    def excl_cumsum(mask):
        mb = jnp.where(mask, 1.0, 0.0).astype(BF16)
        off = jnp.zeros((ne, 1), F32)
        outs = []
        for j in range(l // LANES):
            blk = mb[:, j * LANES:(j + 1) * LANES]
            outs.append(jnp.dot(blk, tri, preferred_element_type=F32) + off)
            off = off + jnp.sum(blk.astype(F32), axis=1, keepdims=True)
        return jnp.concatenate(outs, axis=1)

    sel = gt | (eq & (excl_cumsum(eq) < need))
    pos_scr[...] = jnp.where(sel, excl_cumsum(sel), -1.0)

    tc = 512
    slot = lax.broadcasted_iota(jnp.int32, (cap, tc), 0).astype(F32)

    def per_expert(e, _):
        acc = jnp.zeros((cap, LANES), F32)
        for c in range(l // tc):
            pc = pos_scr[pl.ds(e, 1), c * tc:(c + 1) * tc]
            onehot = jnp.where(pc == slot, 1.0, 0.0).astype(BF16)
            acc = acc + jnp.dot(onehot, tok_ref[c * tc:(c + 1) * tc, :], preferred_element_type=F32)
        acc_t = acc.T
        idx = acc_t[0:1, :] * 64.0 + acc_t[1:2, :]
        idx_ref[0, pl.ds(e, 1), :] = idx.astype(jnp.int32)
        return 0

    lax.fori_loop(0, ne, per_expert, 0)


def _route(aff_t, *, cap):
    b, ne, l = aff_t.shape
    tok_np = np.zeros((l, LANES), np.float32)
    tok_np[:, 0] = np.arange(l) >> 6
    tok_np[:, 1] = np.arange(l) & 63
    tok = jnp.asarray(tok_np, dtype=BF16)
    tri = jnp.asarray(np.arange(LANES)[:, None] < np.arange(LANES)[None, :], dtype=BF16)
    return pl.pallas_call(
        functools.partial(_route_kernel, cap=cap),
        grid=(b,),
        in_specs=[pl.BlockSpec((1, ne, l), lambda i: (i, 0, 0)),
                  pl.BlockSpec((l, LANES), lambda i: (0, 0)),
                  pl.BlockSpec((LANES, LANES), lambda i: (0, 0))],
        out_specs=pl.BlockSpec((1, ne, cap), lambda i: (i, 0, 0)),
        out_shape=jax.ShapeDtypeStruct((b, ne, cap), jnp.int32),
        scratch_shapes=[pltpu.VMEM((ne, l), F32)],
        compiler_params=_cparams(("parallel",)),
        name="route",
    )(aff_t, tok, tri)


ROW_UNROLL = 16


def _gather_kernel(idx_ref, h2_ref, aff_ref, xs_ref, g_ref, *, cap, dk):
    def gather(s, _):
        t = idx_ref[0, 0, s]
        xs_ref[0, 0, pl.ds(pl.multiple_of(s * dk, dk), dk), :] = h2_ref[0, pl.ds(pl.multiple_of(t * dk, dk), dk), :]
        g_ref[0, 0, pl.ds(s, 1), :] = aff_ref[0, pl.ds(t, 1), :]
        return 0

    lax.fori_loop(0, cap, gather, 0, unroll=ROW_UNROLL)


def _moe_gather(idx, h2, aff, *, d):
    b, ne, cap = idx.shape
    l = aff.shape[1]
    dk = d // LANES
    return pl.pallas_call(
        functools.partial(_gather_kernel, cap=cap, dk=dk),
        grid=(b, ne),
        in_specs=[pl.BlockSpec((1, 1, cap), lambda i, e: (i * ne + e, 0, 0), memory_space=pltpu.SMEM),
                  pl.BlockSpec((1, l * dk, LANES), lambda i, e: (i, 0, 0), pipeline_mode=pl.Buffered(1)),
                  pl.BlockSpec((1, l, LANES), lambda i, e: (i, 0, 0))],
        out_specs=[pl.BlockSpec((1, 1, cap * dk, LANES), lambda i, e: (i, e, 0, 0)),
                   pl.BlockSpec((1, 1, cap, LANES), lambda i, e: (i, e, 0, 0))],
        out_shape=[jax.ShapeDtypeStruct((b, ne, cap * dk, LANES), F32),
                   jax.ShapeDtypeStruct((b, ne, cap, LANES), F32)],
        compiler_params=_cparams(("arbitrary", "arbitrary")),
        name="moe_gather",
    )(idx.reshape(b * ne, 1, cap), h2, aff)


def _expert_kernel(xs_ref, g_ref, wg_ref, wu_ref, wd_ref, ys_ref, *, nsplit):
    e = pl.program_id(0)
    cap = g_ref.shape[2]
    dk = xs_ref.shape[2] // cap
    xs = jnp.concatenate([xs_ref[0, 0, pl.ds(k, cap, stride=dk), :] for k in range(dk)], axis=1).astype(BF16)
    tf = wg_ref.shape[2] // nsplit
    y = None
    for j in range(nsplit):
        cols = slice(j * tf, (j + 1) * tf)
        gg = jnp.dot(xs, wg_ref[0, :, cols], preferred_element_type=F32)
        uu = jnp.dot(xs, wu_ref[0, :, cols], preferred_element_type=F32)
        hid = (gg * _sigmoid(gg) * uu).astype(BF16)
        part = jnp.dot(hid, wd_ref[0, cols, :], preferred_element_type=F32)
        y = part if y is None else y + part
    g = g_ref[0, 0]
    lane = lax.broadcasted_iota(jnp.int32, g.shape, 1)
    gate = jnp.sum(jnp.where(lane == e, g, 0.0), axis=1, keepdims=True)
    yg = y * gate
    for k in range(dk):
        ys_ref[0, 0, pl.ds(k, cap, stride=dk), :] = yg[:, k * LANES:(k + 1) * LANES]


def _moe_experts(xs, gates, wg, wu, wd):
    b, ne, rows, _ = xs.shape
    cap = gates.shape[2]
    d, fdim = wg.shape[1:]
    tok = pl.BlockSpec((1, 1, rows, LANES), lambda e, i: (i, e, 0, 0))
    return pl.pallas_call(
        functools.partial(_expert_kernel, nsplit=2),
        grid=(ne, b),
        in_specs=[tok,
                  pl.BlockSpec((1, 1, cap, LANES), lambda e, i: (i, e, 0, 0)),
                  pl.BlockSpec((1, d, fdim), lambda e, i: (e, 0, 0)),
                  pl.BlockSpec((1, d, fdim), lambda e, i: (e, 0, 0)),
                  pl.BlockSpec((1, fdim, d), lambda e, i: (e, 0, 0))],
        out_specs=tok,
        out_shape=jax.ShapeDtypeStruct((b, ne, rows, LANES), F32),
        compiler_params=_cparams(("arbitrary", "arbitrary")),
        name="moe_experts",
    )(xs, gates, wg, wu, wd)


def _scatter_kernel(idx_ref, ys_ref, x1_ref, gf_ref, nw_ref, o_ref, f_scr, *, cap, ne):
    e = pl.program_id(1)
    dk = ys_ref.shape[2] // cap
    tm = x1_ref.shape[1]

    @pl.when(e == 0)
    def _():
        f_scr[...] = jnp.zeros_like(f_scr)

    @pl.when(e < ne)
    def _():
        def scatter(s0, _):
            rows = []
            vals = []
            for i in range(ROW_UNROLL):
                s = s0 * ROW_UNROLL + i
                r = pl.multiple_of(idx_ref[0, 0, s] * dk, dk)
                rows.append(r)
                vals.append(f_scr[pl.ds(r, dk), :] + ys_ref[0, 0, pl.ds(pl.multiple_of(s * dk, dk), dk), :])
            for r, v in zip(rows, vals):
                f_scr[pl.ds(r, dk), :] = v
            return 0

        lax.fori_loop(0, cap // ROW_UNROLL, scatter, 0)

    @pl.when(e >= ne)
    def _():
        base = (e - ne) * (tm * dk)
        f = jnp.concatenate([f_scr[pl.ds(base + k, tm, stride=dk), :] for k in range(dk)], axis=1)
        o_ref[0] = x1_ref[0] + gf_ref[0] * (_rms(f) * nw_ref[...])


def _moe_scatter_final(idx, ys, x1, gf, nw, *, tm):
    b, ne, cap = idx.shape
    l, d = x1.shape[1:]
    dk = d // LANES
    nt = l // tm

    def expert(e):
        return jnp.minimum(e, ne - 1)

    def tile(e):
        return jnp.maximum(e - ne, 0)

    return pl.pallas_call(
        functools.partial(_scatter_kernel, cap=cap, ne=ne),
        grid=(b, ne + nt),
        in_specs=[pl.BlockSpec((1, 1, cap), lambda i, e: (i * ne + expert(e), 0, 0), memory_space=pltpu.SMEM),
                  pl.BlockSpec((1, 1, cap * dk, LANES), lambda i, e: (i, expert(e), 0, 0)),
                  pl.BlockSpec((1, tm, d), lambda i, e: (i, tile(e), 0)),
                  pl.BlockSpec((1, 1, d), lambda i, e: (i, 0, 0)),
                  pl.BlockSpec((1, d), lambda i, e: (0, 0))],
        out_specs=pl.BlockSpec((1, tm, d), lambda i, e: (i, tile(e), 0)),
        out_shape=jax.ShapeDtypeStruct((b, l, d), F32),
        scratch_shapes=[pltpu.VMEM((l * dk, LANES), F32)],
        compiler_params=_cparams(("arbitrary", "arbitrary")),
        name="moe_scatter_final",
    )(idx.reshape(b * ne, 1, cap), ys, x1, gf, nw)


def kernel(x, c, ctx, c_ctx, w_ada, b_ada, norm_pre_mix, norm_post_mix, norm_pre_ffn, norm_post_ffn, w_in, s5_lam_re, s5_lam_im, s5_log_dt, s5_b_re, s5_b_im, s5_c_re, s5_c_im, s5_d, w_glu, da_lambda, da_subln, w_proj_a, w_proj_b, w_out, w_router, w_exp_gate, w_exp_up, w_exp_down):
    depth = w_ada.shape[0]
    assert depth == 1, "single trunk layer: the context stream's outputs are never consumed"
    b, l, d = x.shape
    lc = ctx.shape[1]
    assert b == SUBLANES and l % (S5_CHUNK * 32) == 0 and lc % (S5_CHUNK * 16) == 0
    s5w = s5_d.shape[1]
    qkw = DA_HEADS * 2 * DA_HEAD_DIM
    vw = DA_HEADS * DA_V_DIM
    widths = (s5w, qkw, vw, d)
    lam_init = 0.8 - 0.6 * math.exp(-0.3 * 0)

    c_all = jnp.zeros((2 * SUBLANES, d), F32).at[:b].set(c).at[b].set(c_ctx)
    mod = _ada(c_all, w_ada[0], b_ada[0])
    sh_m, sc_m, g_m, sh_f, sc_f, g_f = [mod[:b, i * d:(i + 1) * d].reshape(b, 1, d) for i in range(6)]
    csh_m = mod[b:b + 1, 0:d].reshape(1, 1, d)
    csc_m = mod[b:b + 1, d:2 * d].reshape(1, 1, d)

    w_in_b = w_in[0].astype(BF16)
    cos_t, sin_t = _rope_tables(l)
    npm = norm_pre_mix[0].reshape(1, d)
    u, q, k, v, ga, gb = _inproj(x, npm, sh_m, sc_m, w_in_b, cos_t, sin_t, latent=True, widths=widths, tm=1024)
    uc, kc, vc = _inproj(ctx, npm, csh_m, csc_m, w_in_b, cos_t[:lc], sin_t[:lc], latent=False, widths=widths, tm=lc)

    w_mat, m_mat, v_mat, a_mat = _s5_matrices(s5_lam_re[0], s5_lam_im[0], s5_log_dt[0], s5_b_re[0], s5_b_im[0],
                                              s5_c_re[0], s5_c_im[0], s5_d[0])
    ys_rows = _s5(uc, u, w_mat, m_mat, v_mat, a_mat, n_ctx=lc // S5_CHUNK, n_lat=l // S5_CHUNK)

    lq1, lk1, lq2, lk2 = da_lambda[0].astype(F32)
    lam = jnp.exp(jnp.sum(lq1 * lk1)) - jnp.exp(jnp.sum(lq2 * lk2)) + lam_init
    yb = _attn(lam.reshape(1, 1), q, kc, k, vc, v, da_subln[0].reshape(1, DA_V_DIM),
               tq=256, out_scale=1.0 - lam_init)

    wr = jnp.zeros((d, LANES), F32).at[:, :N_EXPERTS].set(w_router[0])
    wr_hi = wr.astype(BF16)
    wr_lo = (wr - wr_hi.astype(F32)).astype(BF16)
    wr2 = jnp.concatenate([jnp.concatenate([wr_hi, wr_lo], axis=1),
                           jnp.concatenate([wr_hi, jnp.zeros_like(wr_hi)], axis=1)], axis=0)
    x1, h2, aff, aff_t = _merge(x, ys_rows, yb, ga, gb, g_m, sh_f, sc_f,
                                w_glu[0].astype(BF16), w_proj_a[0].astype(BF16), w_proj_b[0].astype(BF16),
                                w_out[0].astype(BF16), norm_post_mix[0].reshape(1, d), norm_pre_ffn[0].reshape(1, d),
                                wr2, tm=512)

    cap = EC_CAPACITY * l // N_EXPERTS
    idx = _route(aff_t, cap=cap)
    xs, gates = _moe_gather(idx, h2, aff, d=d)
    ys = _moe_experts(xs, gates, w_exp_gate[0].astype(BF16), w_exp_up[0].astype(BF16), w_exp_down[0].astype(BF16))
    return _moe_scatter_final(idx, ys, x1, g_f, norm_post_ffn[0].reshape(1, d), tm=512)
```

```python
import functools
import math

import numpy as np

import jax
import jax.numpy as jnp
from jax import lax
from jax.experimental import pallas as pl
from jax.experimental.pallas import tpu as pltpu

F32 = jnp.float32
BF16 = jnp.bfloat16

EPS = 1e-6
GRID_W = 64
ROPE_THETA = 10000.0
S5_GROUP = 16
S5_STATE = 64
S5_DT_MAX_RE = -1e-4
S5_CHUNK = 16
DA_HEADS = 4
DA_HEAD_DIM = 64
DA_V_DIM = 128
N_EXPERTS = 16
EC_CAPACITY = 2
LANES = 128
SUBLANES = 8
VMEM_LIMIT = 60 * 1024 * 1024
LOG2E = 1.4426950408889634
F32_TINY = 1e-37


def _cparams(sem):
    return pltpu.CompilerParams(dimension_semantics=sem, vmem_limit_bytes=VMEM_LIMIT)


def _rms(x, eps=EPS):
    return x * lax.rsqrt(jnp.mean(x * x, axis=-1, keepdims=True) + eps)


def _sigmoid(x):
    return 1.0 / (1.0 + jnp.exp(-x))


def _ada_kernel(c_ref, w_ref, b_ref, o_ref):
    c = c_ref[...]
    s = (c * _sigmoid(c)).astype(BF16)
    o_ref[...] = jnp.dot(s, w_ref[...].astype(BF16), preferred_element_type=F32) + b_ref[...]


def _ada(c_all, w_ada, b_ada):
    rows, d = c_all.shape
    n = w_ada.shape[1]
    tn = 512
    return pl.pallas_call(
        _ada_kernel,
        grid=(n // tn,),
        in_specs=[pl.BlockSpec((rows, d), lambda j: (0, 0)),
                  pl.BlockSpec((d, tn), lambda j: (0, j)),
                  pl.BlockSpec((1, tn), lambda j: (0, j))],
        out_specs=pl.BlockSpec((rows, tn), lambda j: (0, j)),
        out_shape=jax.ShapeDtypeStruct((rows, n), F32),
        compiler_params=_cparams(("arbitrary",)),
        name="ada",
    )(c_all, w_ada, b_ada.reshape(1, n))


def _block_transpose8(vs):
    lane_blk = lax.broadcasted_iota(jnp.int32, vs[0].shape, 1) // S5_GROUP
    vs = list(vs)
    for s in (4, 2, 1):
        upper = (lane_blk & s) != 0
        for i in range(8):
            if i & s:
                continue
            a, b = vs[i], vs[i + s]
            vs[i] = jnp.where(upper, pltpu.roll(b, S5_GROUP * s, 1), a)
            vs[i + s] = jnp.where(upper, b, pltpu.roll(a, LANES - S5_GROUP * s, 1))
    return vs


def _tokens_to_chunks(scr, out_ref, row0, nrows):
    nc = nrows // S5_CHUNK
    c0 = row0 // S5_CHUNK
    for gt in range(scr.shape[0]):
        for j in range(S5_CHUNK // 8):
            vs = [scr[gt, pl.ds(row0 + 8 * j + i, nc, stride=S5_CHUNK), :] for i in range(8)]
            vs = _block_transpose8(vs)
            for gi in range(8):
                out_ref[gt * 8 + gi, c0:c0 + nc, j * LANES:(j + 1) * LANES] = vs[gi].astype(out_ref.dtype)


def _chunks_to_tokens(in_ref, scr, row0, nrows):
    nc = nrows // S5_CHUNK
    c0 = row0 // S5_CHUNK
    for gt in range(scr.shape[0]):
        for j in range(S5_CHUNK // 8):
            vs = [in_ref[gt * 8 + gi, c0:c0 + nc, j * LANES:(j + 1) * LANES].astype(F32) for gi in range(8)]
            vs = _block_transpose8(vs)
            for i in range(8):
                scr[gt, pl.ds(row0 + 8 * j + i, nc, stride=S5_CHUNK), :] = vs[i]


def _swap16(x):
    lane = lax.broadcasted_iota(jnp.int32, x.shape, 1)
    return jnp.where((lane & 16) == 0, pltpu.roll(x, LANES - 16, 1), pltpu.roll(x, 16, 1))


def _inproj_kernel(x_ref, nw_ref, sh_ref, sc_ref, w_ref, cos_ref, sin_ref, *refs, latent, widths):
    tm = x_ref.shape[1]
    nparts = 2 if tm % (2 * S5_CHUNK * 16) == 0 else 1
    nr = tm // nparts
    s5w, qkw, vw, dm = widths
    o_u, o_q, o_k, o_v, o_ga = 0, s5w, s5w + qkw, s5w + 2 * qkw, s5w + 2 * qkw + vw
    if latent:
        u_ref, q_ref, k_ref, v_ref, ga_ref, gb_ref, u_scr = refs
    else:
        u_ref, k_ref, v_ref, u_scr = refs
    hs = [None] * nparts

    def rows(part):
        return slice(part * nr, (part + 1) * nr)

    def proj(part, lo, n):
        return jnp.dot(hs[part], w_ref[:, lo:lo + n], preferred_element_type=F32)

    def rope(part, z, scale):
        cos = cos_ref[rows(part), :]
        sin = sin_ref[rows(part), :]
        pieces = []
        for j in range(z.shape[1] // LANES):
            zj = z[:, j * LANES:(j + 1) * LANES]
            pieces.append((zj * cos + _swap16(zj) * sin) * scale)
        return jnp.concatenate(pieces, axis=1)

    def st_norm(part):
        h = _rms(x_ref[0, rows(part), :]) * nw_ref[...]
        hs[part] = (h * (1.0 + sc_ref[0]) + sh_ref[0]).astype(BF16)

    def st_q(part):
        q_ref[0, rows(part), :] = rope(part, proj(part, o_q, qkw), DA_HEAD_DIM ** -0.5 * LOG2E).astype(BF16)

    def st_k(part):
        k = proj(part, o_k, qkw)
        k_ref[0, rows(part), :] = (rope(part, k, 1.0) if latent else k).astype(BF16)

    def st_v(part):
        v_ref[0, rows(part), :] = proj(part, o_v, vw).astype(BF16)

    def st_ga(part):
        ga_ref[0, rows(part), :] = _sigmoid(proj(part, o_ga, dm)).astype(BF16)

    def st_gb(part):
        gb_ref[0, rows(part), :] = _sigmoid(proj(part, o_ga + dm, dm)).astype(BF16)

    def st_u(part):
        u = proj(part, o_u, s5w)
        for gt in range(s5w // LANES):
            u_scr[gt, rows(part), :] = u[:, gt * LANES:(gt + 1) * LANES]
        _tokens_to_chunks(u_scr, u_ref, part * nr, nr)

    stages = (st_norm, st_q, st_k, st_v, st_ga, st_gb, st_u) if latent else (st_norm, st_k, st_v, st_u)
    for step in range(len(stages) + nparts - 1):
        for part in range(nparts):
            if 0 <= step - part < len(stages):
                stages[step - part](part)


def _inproj(x, nw, sh, sc, w_in, cos_t, sin_t, *, latent, widths, tm):
    b, l, d = x.shape
    s5w, qkw, vw, dm = widths
    groups = s5w // S5_GROUP
    nc = tm // S5_CHUNK
    per_b = sh.shape[0] > 1
    mod_spec = pl.BlockSpec((1, 1, d), (lambda i, j: (i, 0, 0)) if per_b else (lambda i, j: (0, 0, 0)))

    def tok_spec(n):
        return pl.BlockSpec((1, tm, n), lambda i, j: (i, j, 0))

    nj = l // tm
    u_spec = pl.BlockSpec((groups, nc, S5_CHUNK * S5_GROUP), lambda i, j: (0, i * nj + j, 0))
    u_shape = jax.ShapeDtypeStruct((groups, b * (l // S5_CHUNK), S5_CHUNK * S5_GROUP), BF16)
    out_w = (qkw, qkw, vw, dm, dm) if latent else (qkw, vw)
    return pl.pallas_call(
        functools.partial(_inproj_kernel, latent=latent, widths=widths),
        grid=(b, nj),
        in_specs=[tok_spec(d),
                  pl.BlockSpec((1, d), lambda i, j: (0, 0)),
                  mod_spec, mod_spec,
                  pl.BlockSpec(w_in.shape, lambda i, j: (0, 0)),
                  pl.BlockSpec((tm, LANES), lambda i, j: (j, 0)),
                  pl.BlockSpec((tm, LANES), lambda i, j: (j, 0))],
        out_specs=[u_spec] + [tok_spec(n) for n in out_w],
        out_shape=[u_shape] + [jax.ShapeDtypeStruct((b, l, n), BF16) for n in out_w],
        scratch_shapes=[pltpu.VMEM((s5w // LANES, tm, LANES), F32)],
        compiler_params=_cparams(("parallel", "parallel")),
        name="inproj_lat" if latent else "inproj_ctx",
    )(x, nw, sh, sc, w_in, cos_t, sin_t)


def _rope_tables(seq_len):
    rows = seq_len // GRID_W
    pairs = DA_HEAD_DIM // 4
    row = np.repeat(np.arange(rows), GRID_W).astype(np.float64)
    col = np.tile(np.arange(GRID_W), rows).astype(np.float64)
    inv_freq = ROPE_THETA ** (-np.arange(pairs, dtype=np.float64) / pairs)
    ra = row[:, None] * inv_freq[None, :]
    ca = col[:, None] * inv_freq[None, :]
    cos64 = np.concatenate([np.cos(ra), np.cos(ra), np.cos(ca), np.cos(ca)], axis=1)
    sin64 = np.concatenate([-np.sin(ra), np.sin(ra), -np.sin(ca), np.sin(ca)], axis=1)
    return (jnp.asarray(np.tile(cos64, (1, 2)), dtype=F32), jnp.asarray(np.tile(sin64, (1, 2)), dtype=F32))


def _s5_matrices(lam_re, lam_im, log_dt, b_re, b_im, c_re, c_im, d_skip):
    hp = lax.Precision.HIGHEST
    t = S5_CHUNK
    lam_re = jnp.minimum(lam_re.astype(F32), S5_DT_MAX_RE)
    lam_im = lam_im.astype(F32)
    dt = jnp.exp(log_dt.astype(F32))[..., None]
    g, p = lam_re.shape[1:]
    hh = S5_GROUP
    mag = jnp.exp(lam_re * dt)
    lb_re = mag * jnp.cos(lam_im * dt)
    lb_im = mag * jnp.sin(lam_im * dt)
    den = lam_re * lam_re + lam_im * lam_im
    num_re = lb_re - 1.0
    co_re = (num_re * lam_re + lb_im * lam_im) / den
    co_im = (lb_im * lam_re - num_re * lam_im) / den
    br = b_re.astype(F32)
    bi = b_im.astype(F32)
    bb_re = co_re[..., None] * br - co_im[..., None] * bi
    bb_im = co_re[..., None] * bi + co_im[..., None] * br
    j = jnp.arange(t + 1, dtype=F32)[None, None, :, None]
    pmag = jnp.exp(lam_re[:, :, None, :] * dt[:, :, None, :] * j)
    pang = lam_im[:, :, None, :] * dt[:, :, None, :] * j
    pw_re = pmag * jnp.cos(pang)
    pw_im = pmag * jnp.sin(pang)
    cr = c_re.astype(F32)
    ci = c_im.astype(F32)
    cp_re = cr[:, :, None] * pw_re[:, :, :t, None, :] - ci[:, :, None] * pw_im[:, :, :t, None, :]
    cp_im = cr[:, :, None] * pw_im[:, :, :t, None, :] + ci[:, :, None] * pw_re[:, :, :t, None, :]
    taps = (jnp.einsum('dgjhp,dgpi->dgjhi', cp_re, bb_re, precision=hp)
            - jnp.einsum('dgjhp,dgpi->dgjhi', cp_im, bb_im, precision=hp))
    sig = np.arange(t)[:, None, None]
    tau = np.arange(t)[None, :, None]
    lag = np.arange(t)[None, None, :]
    place = np.concatenate([tau - sig == lag, sig - tau == lag], axis=2).astype(np.float32)
    taps2 = jnp.concatenate([taps[0], taps[1]], axis=1)
    skip = d_skip.astype(F32).reshape(g, hh)
    diag = (np.eye(t, dtype=np.float32)[None, :, None, :, None]
            * (jnp.eye(hh, dtype=F32)[None] * skip[:, None, :])[:, None, :, None, :])
    m_mat = (jnp.einsum('stj,gjab->gsbta', place, taps2, precision=hp) + diag).reshape(g, t * hh, t * hh)
    pf_re = pw_re[0][:, :t][:, ::-1]
    pf_im = pw_im[0][:, :t][:, ::-1]
    pr_re = pw_re[1][:, :t]
    pr_im = pw_im[1][:, :t]

    def state_in(pr_, pi_, br_, bi_):
        brt = br_.transpose(0, 2, 1)[:, None]
        bit = bi_.transpose(0, 2, 1)[:, None]
        re = pr_[:, :, None, :] * brt - pi_[:, :, None, :] * bit
        im = pr_[:, :, None, :] * bit + pi_[:, :, None, :] * brt
        return re.reshape(g, t * hh, p), im.reshape(g, t * hh, p)

    wf_re, wf_im = state_in(pf_re, pf_im, bb_re[0], bb_im[0])
    wr_re, wr_im = state_in(pr_re, pr_im, bb_re[1], bb_im[1])
    w_mat = jnp.concatenate([wf_re, wr_re, wf_im, wr_im], axis=2)
    ef_re = pw_re[0][:, 1:t + 1]
    ef_im = pw_im[0][:, 1:t + 1]
    er_re = pw_re[1][:, 1:t + 1][:, ::-1]
    er_im = pw_im[1][:, 1:t + 1][:, ::-1]

    def state_out(e_re, e_im, cr_, ci_):
        g_re = cr_[:, None] * e_re[:, :, None, :] - ci_[:, None] * e_im[:, :, None, :]
        g_im = cr_[:, None] * e_im[:, :, None, :] + ci_[:, None] * e_re[:, :, None, :]
        return (g_re.transpose(0, 3, 1, 2).reshape(g, p, t * hh),
                (-g_im).transpose(0, 3, 1, 2).reshape(g, p, t * hh))

    vf_re, vf_im = state_out(ef_re, ef_im, cr[0], ci[0])
    vr_re, vr_im = state_out(er_re, er_im, cr[1], ci[1])
    v_mat = jnp.concatenate([vf_re, vr_re, vf_im, vr_im], axis=1)
    a_mat = jnp.stack([jnp.concatenate([pw_re[0][:, t], pw_re[1][:, t]], axis=1),
                       jnp.concatenate([pw_im[0][:, t], pw_im[1][:, t]], axis=1)], axis=1)
    return w_mat.astype(BF16), m_mat.astype(BF16), v_mat.astype(BF16), a_mat


def _s5_kernel(uc_ref, ul_ref, w_ref, m_ref, v_ref, a_ref, y_ref, sc_scr, sl_scr, hf_scr, hr_scr, *, n_ctx, n_lat):
    nb = SUBLANES
    half = S5_STATE
    ul = ul_ref[0]
    w = w_ref[0]
    s_c = jnp.dot(uc_ref[0], w, preferred_element_type=F32)
    s_l = jnp.dot(ul, w, preferred_element_type=F32)
    for part in range(2):
        for b in range(nb):
            sc_scr[part, pl.ds(b, n_ctx, stride=nb), :] = s_c[b * n_ctx:(b + 1) * n_ctx, part * LANES:(part + 1) * LANES]
            sl_scr[part, pl.ds(b, n_lat, stride=nb), :] = s_l[b * n_lat:(b + 1) * n_lat, part * LANES:(part + 1) * LANES]
    a = a_ref[0]
    a_re = a[0:1, :]
    a_im = a[1:2, :]
    is_fwd = lax.broadcasted_iota(jnp.int32, (nb, LANES), 1) < half

    def tile(j):
        return pl.ds(pl.multiple_of(j * nb, nb), nb)

    def pick(scr, jf, jr):
        return (jnp.where(is_fwd, scr[0, tile(jf), :], scr[0, tile(jr), :]),
                jnp.where(is_fwd, scr[1, tile(jf), :], scr[1, tile(jr), :]))

    def update(h_re, h_im, s_re, s_im):
        return a_re * h_re - a_im * h_im + s_re, a_re * h_im + a_im * h_re + s_im

    def ctx_step(i, carry):
        return update(*carry, *pick(sc_scr, i, n_ctx - 1 - i))

    def lat_step(c, carry):
        h_re, h_im = carry
        cr = n_lat - 1 - c
        hf_scr[0, tile(c), :] = h_re
        hf_scr[1, tile(c), :] = h_im
        hr_scr[0, tile(cr), :] = h_re
        hr_scr[1, tile(cr), :] = h_im
        return update(h_re, h_im, *pick(sl_scr, c, cr))

    zero = jnp.zeros((nb, LANES), F32)
    carry = lax.fori_loop(0, n_ctx, ctx_step, (zero, zero))
    lax.fori_loop(0, n_lat, lat_step, carry, unroll=2)
    lane = lax.broadcasted_iota(jnp.int32, (n_lat, LANES), 1)

    def entry_states(b):
        rows = pl.ds(b, n_lat, stride=nb)
        return jnp.concatenate([jnp.where(lane < half, hf_scr[part, rows, :], hr_scr[part, rows, :])
                                for part in range(2)], axis=1)

    hin = jnp.concatenate([entry_states(b) for b in range(nb)], axis=0).astype(BF16)
    y = jnp.dot(ul, m_ref[0], preferred_element_type=F32)
    y = y + jnp.dot(hin, v_ref[0], preferred_element_type=F32)
    y_ref[0] = y.astype(BF16)


def _s5(u_ctx, u_lat, w_mat, m_mat, v_mat, a_mat, *, n_ctx, n_lat):
    g, rc, k = u_ctx.shape
    rl = u_lat.shape[1]
    mat_spec = pl.BlockSpec((1, k, k), lambda i: (i, 0, 0))
    return pl.pallas_call(
        functools.partial(_s5_kernel, n_ctx=n_ctx, n_lat=n_lat),
        grid=(g,),
        in_specs=[pl.BlockSpec((1, rc, k), lambda i: (i, 0, 0)),
                  pl.BlockSpec((1, rl, k), lambda i: (i, 0, 0)),
                  mat_spec, mat_spec, mat_spec,
                  pl.BlockSpec((1, 2, LANES), lambda i: (i, 0, 0))],
        out_specs=pl.BlockSpec((1, rl, k), lambda i: (i, 0, 0)),
        out_shape=jax.ShapeDtypeStruct((g, rl, k), BF16),
        scratch_shapes=[pltpu.VMEM((2, rc, LANES), F32), pltpu.VMEM((2, rl, LANES), F32),
                        pltpu.VMEM((2, rl, LANES), F32), pltpu.VMEM((2, rl, LANES), F32)],
        compiler_params=_cparams(("parallel",)),
        name="s5",
    )(u_ctx, u_lat, w_mat, m_mat, v_mat, a_mat)


ATTN_KEY_CHUNK = 256
ATTN_ITEM_ROWS = 128


def _attn_kernel(lam_ref, q_ref, ktc_ref, ktl_ref, vc_ref, vl_ref, sw_ref, o_ref, s_a, s_b, p_a, p_b, *, out_scale):
    lam = lam_ref[0, 0]
    sw = sw_ref[...]
    tr = ATTN_ITEM_ROWS
    lk = ktc_ref.shape[3] + ktl_ref.shape[3]
    assert ktc_ref.shape[3] == ATTN_KEY_CHUNK

    def kt_chunk(h, c):
        return ktc_ref[0, h] if c == 0 else ktl_ref[0, h, :, keys(c - 1)]

    def v_chunk(c, lanes):
        return vc_ref[0, :, lanes] if c == 0 else vl_ref[0, keys(c - 1), lanes]

    kc = ATTN_KEY_CHUNK
    nch = lk // kc
    s_bufs = (s_a, s_b)
    p_bufs = (p_a, p_b)
    lane = lax.broadcasted_iota(jnp.int32, (tr, DA_V_DIM), 1)
    work = [(r, h) for r in range(q_ref.shape[1] // tr) for h in range(DA_HEADS)]
    items = [dict() for _ in work]

    def keys(c):
        return slice(c * kc, (c + 1) * kc)

    def rows(i):
        return slice(work[i][0] * tr, (work[i][0] + 1) * tr)

    def cols(i):
        return slice(work[i][1] * DA_V_DIM, (work[i][1] + 1) * DA_V_DIM)

    def stage_a(i, c):
        it = items[i]
        if c == 0:
            q = q_ref[0, rows(i), cols(i)]
            zero = jnp.zeros_like(q)
            it['q2'] = jnp.concatenate([jnp.where(lane < DA_HEAD_DIM, q, zero),
                                        jnp.where(lane >= DA_HEAD_DIM, q, zero)], axis=0)
        s = jnp.dot(it['q2'], kt_chunk(work[i][1], c), preferred_element_type=F32)
        s_bufs[i % 2][:, keys(c)] = s
        mx = jnp.maximum(s[:, :LANES], s[:, LANES:])
        it['mx'] = mx if c == 0 else jnp.maximum(it['mx'], mx)
        if c == nch - 1:
            it['m'] = jnp.broadcast_to(jnp.max(it['mx'], axis=-1, keepdims=True), (2 * tr, kc))

    def stage_b(i, c):
        it = items[i]
        p = jnp.exp2(s_bufs[i % 2][:, keys(c)] - it['m'])
        ps = p[:, :LANES] + p[:, LANES:]
        it['ls'] = ps if c == 0 else it['ls'] + ps
        p_bufs[i % 2][:, keys(c)] = p.astype(BF16)
        if c == nch - 1:
            l = jnp.sum(it['ls'], axis=-1, keepdims=True)
            it['r1'] = jnp.broadcast_to((1.0 / l[:tr]).astype(BF16), (tr, kc))
            it['r2'] = jnp.broadcast_to((lam / l[tr:]).astype(BF16), (tr, kc))

    def stage_c(i, c):
        it = items[i]
        pb = p_bufs[i % 2]
        pd = pb[0:tr, keys(c)] * it['r1'] - pb[tr:2 * tr, keys(c)] * it['r2']
        part = jnp.dot(pd, v_chunk(c, cols(i)), preferred_element_type=F32)
        it['acc'] = part if c == 0 else it['acc'] + part
        if c == nch - 1:
            o_ref[0, rows(i), cols(i)] = (_rms(it['acc']) * sw * out_scale).astype(BF16)

    n = len(work)
    for slot in range(n + 2):
        for c in range(nch):
            if slot < n:
                stage_a(slot, c)
            if 0 <= slot - 1 < n:
                stage_b(slot - 1, c)
            if 0 <= slot - 2 < n:
                stage_c(slot - 2, c)


def _attn(lam, q, kt_ctx, kt_lat, v_ctx, v_lat, subln, *, tq, out_scale):
    b, l, w = q.shape
    lc = v_ctx.shape[1]
    lk = lc + l
    assert l % ATTN_KEY_CHUNK == 0 and tq % ATTN_ITEM_ROWS == 0
    tr = ATTN_ITEM_ROWS
    return pl.pallas_call(
        functools.partial(_attn_kernel, out_scale=out_scale),
        scratch_shapes=[pltpu.VMEM((2 * tr, lk), F32), pltpu.VMEM((2 * tr, lk), F32),
                        pltpu.VMEM((2 * tr, lk), BF16), pltpu.VMEM((2 * tr, lk), BF16)],
        grid=(b, l // tq),
        in_specs=[pl.BlockSpec(memory_space=pltpu.SMEM),
                  pl.BlockSpec((1, tq, w), lambda i, j: (i, j, 0)),
                  pl.BlockSpec((1, DA_HEADS, DA_V_DIM, lc), lambda i, j: (i, 0, 0, 0)),
                  pl.BlockSpec((1, DA_HEADS, DA_V_DIM, l), lambda i, j: (i, 0, 0, 0)),
                  pl.BlockSpec((1, lc, w), lambda i, j: (i, 0, 0)),
                  pl.BlockSpec((1, l, w), lambda i, j: (i, 0, 0)),
                  pl.BlockSpec((1, DA_V_DIM), lambda i, j: (0, 0))],
        out_specs=pl.BlockSpec((1, tq, w), lambda i, j: (i, j, 0)),
        out_shape=jax.ShapeDtypeStruct((b, l, w), BF16),
        compiler_params=_cparams(("parallel", "parallel")),
        name="attn",
    )(lam, q, kt_ctx, kt_lat, v_ctx, v_lat, subln)


def _gelu_tanh(x):
    return 0.5 * x * (1.0 + jnp.tanh(math.sqrt(2.0 / math.pi) * (x + 0.044715 * (x * x * x))))


def _merge_kernel(x_ref, ys_ref, yb_ref, ga_ref, gb_ref, gm_ref, shf_ref, scf_ref,
                  wglu_ref, wpa_ref, wpb_ref, wout_ref, npost_ref, npre_ref, wr2_ref,
                  x1_ref, h2_ref, aff_ref, afft_ref, ys_scr):
    tm, d = x_ref.shape[1:]
    dk = d // LANES
    nr = tm // 2
    state = [dict(), dict()]

    def rows(hf):
        return slice(hf * nr, (hf + 1) * nr)

    def st_relayout(hf):
        _chunks_to_tokens(ys_ref, ys_scr, hf * nr, nr)

    def st_glu(hf):
        st = state[hf]
        st['ya'] = _gelu_tanh(jnp.concatenate([ys_scr[gt, rows(hf), :] for gt in range(ys_scr.shape[0])], axis=1))
        st['glu'] = jnp.dot(st['ya'].astype(BF16), wglu_ref[...], preferred_element_type=F32)

    def st_proj(hf):
        st = state[hf]
        ya = st['ya'] * _sigmoid(st['glu'])
        st['pa'] = jnp.dot(ya.astype(BF16), wpa_ref[...], preferred_element_type=F32)
        st['pb'] = jnp.dot(yb_ref[0, rows(hf), :], wpb_ref[...], preferred_element_type=F32)

    def st_out(hf):
        st = state[hf]
        mix = ga_ref[0, rows(hf), :].astype(F32) * st['pa'] + gb_ref[0, rows(hf), :].astype(F32) * st['pb']
        st['o'] = jnp.dot(mix.astype(BF16), wout_ref[...], preferred_element_type=F32)

    def st_residual(hf):
        st = state[hf]
        x1 = x_ref[0, rows(hf), :] + gm_ref[0] * (_rms(st['o']) * npost_ref[...])
        x1_ref[0, rows(hf), :] = x1
        h = _rms(x1) * npre_ref[...]
        h = h * (1.0 + scf_ref[0]) + shf_ref[0]
        for k in range(dk):
            h2_ref[0, pl.ds(hf * nr * dk + k, nr, stride=dk), :] = h[:, k * LANES:(k + 1) * LANES]
        st['h'] = h

    def st_router(hf):
        st = state[hf]
        h = st['h']
        h_hi = h.astype(BF16)
        h_lo = (h - h_hi.astype(F32)).astype(BF16)
        st['lg'] = jnp.dot(jnp.concatenate([h_hi, h_lo], axis=1), wr2_ref[...], preferred_element_type=F32)

    def st_softmax(hf):
        lg = state[hf]['lg']
        logits = lg[:, :LANES] + lg[:, LANES:]
        lane = lax.broadcasted_iota(jnp.int32, logits.shape, 1)
        logits = jnp.where(lane < N_EXPERTS, logits, -1e30)
        ex = jnp.exp(logits - jnp.max(logits, axis=-1, keepdims=True))
        aff = ex / jnp.sum(ex, axis=-1, keepdims=True)
        aff_ref[0, rows(hf), :] = aff
        afft_ref[0, :, rows(hf)] = aff.T[:N_EXPERTS, :]

    stages = (st_relayout, st_glu, st_proj, st_out, st_residual, st_router, st_softmax)
    for k in range(len(stages) + 1):
        if k < len(stages):
            stages[k](0)
        if k >= 1:
            stages[k - 1](1)


def _merge(x, ys_rows, yb, ga, gb, gm, shf, scf, wglu, wpa, wpb, wout, npost, npre, wr2, *, tm):
    b, l, d = x.shape
    dk = d // LANES
    groups, _, ck = ys_rows.shape
    nj = l // tm

    def tok(n):
        return pl.BlockSpec((1, tm, n), lambda i, j: (i, j, 0))

    def full(a):
        return pl.BlockSpec(a.shape, lambda i, j: (0,) * a.ndim)

    mod = pl.BlockSpec((1, 1, d), lambda i, j: (i, 0, 0))
    return pl.pallas_call(
        _merge_kernel,
        grid=(b, nj),
        in_specs=[tok(d),
                  pl.BlockSpec((groups, tm // S5_CHUNK, ck), lambda i, j: (0, i * nj + j, 0)),
                  tok(yb.shape[2]), tok(d), tok(d), mod, mod, mod,
                  full(wglu), full(wpa), full(wpb), full(wout), full(npost), full(npre), full(wr2)],
        out_specs=[tok(d),
                   pl.BlockSpec((1, tm * dk, LANES), lambda i, j: (i, j, 0)),
                   tok(LANES),
                   pl.BlockSpec((1, N_EXPERTS, tm), lambda i, j: (i, 0, j))],
        out_shape=[jax.ShapeDtypeStruct((b, l, d), F32),
                   jax.ShapeDtypeStruct((b, l * dk, LANES), F32),
                   jax.ShapeDtypeStruct((b, l, LANES), F32),
                   jax.ShapeDtypeStruct((b, N_EXPERTS, l), F32)],
        scratch_shapes=[pltpu.VMEM((groups * S5_GROUP // LANES, tm, LANES), F32)],
        compiler_params=_cparams(("parallel", "parallel")),
        name="merge",
    )(x, ys_rows, yb, ga, gb, gm, shf, scf, wglu, wpa, wpb, wout, npost, npre, wr2)


def _route_kernel(a_ref, tok_ref, tri_ref, idx_ref, pos_scr, *, cap):
    a = a_ref[0]
    ne, l = a.shape
    capf = float(cap)

    def count_ge(thr):
        return jnp.sum(jnp.where(a >= thr, 1.0, 0.0), axis=1, keepdims=True)

    def narrow(mid, lo, hi):
        mid = jnp.minimum(jnp.maximum(mid, lo), hi)
        ok = count_ge(mid) >= capf
        return jnp.where(ok, mid, lo), jnp.where(ok, hi, mid)

    def geo(_, lohi):
        lo, hi = lohi
        return narrow(jnp.sqrt(jnp.maximum(lo, F32_TINY) * hi), lo, hi)

    def ari(_, lohi):
        lo, hi = lohi
        return narrow(0.5 * lo + 0.5 * hi, lo, hi)

    lohi = (jnp.zeros((ne, 1), F32), jnp.full((ne, 1), 2.0, F32))
    lohi = lax.fori_loop(0, 34, geo, lohi)
    lo, hi = lax.fori_loop(0, 8, ari, lohi)
    gt = a >= hi
    eq = (a >= lo) & jnp.logical_not(gt)
    need = capf - jnp.sum(jnp.where(gt, 1.0, 0.0), axis=1, keepdims=True)

    tri = tri_ref[...]

    def excl_cumsum(mask):
        mb = jnp.where(mask, 1.0, 0.0).astype(BF16)
        off = jnp.zeros((ne, 1), F32)
        outs = []
        for j in range(l // LANES):
            blk = mb[:, j * LANES:(j + 1) * LANES]
            outs.append(jnp.dot(blk, tri, preferred_element_type=F32) + off)
            off = off + jnp.sum(blk.astype(F32), axis=1, keepdims=True)
        return jnp.concatenate(outs, axis=1)

    sel = gt | (eq & (excl_cumsum(eq) < need))
    pos_scr[...] = jnp.where(sel, excl_cumsum(sel), -1.0)

    tc = 512
    slot = lax.broadcasted_iota(jnp.int32, (cap, tc), 0).astype(F32)

    def per_expert(e, _):
        acc = jnp.zeros((cap, LANES), F32)
        for c in range(l // tc):
            pc = pos_scr[pl.ds(e, 1), c * tc:(c + 1) * tc]
            onehot = jnp.where(pc == slot, 1.0, 0.0).astype(BF16)
            acc = acc + jnp.dot(onehot, tok_ref[c * tc:(c + 1) * tc, :], preferred_element_type=F32)
        acc_t = acc.T
        idx = acc_t[0:1, :] * 64.0 + acc_t[1:2, :]
        idx_ref[0, pl.ds(e, 1), :] = idx.astype(jnp.int32)
        return 0

    lax.fori_loop(0, ne, per_expert, 0)


def _route(aff_t, *, cap):
    b, ne, l = aff_t.shape
    tok_np = np.zeros((l, LANES), np.float32)
    tok_np[:, 0] = np.arange(l) >> 6
    tok_np[:, 1] = np.arange(l) & 63
    tok = jnp.asarray(tok_np, dtype=BF16)
    tri = jnp.asarray(np.arange(LANES)[:, None] < np.arange(LANES)[None, :], dtype=BF16)
    return pl.pallas_call(
        functools.partial(_route_kernel, cap=cap),
        grid=(b,),
        in_specs=[pl.BlockSpec((1, ne, l), lambda i: (i, 0, 0)),
                  pl.BlockSpec((l, LANES), lambda i: (0, 0)),
                  pl.BlockSpec((LANES, LANES), lambda i: (0, 0))],
        out_specs=pl.BlockSpec((1, ne, cap), lambda i: (i, 0, 0)),
        out_shape=jax.ShapeDtypeStruct((b, ne, cap), jnp.int32),
        scratch_shapes=[pltpu.VMEM((ne, l), F32)],
        compiler_params=_cparams(("parallel",)),
        name="route",
    )(aff_t, tok, tri)


ROW_UNROLL = 16


def _gather_kernel(idx_ref, h2_ref, aff_ref, xs_ref, g_ref, *, cap, dk):
    def gather(s, _):
        t = idx_ref[0, 0, s]
        xs_ref[0, 0, pl.ds(pl.multiple_of(s * dk, dk), dk), :] = h2_ref[0, pl.ds(pl.multiple_of(t * dk, dk), dk), :]
        g_ref[0, 0, pl.ds(s, 1), :] = aff_ref[0, pl.ds(t, 1), :]
        return 0

    lax.fori_loop(0, cap, gather, 0, unroll=ROW_UNROLL)


def _moe_gather(idx, h2, aff, *, d):
    b, ne, cap = idx.shape
    l = aff.shape[1]
    dk = d // LANES
    return pl.pallas_call(
        functools.partial(_gather_kernel, cap=cap, dk=dk),
        grid=(b, ne),
        in_specs=[pl.BlockSpec((1, 1, cap), lambda i, e: (i * ne + e, 0, 0), memory_space=pltpu.SMEM),
                  pl.BlockSpec((1, l * dk, LANES), lambda i, e: (i, 0, 0)),
                  pl.BlockSpec((1, l, LANES), lambda i, e: (i, 0, 0))],
        out_specs=[pl.BlockSpec((1, 1, cap * dk, LANES), lambda i, e: (i, e, 0, 0)),
                   pl.BlockSpec((1, 1, cap, LANES), lambda i, e: (i, e, 0, 0))],
        out_shape=[jax.ShapeDtypeStruct((b, ne, cap * dk, LANES), F32),
                   jax.ShapeDtypeStruct((b, ne, cap, LANES), F32)],
        compiler_params=_cparams(("arbitrary", "arbitrary")),
        name="moe_gather",
    )(idx.reshape(b * ne, 1, cap), h2, aff)


def _expert_kernel(xs_ref, g_ref, wg_ref, wu_ref, wd_ref, ys_ref, *, nsplit):
    e = pl.program_id(0)
    cap = g_ref.shape[2]
    dk = xs_ref.shape[2] // cap
    xs = jnp.concatenate([xs_ref[0, 0, pl.ds(k, cap, stride=dk), :] for k in range(dk)], axis=1).astype(BF16)
    tf = wg_ref.shape[2] // nsplit
    y = None
    for j in range(nsplit):
        cols = slice(j * tf, (j + 1) * tf)
        gg = jnp.dot(xs, wg_ref[0, :, cols], preferred_element_type=F32)
        uu = jnp.dot(xs, wu_ref[0, :, cols], preferred_element_type=F32)
        hid = (gg * _sigmoid(gg) * uu).astype(BF16)
        part = jnp.dot(hid, wd_ref[0, cols, :], preferred_element_type=F32)
        y = part if y is None else y + part
    g = g_ref[0, 0]
    lane = lax.broadcasted_iota(jnp.int32, g.shape, 1)
    gate = jnp.sum(jnp.where(lane == e, g, 0.0), axis=1, keepdims=True)
    yg = y * gate
    for k in range(dk):
        ys_ref[0, 0, pl.ds(k, cap, stride=dk), :] = yg[:, k * LANES:(k + 1) * LANES]


def _moe_experts(xs, gates, wg, wu, wd):
    b, ne, rows, _ = xs.shape
    cap = gates.shape[2]
    d, fdim = wg.shape[1:]
    tok = pl.BlockSpec((1, 1, rows, LANES), lambda e, i: (i, e, 0, 0))
    return pl.pallas_call(
        functools.partial(_expert_kernel, nsplit=2),
        grid=(ne, b),
        in_specs=[tok,
                  pl.BlockSpec((1, 1, cap, LANES), lambda e, i: (i, e, 0, 0)),
                  pl.BlockSpec((1, d, fdim), lambda e, i: (e, 0, 0)),
                  pl.BlockSpec((1, d, fdim), lambda e, i: (e, 0, 0)),
                  pl.BlockSpec((1, fdim, d), lambda e, i: (e, 0, 0))],
        out_specs=tok,
        out_shape=jax.ShapeDtypeStruct((b, ne, rows, LANES), F32),
        compiler_params=_cparams(("arbitrary", "arbitrary")),
        name="moe_experts",
    )(xs, gates, wg, wu, wd)


def _scatter_kernel(idx_ref, ys_ref, x1_ref, gf_ref, nw_ref, o_ref, f_scr, *, cap, ne):
    e = pl.program_id(1)
    dk = ys_ref.shape[2] // cap
    tm = x1_ref.shape[1]

    @pl.when(e == 0)
    def _():
        f_scr[...] = jnp.zeros_like(f_scr)

    @pl.when(e < ne)
    def _():
        def scatter(s0, _):
            rows = []
            vals = []
            for i in range(ROW_UNROLL):
                s = s0 * ROW_UNROLL + i
                r = pl.multiple_of(idx_ref[0, 0, s] * dk, dk)
                rows.append(r)
                vals.append(f_scr[pl.ds(r, dk), :] + ys_ref[0, 0, pl.ds(pl.multiple_of(s * dk, dk), dk), :])
            for r, v in zip(rows, vals):
                f_scr[pl.ds(r, dk), :] = v
            return 0

        lax.fori_loop(0, cap // ROW_UNROLL, scatter, 0)

    @pl.when(e >= ne)
    def _():
        base = (e - ne) * (tm * dk)
        f = jnp.concatenate([f_scr[pl.ds(base + k, tm, stride=dk), :] for k in range(dk)], axis=1)
        o_ref[0] = x1_ref[0] + gf_ref[0] * (_rms(f) * nw_ref[...])


def _moe_scatter_final(idx, ys, x1, gf, nw, *, tm):
    b, ne, cap = idx.shape
    l, d = x1.shape[1:]
    dk = d // LANES
    nt = l // tm

    def expert(e):
        return jnp.minimum(e, ne - 1)

    def tile(e):
        return jnp.maximum(e - ne, 0)

    return pl.pallas_call(
        functools.partial(_scatter_kernel, cap=cap, ne=ne),
        grid=(b, ne + nt),
        in_specs=[pl.BlockSpec((1, 1, cap), lambda i, e: (i * ne + expert(e), 0, 0), memory_space=pltpu.SMEM),
                  pl.BlockSpec((1, 1, cap * dk, LANES), lambda i, e: (i, expert(e), 0, 0)),
                  pl.BlockSpec((1, tm, d), lambda i, e: (i, tile(e), 0)),
                  pl.BlockSpec((1, 1, d), lambda i, e: (i, 0, 0)),
                  pl.BlockSpec((1, d), lambda i, e: (0, 0))],
        out_specs=pl.BlockSpec((1, tm, d), lambda i, e: (i, tile(e), 0)),
        out_shape=jax.ShapeDtypeStruct((b, l, d), F32),
        scratch_shapes=[pltpu.VMEM((l * dk, LANES), F32)],
        compiler_params=_cparams(("arbitrary", "arbitrary")),
        name="moe_scatter_final",
    )(idx.reshape(b * ne, 1, cap), ys, x1, gf, nw)


def kernel(x, c, ctx, c_ctx, w_ada, b_ada, norm_pre_mix, norm_post_mix, norm_pre_ffn, norm_post_ffn, w_in, s5_lam_re, s5_lam_im, s5_log_dt, s5_b_re, s5_b_im, s5_c_re, s5_c_im, s5_d, w_glu, da_lambda, da_subln, w_proj_a, w_proj_b, w_out, w_router, w_exp_gate, w_exp_up, w_exp_down):
    depth = w_ada.shape[0]
    assert depth == 1, "single trunk layer: the context stream's outputs are never consumed"
    b, l, d = x.shape
    lc = ctx.shape[1]
    assert b == SUBLANES and l % (S5_CHUNK * 32) == 0 and lc % (S5_CHUNK * 16) == 0
    s5w = s5_d.shape[1]
    qkw = DA_HEADS * 2 * DA_HEAD_DIM
    vw = DA_HEADS * DA_V_DIM
    widths = (s5w, qkw, vw, d)
    lam_init = 0.8 - 0.6 * math.exp(-0.3 * 0)

    c_all = jnp.zeros((2 * SUBLANES, d), F32).at[:b].set(c).at[b].set(c_ctx)
    mod = _ada(c_all, w_ada[0], b_ada[0])
    sh_m, sc_m, g_m, sh_f, sc_f, g_f = [mod[:b, i * d:(i + 1) * d].reshape(b, 1, d) for i in range(6)]
    csh_m = mod[b:b + 1, 0:d].reshape(1, 1, d)
    csc_m = mod[b:b + 1, d:2 * d].reshape(1, 1, d)

    w_in_b = w_in[0].astype(BF16)
    cos_t, sin_t = _rope_tables(l)
    npm = norm_pre_mix[0].reshape(1, d)
    u, q, k, v, ga, gb = _inproj(x, npm, sh_m, sc_m, w_in_b, cos_t, sin_t, latent=True, widths=widths, tm=1024)
    uc, kc, vc = _inproj(ctx, npm, csh_m, csc_m, w_in_b, cos_t[:lc], sin_t[:lc], latent=False, widths=widths, tm=lc)

    w_mat, m_mat, v_mat, a_mat = _s5_matrices(s5_lam_re[0], s5_lam_im[0], s5_log_dt[0], s5_b_re[0], s5_b_im[0],
                                              s5_c_re[0], s5_c_im[0], s5_d[0])
    ys_rows = _s5(uc, u, w_mat, m_mat, v_mat, a_mat, n_ctx=lc // S5_CHUNK, n_lat=l // S5_CHUNK)

    lq1, lk1, lq2, lk2 = da_lambda[0].astype(F32)
    lam = jnp.exp(jnp.sum(lq1 * lk1)) - jnp.exp(jnp.sum(lq2 * lk2)) + lam_init
    kt_ctx = kc.reshape(b, lc, DA_HEADS, DA_V_DIM).transpose(0, 2, 3, 1)
    kt_lat = k.reshape(b, l, DA_HEADS, DA_V_DIM).transpose(0, 2, 3, 1)
    yb = _attn(lam.reshape(1, 1), q, kt_ctx, kt_lat, vc, v, da_subln[0].reshape(1, DA_V_DIM),
               tq=256, out_scale=1.0 - lam_init)

    wr = jnp.zeros((d, LANES), F32).at[:, :N_EXPERTS].set(w_router[0])
    wr_hi = wr.astype(BF16)
    wr_lo = (wr - wr_hi.astype(F32)).astype(BF16)
    wr2 = jnp.concatenate([jnp.concatenate([wr_hi, wr_lo], axis=1),
                           jnp.concatenate([wr_hi, jnp.zeros_like(wr_hi)], axis=1)], axis=0)
    x1, h2, aff, aff_t = _merge(x, ys_rows, yb, ga, gb, g_m, sh_f, sc_f,
                                w_glu[0].astype(BF16), w_proj_a[0].astype(BF16), w_proj_b[0].astype(BF16),
                                w_out[0].astype(BF16), norm_post_mix[0].reshape(1, d), norm_pre_ffn[0].reshape(1, d),
                                wr2, tm=512)

    cap = EC_CAPACITY * l // N_EXPERTS
    idx = _route(aff_t, cap=cap)
    xs, gates = _moe_gather(idx, h2, aff, d=d)
    ys = _moe_experts(xs, gates, w_exp_gate[0].astype(BF16), w_exp_up[0].astype(BF16), w_exp_down[0].astype(BF16))
    return _moe_scatter_final(idx, ys, x1, g_f, norm_post_ffn[0].reshape(1, d), tm=512)
```

```python
import functools
import math

import numpy as np

import jax
import jax.numpy as jnp
from jax import lax
from jax.experimental import pallas as pl
from jax.experimental.pallas import tpu as pltpu

F32 = jnp.float32
BF16 = jnp.bfloat16

EPS = 1e-6
GRID_W = 64
ROPE_THETA = 10000.0
S5_GROUP = 16
S5_STATE = 64
S5_DT_MAX_RE = -1e-4
S5_CHUNK = 16
DA_HEADS = 4
DA_HEAD_DIM = 64
DA_V_DIM = 128
N_EXPERTS = 16
EC_CAPACITY = 2
LANES = 128
SUBLANES = 8
VMEM_LIMIT = 60 * 1024 * 1024
LOG2E = 1.4426950408889634
F32_TINY = 1e-37


def _cparams(sem):
    return pltpu.CompilerParams(dimension_semantics=sem, vmem_limit_bytes=VMEM_LIMIT)


def _rms(x, eps=EPS):
    return x * lax.rsqrt(jnp.mean(x * x, axis=-1, keepdims=True) + eps)


def _sigmoid(x):
    return 1.0 / (1.0 + jnp.exp(-x))


def _ada_kernel(c_ref, w_ref, b_ref, o_ref):
    c = c_ref[...]
    s = (c * _sigmoid(c)).astype(BF16)
    o_ref[...] = jnp.dot(s, w_ref[...].astype(BF16), preferred_element_type=F32) + b_ref[...]


def _ada(c_all, w_ada, b_ada):
    rows, d = c_all.shape
    n = w_ada.shape[1]
    tn = 512
    return pl.pallas_call(
        _ada_kernel,
        grid=(n // tn,),
        in_specs=[pl.BlockSpec((rows, d), lambda j: (0, 0)),
                  pl.BlockSpec((d, tn), lambda j: (0, j)),
                  pl.BlockSpec((1, tn), lambda j: (0, j))],
        out_specs=pl.BlockSpec((rows, tn), lambda j: (0, j)),
        out_shape=jax.ShapeDtypeStruct((rows, n), F32),
        compiler_params=_cparams(("arbitrary",)),
        name="ada",
    )(c_all, w_ada, b_ada.reshape(1, n))


def _block_transpose8(vs):
    lane_blk = lax.broadcasted_iota(jnp.int32, vs[0].shape, 1) // S5_GROUP
    vs = list(vs)
    for s in (4, 2, 1):
        upper = (lane_blk & s) != 0
        for i in range(8):
            if i & s:
                continue
            a, b = vs[i], vs[i + s]
            vs[i] = jnp.where(upper, pltpu.roll(b, S5_GROUP * s, 1), a)
            vs[i + s] = jnp.where(upper, b, pltpu.roll(a, LANES - S5_GROUP * s, 1))
    return vs


def _tokens_to_chunks(scr, out_ref, row0, nrows):
    nc = nrows // S5_CHUNK
    c0 = row0 // S5_CHUNK
    for gt in range(scr.shape[0]):
        for j in range(S5_CHUNK // 8):
            vs = [scr[gt, pl.ds(row0 + 8 * j + i, nc, stride=S5_CHUNK), :] for i in range(8)]
            vs = _block_transpose8(vs)
            for gi in range(8):
                out_ref[gt * 8 + gi, c0:c0 + nc, j * LANES:(j + 1) * LANES] = vs[gi].astype(out_ref.dtype)


def _chunks_to_tokens(in_ref, scr, row0, nrows):
    nc = nrows // S5_CHUNK
    c0 = row0 // S5_CHUNK
    for gt in range(scr.shape[0]):
        for j in range(S5_CHUNK // 8):
            vs = [in_ref[gt * 8 + gi, c0:c0 + nc, j * LANES:(j + 1) * LANES].astype(F32) for gi in range(8)]
            vs = _block_transpose8(vs)
            for i in range(8):
                scr[gt, pl.ds(row0 + 8 * j + i, nc, stride=S5_CHUNK), :] = vs[i]


def _swap16(x):
    lane = lax.broadcasted_iota(jnp.int32, x.shape, 1)
    return jnp.where((lane & 16) == 0, pltpu.roll(x, LANES - 16, 1), pltpu.roll(x, 16, 1))


def _inproj_kernel(x_ref, nw_ref, sh_ref, sc_ref, w_ref, cos_ref, sin_ref, *refs, latent, widths):
    tm = x_ref.shape[1]
    nparts = 2 if tm % (2 * S5_CHUNK * 16) == 0 else 1
    nr = tm // nparts
    s5w, qkw, vw, dm = widths
    o_u, o_q, o_k, o_v, o_ga = 0, s5w, s5w + qkw, s5w + 2 * qkw, s5w + 2 * qkw + vw
    if latent:
        u_ref, q_ref, k_ref, v_ref, ga_ref, gb_ref, u_scr = refs
    else:
        u_ref, k_ref, v_ref, u_scr = refs
    hs = [None] * nparts

    def rows(part):
        return slice(part * nr, (part + 1) * nr)

    def proj(part, lo, n):
        return jnp.dot(hs[part], w_ref[:, lo:lo + n], preferred_element_type=F32)

    def rope(part, z, scale):
        cos = cos_ref[rows(part), :]
        sin = sin_ref[rows(part), :]
        pieces = []
        for j in range(z.shape[1] // LANES):
            zj = z[:, j * LANES:(j + 1) * LANES]
            pieces.append((zj * cos + _swap16(zj) * sin) * scale)
        return jnp.concatenate(pieces, axis=1)

    def st_norm(part):
        h = _rms(x_ref[0, rows(part), :]) * nw_ref[...]
        hs[part] = (h * (1.0 + sc_ref[0]) + sh_ref[0]).astype(BF16)

    def st_q(part):
        q_ref[0, rows(part), :] = rope(part, proj(part, o_q, qkw), DA_HEAD_DIM ** -0.5 * LOG2E).astype(BF16)

    def st_k(part):
        k = proj(part, o_k, qkw)
        k_ref[0, rows(part), :] = (rope(part, k, 1.0) if latent else k).astype(BF16)

    def st_v(part):
        v_ref[0, rows(part), :] = proj(part, o_v, vw).astype(BF16)

    def st_ga(part):
        ga_ref[0, rows(part), :] = _sigmoid(proj(part, o_ga, dm)).astype(BF16)

    def st_gb(part):
        gb_ref[0, rows(part), :] = _sigmoid(proj(part, o_ga + dm, dm)).astype(BF16)

    def st_u(part):
        u = proj(part, o_u, s5w)
        for gt in range(s5w // LANES):
            u_scr[gt, rows(part), :] = u[:, gt * LANES:(gt + 1) * LANES]
        _tokens_to_chunks(u_scr, u_ref, part * nr, nr)

    stages = (st_norm, st_q, st_k, st_v, st_ga, st_gb, st_u) if latent else (st_norm, st_k, st_v, st_u)
    for step in range(len(stages) + nparts - 1):
        for part in range(nparts):
            if 0 <= step - part < len(stages):
                stages[step - part](part)


def _inproj(x, nw, sh, sc, w_in, cos_t, sin_t, *, latent, widths, tm):
    b, l, d = x.shape
    s5w, qkw, vw, dm = widths
    groups = s5w // S5_GROUP
    nc = tm // S5_CHUNK
    per_b = sh.shape[0] > 1
    mod_spec = pl.BlockSpec((1, 1, d), (lambda i, j: (i, 0, 0)) if per_b else (lambda i, j: (0, 0, 0)))

    def tok_spec(n):
        return pl.BlockSpec((1, tm, n), lambda i, j: (i, j, 0))

    nj = l // tm
    u_spec = pl.BlockSpec((groups, nc, S5_CHUNK * S5_GROUP), lambda i, j: (0, i * nj + j, 0))
    u_shape = jax.ShapeDtypeStruct((groups, b * (l // S5_CHUNK), S5_CHUNK * S5_GROUP), BF16)
    out_w = (qkw, qkw, vw, dm, dm) if latent else (qkw, vw)
    return pl.pallas_call(
        functools.partial(_inproj_kernel, latent=latent, widths=widths),
        grid=(b, nj),
        in_specs=[tok_spec(d),
                  pl.BlockSpec((1, d), lambda i, j: (0, 0)),
                  mod_spec, mod_spec,
                  pl.BlockSpec(w_in.shape, lambda i, j: (0, 0)),
                  pl.BlockSpec((tm, LANES), lambda i, j: (j, 0)),
                  pl.BlockSpec((tm, LANES), lambda i, j: (j, 0))],
        out_specs=[u_spec] + [tok_spec(n) for n in out_w],
        out_shape=[u_shape] + [jax.ShapeDtypeStruct((b, l, n), BF16) for n in out_w],
        scratch_shapes=[pltpu.VMEM((s5w // LANES, tm, LANES), F32)],
        compiler_params=_cparams(("parallel", "parallel")),
        name="inproj_lat" if latent else "inproj_ctx",
    )(x, nw, sh, sc, w_in, cos_t, sin_t)


def _rope_tables(seq_len):
    rows = seq_len // GRID_W
    pairs = DA_HEAD_DIM // 4
    row = np.repeat(np.arange(rows), GRID_W).astype(np.float64)
    col = np.tile(np.arange(GRID_W), rows).astype(np.float64)
    inv_freq = ROPE_THETA ** (-np.arange(pairs, dtype=np.float64) / pairs)
    ra = row[:, None] * inv_freq[None, :]
    ca = col[:, None] * inv_freq[None, :]
    cos64 = np.concatenate([np.cos(ra), np.cos(ra), np.cos(ca), np.cos(ca)], axis=1)
    sin64 = np.concatenate([-np.sin(ra), np.sin(ra), -np.sin(ca), np.sin(ca)], axis=1)
    return (jnp.asarray(np.tile(cos64, (1, 2)), dtype=F32), jnp.asarray(np.tile(sin64, (1, 2)), dtype=F32))


def _s5_matrices(lam_re, lam_im, log_dt, b_re, b_im, c_re, c_im, d_skip):
    hp = lax.Precision.HIGHEST
    t = S5_CHUNK
    lam_re = jnp.minimum(lam_re.astype(F32), S5_DT_MAX_RE)
    lam_im = lam_im.astype(F32)
    dt = jnp.exp(log_dt.astype(F32))[..., None]
    g, p = lam_re.shape[1:]
    hh = S5_GROUP
    mag = jnp.exp(lam_re * dt)
    lb_re = mag * jnp.cos(lam_im * dt)
    lb_im = mag * jnp.sin(lam_im * dt)
    den = lam_re * lam_re + lam_im * lam_im
    num_re = lb_re - 1.0
    co_re = (num_re * lam_re + lb_im * lam_im) / den
    co_im = (lb_im * lam_re - num_re * lam_im) / den
    br = b_re.astype(F32)
    bi = b_im.astype(F32)
    bb_re = co_re[..., None] * br - co_im[..., None] * bi
    bb_im = co_re[..., None] * bi + co_im[..., None] * br
    j = jnp.arange(t + 1, dtype=F32)[None, None, :, None]
    pmag = jnp.exp(lam_re[:, :, None, :] * dt[:, :, None, :] * j)
    pang = lam_im[:, :, None, :] * dt[:, :, None, :] * j
    pw_re = pmag * jnp.cos(pang)
    pw_im = pmag * jnp.sin(pang)
    cr = c_re.astype(F32)
    ci = c_im.astype(F32)
    cp_re = cr[:, :, None] * pw_re[:, :, :t, None, :] - ci[:, :, None] * pw_im[:, :, :t, None, :]
    cp_im = cr[:, :, None] * pw_im[:, :, :t, None, :] + ci[:, :, None] * pw_re[:, :, :t, None, :]
    taps = (jnp.einsum('dgjhp,dgpi->dgjhi', cp_re, bb_re, precision=hp)
            - jnp.einsum('dgjhp,dgpi->dgjhi', cp_im, bb_im, precision=hp))
    sig = np.arange(t)[:, None, None]
    tau = np.arange(t)[None, :, None]
    lag = np.arange(t)[None, None, :]
    place = np.concatenate([tau - sig == lag, sig - tau == lag], axis=2).astype(np.float32)
    taps2 = jnp.concatenate([taps[0], taps[1]], axis=1)
    skip = d_skip.astype(F32).reshape(g, hh)
    diag = (np.eye(t, dtype=np.float32)[None, :, None, :, None]
            * (jnp.eye(hh, dtype=F32)[None] * skip[:, None, :])[:, None, :, None, :])
    m_mat = (jnp.einsum('stj,gjab->gsbta', place, taps2, precision=hp) + diag).reshape(g, t * hh, t * hh)
    pf_re = pw_re[0][:, :t][:, ::-1]
    pf_im = pw_im[0][:, :t][:, ::-1]
    pr_re = pw_re[1][:, :t]
    pr_im = pw_im[1][:, :t]

    def state_in(pr_, pi_, br_, bi_):
        brt = br_.transpose(0, 2, 1)[:, None]
        bit = bi_.transpose(0, 2, 1)[:, None]
        re = pr_[:, :, None, :] * brt - pi_[:, :, None, :] * bit
        im = pr_[:, :, None, :] * bit + pi_[:, :, None, :] * brt
        return re.reshape(g, t * hh, p), im.reshape(g, t * hh, p)

    wf_re, wf_im = state_in(pf_re, pf_im, bb_re[0], bb_im[0])
    wr_re, wr_im = state_in(pr_re, pr_im, bb_re[1], bb_im[1])
    w_mat = jnp.concatenate([wf_re, wr_re, wf_im, wr_im], axis=2)
    ef_re = pw_re[0][:, 1:t + 1]
    ef_im = pw_im[0][:, 1:t + 1]
    er_re = pw_re[1][:, 1:t + 1][:, ::-1]
    er_im = pw_im[1][:, 1:t + 1][:, ::-1]

    def state_out(e_re, e_im, cr_, ci_):
        g_re = cr_[:, None] * e_re[:, :, None, :] - ci_[:, None] * e_im[:, :, None, :]
        g_im = cr_[:, None] * e_im[:, :, None, :] + ci_[:, None] * e_re[:, :, None, :]
        return (g_re.transpose(0, 3, 1, 2).reshape(g, p, t * hh),
                (-g_im).transpose(0, 3, 1, 2).reshape(g, p, t * hh))

    vf_re, vf_im = state_out(ef_re, ef_im, cr[0], ci[0])
    vr_re, vr_im = state_out(er_re, er_im, cr[1], ci[1])
    v_mat = jnp.concatenate([vf_re, vr_re, vf_im, vr_im], axis=1)
    a_mat = jnp.stack([jnp.concatenate([pw_re[0][:, t], pw_re[1][:, t]], axis=1),
                       jnp.concatenate([pw_im[0][:, t], pw_im[1][:, t]], axis=1)], axis=1)
    return w_mat.astype(BF16), m_mat.astype(BF16), v_mat.astype(BF16), a_mat


def _s5_kernel(uc_ref, ul_ref, w_ref, m_ref, v_ref, a_ref, y_ref, sc_scr, sl_scr, hf_scr, hr_scr, *, n_ctx, n_lat):
    nb = SUBLANES
    half = S5_STATE
    ul = ul_ref[0]
    w = w_ref[0]
    s_c = jnp.dot(uc_ref[0], w, preferred_element_type=F32)
    s_l = jnp.dot(ul, w, preferred_element_type=F32)
    for part in range(2):
        for b in range(nb):
            sc_scr[part, pl.ds(b, n_ctx, stride=nb), :] = s_c[b * n_ctx:(b + 1) * n_ctx, part * LANES:(part + 1) * LANES]
            sl_scr[part, pl.ds(b, n_lat, stride=nb), :] = s_l[b * n_lat:(b + 1) * n_lat, part * LANES:(part + 1) * LANES]
    a = a_ref[0]
    a_re = a[0:1, :]
    a_im = a[1:2, :]
    is_fwd = lax.broadcasted_iota(jnp.int32, (nb, LANES), 1) < half

    def tile(j):
        return pl.ds(pl.multiple_of(j * nb, nb), nb)

    def pick(scr, jf, jr):
        return (jnp.where(is_fwd, scr[0, tile(jf), :], scr[0, tile(jr), :]),
                jnp.where(is_fwd, scr[1, tile(jf), :], scr[1, tile(jr), :]))

    def update(h_re, h_im, s_re, s_im):
        return a_re * h_re - a_im * h_im + s_re, a_re * h_im + a_im * h_re + s_im

    def ctx_step(i, carry):
        return update(*carry, *pick(sc_scr, i, n_ctx - 1 - i))

    def lat_step(c, carry):
        h_re, h_im = carry
        cr = n_lat - 1 - c
        hf_scr[0, tile(c), :] = h_re
        hf_scr[1, tile(c), :] = h_im
        hr_scr[0, tile(cr), :] = h_re
        hr_scr[1, tile(cr), :] = h_im
        return update(h_re, h_im, *pick(sl_scr, c, cr))

    zero = jnp.zeros((nb, LANES), F32)
    carry = lax.fori_loop(0, n_ctx, ctx_step, (zero, zero))
    lax.fori_loop(0, n_lat, lat_step, carry, unroll=2)
    lane = lax.broadcasted_iota(jnp.int32, (n_lat, LANES), 1)

    def entry_states(b):
        rows = pl.ds(b, n_lat, stride=nb)
        return jnp.concatenate([jnp.where(lane < half, hf_scr[part, rows, :], hr_scr[part, rows, :])
                                for part in range(2)], axis=1)

    hin = jnp.concatenate([entry_states(b) for b in range(nb)], axis=0).astype(BF16)
    y = jnp.dot(ul, m_ref[0], preferred_element_type=F32)
    y = y + jnp.dot(hin, v_ref[0], preferred_element_type=F32)
    y_ref[0] = y.astype(BF16)


def _s5(u_ctx, u_lat, w_mat, m_mat, v_mat, a_mat, *, n_ctx, n_lat):
    g, rc, k = u_ctx.shape
    rl = u_lat.shape[1]
    mat_spec = pl.BlockSpec((1, k, k), lambda i: (i, 0, 0))
    return pl.pallas_call(
        functools.partial(_s5_kernel, n_ctx=n_ctx, n_lat=n_lat),
        grid=(g,),
        in_specs=[pl.BlockSpec((1, rc, k), lambda i: (i, 0, 0)),
                  pl.BlockSpec((1, rl, k), lambda i: (i, 0, 0)),
                  mat_spec, mat_spec, mat_spec,
                  pl.BlockSpec((1, 2, LANES), lambda i: (i, 0, 0))],
        out_specs=pl.BlockSpec((1, rl, k), lambda i: (i, 0, 0)),
        out_shape=jax.ShapeDtypeStruct((g, rl, k), BF16),
        scratch_shapes=[pltpu.VMEM((2, rc, LANES), F32), pltpu.VMEM((2, rl, LANES), F32),
                        pltpu.VMEM((2, rl, LANES), F32), pltpu.VMEM((2, rl, LANES), F32)],
        compiler_params=_cparams(("parallel",)),
        name="s5",
    )(u_ctx, u_lat, w_mat, m_mat, v_mat, a_mat)


ATTN_KEY_CHUNK = 256
ATTN_ITEM_ROWS = 128


def _attn_kernel(lam_ref, q_ref, ktc_ref, ktl_ref, vc_ref, vl_ref, sw_ref, o_ref, s_a, s_b, p_a, p_b, *, out_scale):
    lam = lam_ref[0, 0]
    sw = sw_ref[...]
    tr = ATTN_ITEM_ROWS
    lk = ktc_ref.shape[3] + ktl_ref.shape[3]
    assert ktc_ref.shape[3] == ATTN_KEY_CHUNK

    def kt_chunk(h, c):
        return ktc_ref[0, h] if c == 0 else ktl_ref[0, h, :, keys(c - 1)]

    def v_chunk(c, lanes):
        return vc_ref[0, :, lanes] if c == 0 else vl_ref[0, keys(c - 1), lanes]

    kc = ATTN_KEY_CHUNK
    nch = lk // kc
    s_bufs = (s_a, s_b)
    p_bufs = (p_a, p_b)
    lane = lax.broadcasted_iota(jnp.int32, (tr, DA_V_DIM), 1)
    work = [(r, h) for r in range(q_ref.shape[1] // tr) for h in range(DA_HEADS)]
    items = [dict() for _ in work]

    def keys(c):
        return slice(c * kc, (c + 1) * kc)

    def rows(i):
        return slice(work[i][0] * tr, (work[i][0] + 1) * tr)

    def cols(i):
        return slice(work[i][1] * DA_V_DIM, (work[i][1] + 1) * DA_V_DIM)

    def stage_a(i, c):
        it = items[i]
        if c == 0:
            q = q_ref[0, rows(i), cols(i)]
            zero = jnp.zeros_like(q)
            it['q2'] = jnp.concatenate([jnp.where(lane < DA_HEAD_DIM, q, zero),
                                        jnp.where(lane >= DA_HEAD_DIM, q, zero)], axis=0)
        s = jnp.dot(it['q2'], kt_chunk(work[i][1], c), preferred_element_type=F32)
        s_bufs[i % 2][:, keys(c)] = s
        mx = jnp.maximum(s[:, :LANES], s[:, LANES:])
        it['mx'] = mx if c == 0 else jnp.maximum(it['mx'], mx)
        if c == nch - 1:
            it['m'] = jnp.broadcast_to(jnp.max(it['mx'], axis=-1, keepdims=True), (2 * tr, kc))

    def stage_b(i, c):
        it = items[i]
        p = jnp.exp2(s_bufs[i % 2][:, keys(c)] - it['m'])
        ps = p[:, :LANES] + p[:, LANES:]
        it['ls'] = ps if c == 0 else it['ls'] + ps
        p_bufs[i % 2][:, keys(c)] = p.astype(BF16)
        if c == nch - 1:
            l = jnp.sum(it['ls'], axis=-1, keepdims=True)
            it['r1'] = jnp.broadcast_to((1.0 / l[:tr]).astype(BF16), (tr, kc))
            it['r2'] = jnp.broadcast_to((lam / l[tr:]).astype(BF16), (tr, kc))

    def stage_c(i, c):
        it = items[i]
        pb = p_bufs[i % 2]
        pd = pb[0:tr, keys(c)] * it['r1'] - pb[tr:2 * tr, keys(c)] * it['r2']
        part = jnp.dot(pd, v_chunk(c, cols(i)), preferred_element_type=F32)
        it['acc'] = part if c == 0 else it['acc'] + part
        if c == nch - 1:
            o_ref[0, rows(i), cols(i)] = (_rms(it['acc']) * sw * out_scale).astype(BF16)

    n = len(work)
    for slot in range(n + 2):
        for c in range(nch):
            if slot < n:
                stage_a(slot, c)
            if 0 <= slot - 1 < n:
                stage_b(slot - 1, c)
            if 0 <= slot - 2 < n:
                stage_c(slot - 2, c)


def _attn(lam, q, kt_ctx, kt_lat, v_ctx, v_lat, subln, *, tq, out_scale):
    b, l, w = q.shape
    lc = v_ctx.shape[1]
    lk = lc + l
    assert l % ATTN_KEY_CHUNK == 0 and tq % ATTN_ITEM_ROWS == 0
    tr = ATTN_ITEM_ROWS
    return pl.pallas_call(
        functools.partial(_attn_kernel, out_scale=out_scale),
        scratch_shapes=[pltpu.VMEM((2 * tr, lk), F32), pltpu.VMEM((2 * tr, lk), F32),
                        pltpu.VMEM((2 * tr, lk), BF16), pltpu.VMEM((2 * tr, lk), BF16)],
        grid=(b, l // tq),
        in_specs=[pl.BlockSpec(memory_space=pltpu.SMEM),
                  pl.BlockSpec((1, tq, w), lambda i, j: (i, j, 0)),
                  pl.BlockSpec((1, DA_HEADS, DA_V_DIM, lc), lambda i, j: (i, 0, 0, 0)),
                  pl.BlockSpec((1, DA_HEADS, DA_V_DIM, l), lambda i, j: (i, 0, 0, 0)),
                  pl.BlockSpec((1, lc, w), lambda i, j: (i, 0, 0)),
                  pl.BlockSpec((1, l, w), lambda i, j: (i, 0, 0)),
                  pl.BlockSpec((1, DA_V_DIM), lambda i, j: (0, 0))],
        out_specs=pl.BlockSpec((1, tq, w), lambda i, j: (i, j, 0)),
        out_shape=jax.ShapeDtypeStruct((b, l, w), BF16),
        compiler_params=_cparams(("parallel", "parallel")),
        name="attn",
    )(lam, q, kt_ctx, kt_lat, v_ctx, v_lat, subln)


def _gelu_tanh(x):
    return 0.5 * x * (1.0 + jnp.tanh(math.sqrt(2.0 / math.pi) * (x + 0.044715 * (x * x * x))))


def _merge_kernel(x_ref, ys_ref, yb_ref, ga_ref, gb_ref, gm_ref, shf_ref, scf_ref,
                  wglu_ref, wpa_ref, wpb_ref, wout_ref, npost_ref, npre_ref, wr2_ref,
                  x1_ref, h2_ref, aff_ref, afft_ref, ys_scr):
    tm, d = x_ref.shape[1:]
    dk = d // LANES
    nr = tm // 2
    state = [dict(), dict()]

    def rows(hf):
        return slice(hf * nr, (hf + 1) * nr)

    def st_relayout(hf):
        _chunks_to_tokens(ys_ref, ys_scr, hf * nr, nr)

    def st_glu(hf):
        st = state[hf]
        st['ya'] = _gelu_tanh(jnp.concatenate([ys_scr[gt, rows(hf), :] for gt in range(ys_scr.shape[0])], axis=1))
        st['glu'] = jnp.dot(st['ya'].astype(BF16), wglu_ref[...], preferred_element_type=F32)

    def st_proj(hf):
        st = state[hf]
        ya = st['ya'] * _sigmoid(st['glu'])
        st['pa'] = jnp.dot(ya.astype(BF16), wpa_ref[...], preferred_element_type=F32)
        st['pb'] = jnp.dot(yb_ref[0, rows(hf), :], wpb_ref[...], preferred_element_type=F32)

    def st_out(hf):
        st = state[hf]
        mix = ga_ref[0, rows(hf), :].astype(F32) * st['pa'] + gb_ref[0, rows(hf), :].astype(F32) * st['pb']
        st['o'] = jnp.dot(mix.astype(BF16), wout_ref[...], preferred_element_type=F32)

    def st_residual(hf):
        st = state[hf]
        x1 = x_ref[0, rows(hf), :] + gm_ref[0] * (_rms(st['o']) * npost_ref[...])
        x1_ref[0, rows(hf), :] = x1
        h = _rms(x1) * npre_ref[...]
        h = h * (1.0 + scf_ref[0]) + shf_ref[0]
        for k in range(dk):
            h2_ref[0, pl.ds(hf * nr * dk + k, nr, stride=dk), :] = h[:, k * LANES:(k + 1) * LANES]
        st['h'] = h

    def st_router(hf):
        st = state[hf]
        h = st['h']
        h_hi = h.astype(BF16)
        h_lo = (h - h_hi.astype(F32)).astype(BF16)
        st['lg'] = jnp.dot(jnp.concatenate([h_hi, h_lo], axis=1), wr2_ref[...], preferred_element_type=F32)

    def st_softmax(hf):
        lg = state[hf]['lg']
        logits = lg[:, :LANES] + lg[:, LANES:]
        lane = lax.broadcasted_iota(jnp.int32, logits.shape, 1)
        logits = jnp.where(lane < N_EXPERTS, logits, -1e30)
        ex = jnp.exp(logits - jnp.max(logits, axis=-1, keepdims=True))
        aff = ex / jnp.sum(ex, axis=-1, keepdims=True)
        aff_ref[0, rows(hf), :] = aff
        afft_ref[0, :, rows(hf)] = aff.T[:N_EXPERTS, :]

    stages = (st_relayout, st_glu, st_proj, st_out, st_residual, st_router, st_softmax)
    for k in range(len(stages) + 1):
        if k < len(stages):
            stages[k](0)
        if k >= 1:
            stages[k - 1](1)


def _merge(x, ys_rows, yb, ga, gb, gm, shf, scf, wglu, wpa, wpb, wout, npost, npre, wr2, *, tm):
    b, l, d = x.shape
    dk = d // LANES
    groups, _, ck = ys_rows.shape
    nj = l // tm

    def tok(n):
        return pl.BlockSpec((1, tm, n), lambda i, j: (i, j, 0))

    def full(a):
        return pl.BlockSpec(a.shape, lambda i, j: (0,) * a.ndim)

    mod = pl.BlockSpec((1, 1, d), lambda i, j: (i, 0, 0))
    return pl.pallas_call(
        _merge_kernel,
        grid=(b, nj),
        in_specs=[tok(d),
                  pl.BlockSpec((groups, tm // S5_CHUNK, ck), lambda i, j: (0, i * nj + j, 0)),
                  tok(yb.shape[2]), tok(d), tok(d), mod, mod, mod,
                  full(wglu), full(wpa), full(wpb), full(wout), full(npost), full(npre), full(wr2)],
        out_specs=[tok(d),
                   pl.BlockSpec((1, tm * dk, LANES), lambda i, j: (i, j, 0)),
                   tok(LANES),
                   pl.BlockSpec((1, N_EXPERTS, tm), lambda i, j: (i, 0, j))],
        out_shape=[jax.ShapeDtypeStruct((b, l, d), F32),
                   jax.ShapeDtypeStruct((b, l * dk, LANES), F32),
                   jax.ShapeDtypeStruct((b, l, LANES), F32),
                   jax.ShapeDtypeStruct((b, N_EXPERTS, l), F32)],
        scratch_shapes=[pltpu.VMEM((groups * S5_GROUP // LANES, tm, LANES), F32)],
        compiler_params=_cparams(("parallel", "parallel")),
        name="merge",
    )(x, ys_rows, yb, ga, gb, gm, shf, scf, wglu, wpa, wpb, wout, npost, npre, wr2)


def _route_kernel(a_ref, tok_ref, tri_ref, idx_ref, pos_scr, *, cap):
    a = a_ref[0]
    ne, l = a.shape
    capf = float(cap)

    def count_ge(thr):
        return jnp.sum(jnp.where(a >= thr, 1.0, 0.0), axis=1, keepdims=True)

    def narrow(mid, lo, hi):
        mid = jnp.minimum(jnp.maximum(mid, lo), hi)
        ok = count_ge(mid) >= capf
        return jnp.where(ok, mid, lo), jnp.where(ok, hi, mid)

    def geo(_, lohi):
        lo, hi = lohi
        return narrow(jnp.sqrt(jnp.maximum(lo, F32_TINY) * hi), lo, hi)

    def ari(_, lohi):
        lo, hi = lohi
        return narrow(0.5 * lo + 0.5 * hi, lo, hi)

    lohi = (jnp.zeros((ne, 1), F32), jnp.full((ne, 1), 2.0, F32))
    lohi = lax.fori_loop(0, 34, geo, lohi)
    lo, hi = lax.fori_loop(0, 8, ari, lohi)
    gt = a >= hi
    eq = (a >= lo) & jnp.logical_not(gt)
    need = capf - jnp.sum(jnp.where(gt, 1.0, 0.0), axis=1, keepdims=True)

    tri = tri_ref[...]

    def excl_cumsum(mask):
        mb = jnp.where(mask, 1.0, 0.0).astype(BF16)
        off = jnp.zeros((ne, 1), F32)
        outs = []
        for j in range(l // LANES):
            blk = mb[:, j * LANES:(j + 1) * LANES]
            outs.append(jnp.dot(blk, tri, preferred_element_type=F32) + off)
            off = off + jnp.sum(blk.astype(F32), axis=1, keepdims=True)
        return jnp.concatenate(outs, axis=1)

    sel = gt | (eq & (excl_cumsum(eq) < need))
    pos_scr[...] = jnp.where(sel, excl_cumsum(sel), -1.0)

    tc = 512
    slot = lax.broadcasted_iota(jnp.int32, (cap, tc), 0).astype(F32)

    def per_expert(e, _):
        acc = jnp.zeros((cap, LANES), F32)
        for c in range(l // tc):
            pc = pos_scr[pl.ds(e, 1), c * tc:(c + 1) * tc]
            onehot = jnp.where(pc == slot, 1.0, 0.0).astype(BF16)
            acc = acc + jnp.dot(onehot, tok_ref[c * tc:(c + 1) * tc, :], preferred_element_type=F32)
        acc_t = acc.T
        idx = acc_t[0:1, :] * 64.0 + acc_t[1:2, :]
        idx_ref[0, pl.ds(e, 1), :] = idx.astype(jnp.int32)
        return 0

    lax.fori_loop(0, ne, per_expert, 0)


def _route(aff_t, *, cap):
    b, ne, l = aff_t.shape
    tok_np = np.zeros((l, LANES), np.float32)
    tok_np[:, 0] = np.arange(l) >> 6
    tok_np[:, 1] = np.arange(l) & 63
    tok = jnp.asarray(tok_np, dtype=BF16)
    tri = jnp.asarray(np.arange(LANES)[:, None] < np.arange(LANES)[None, :], dtype=BF16)
    return pl.pallas_call(
        functools.partial(_route_kernel, cap=cap),
        grid=(b,),
        in_specs=[pl.BlockSpec((1, ne, l), lambda i: (i, 0, 0)),
                  pl.BlockSpec((l, LANES), lambda i: (0, 0)),
                  pl.BlockSpec((LANES, LANES), lambda i: (0, 0))],
        out_specs=pl.BlockSpec((1, ne, cap), lambda i: (i, 0, 0)),
        out_shape=jax.ShapeDtypeStruct((b, ne, cap), jnp.int32),
        scratch_shapes=[pltpu.VMEM((ne, l), F32)],
        compiler_params=_cparams(("parallel",)),
        name="route",
    )(aff_t, tok, tri)


ROW_UNROLL = 16


def _gather_kernel(idx_ref, h2_ref, aff_ref, xs_ref, g_ref, *, cap, dk):
    def gather(s, _):
        t = idx_ref[0, 0, s]
        xs_ref[0, 0, pl.ds(pl.multiple_of(s * dk, dk), dk), :] = h2_ref[0, pl.ds(pl.multiple_of(t * dk, dk), dk), :]
        g_ref[0, 0, pl.ds(s, 1), :] = aff_ref[0, pl.ds(t, 1), :]
        return 0

    lax.fori_loop(0, cap, gather, 0, unroll=ROW_UNROLL)


def _moe_gather(idx, h2, aff, *, d):
    b, ne, cap = idx.shape
    l = aff.shape[1]
    dk = d // LANES
    return pl.pallas_call(
        functools.partial(_gather_kernel, cap=cap, dk=dk),
        grid=(b, ne),
        in_specs=[pl.BlockSpec((1, 1, cap), lambda i, e: (i * ne + e, 0, 0), memory_space=pltpu.SMEM),
                  pl.BlockSpec((1, l * dk, LANES), lambda i, e: (i, 0, 0), pipeline_mode=pl.Buffered(1)),
                  pl.BlockSpec((1, l, LANES), lambda i, e: (i, 0, 0))],
        out_specs=[pl.BlockSpec((1, 1, cap * dk, LANES), lambda i, e: (i, e, 0, 0)),
                   pl.BlockSpec((1, 1, cap, LANES), lambda i, e: (i, e, 0, 0))],
        out_shape=[jax.ShapeDtypeStruct((b, ne, cap * dk, LANES), F32),
                   jax.ShapeDtypeStruct((b, ne, cap, LANES), F32)],
        compiler_params=_cparams(("arbitrary", "arbitrary")),
        name="moe_gather",
    )(idx.reshape(b * ne, 1, cap), h2, aff)


def _expert_kernel(xs_ref, g_ref, wg_ref, wu_ref, wd_ref, ys_ref, *, nsplit):
    e = pl.program_id(0)
    cap = g_ref.shape[2]
    dk = xs_ref.shape[2] // cap
    nh = cap // 2
    tf = wg_ref.shape[2] // nsplit
    state = [dict(), dict()]

    def st_load(hf):
        state[hf]['xs'] = jnp.concatenate(
            [xs_ref[0, 0, pl.ds(hf * nh * dk + k, nh, stride=dk), :] for k in range(dk)], axis=1).astype(BF16)

    def st_ffn(hf):
        xs = state[hf]['xs']
        y = None
        for j in range(nsplit):
            cols = slice(j * tf, (j + 1) * tf)
            gg = jnp.dot(xs, wg_ref[0, :, cols], preferred_element_type=F32)
            uu = jnp.dot(xs, wu_ref[0, :, cols], preferred_element_type=F32)
            hid = (gg * _sigmoid(gg) * uu).astype(BF16)
            part = jnp.dot(hid, wd_ref[0, cols, :], preferred_element_type=F32)
            y = part if y is None else y + part
        state[hf]['y'] = y

    def st_store(hf):
        g = g_ref[0, 0, hf * nh:(hf + 1) * nh, :]
        lane = lax.broadcasted_iota(jnp.int32, g.shape, 1)
        gate = jnp.sum(jnp.where(lane == e, g, 0.0), axis=1, keepdims=True)
        yg = state[hf]['y'] * gate
        for k in range(dk):
            ys_ref[0, 0, pl.ds(hf * nh * dk + k, nh, stride=dk), :] = yg[:, k * LANES:(k + 1) * LANES]

    stages = (st_load, st_ffn, st_store)
    for step in range(len(stages) + 1):
        if step < len(stages):
            stages[step](0)
        if step >= 1:
            stages[step - 1](1)


def _moe_experts(xs, gates, wg, wu, wd):
    b, ne, rows, _ = xs.shape
    cap = gates.shape[2]
    d, fdim = wg.shape[1:]
    tok = pl.BlockSpec((1, 1, rows, LANES), lambda e, i: (i, e, 0, 0))
    return pl.pallas_call(
        functools.partial(_expert_kernel, nsplit=2),
        grid=(ne, b),
        in_specs=[tok,
                  pl.BlockSpec((1, 1, cap, LANES), lambda e, i: (i, e, 0, 0)),
                  pl.BlockSpec((1, d, fdim), lambda e, i: (e, 0, 0)),
                  pl.BlockSpec((1, d, fdim), lambda e, i: (e, 0, 0)),
                  pl.BlockSpec((1, fdim, d), lambda e, i: (e, 0, 0))],
        out_specs=tok,
        out_shape=jax.ShapeDtypeStruct((b, ne, rows, LANES), F32),
        compiler_params=_cparams(("arbitrary", "arbitrary")),
        name="moe_experts",
    )(xs, gates, wg, wu, wd)


def _scatter_kernel(idx_ref, ys_ref, x1_ref, gf_ref, nw_ref, o_ref, f_scr, *, cap, ne):
    e = pl.program_id(1)
    dk = ys_ref.shape[2] // cap
    tm = x1_ref.shape[1]

    @pl.when(e == 0)
    def _():
        f_scr[...] = jnp.zeros_like(f_scr)

    @pl.when(e < ne)
    def _():
        def scatter(s0, _):
            rows = []
            vals = []
            for i in range(ROW_UNROLL):
                s = s0 * ROW_UNROLL + i
                r = pl.multiple_of(idx_ref[0, 0, s] * dk, dk)
                rows.append(r)
                vals.append(f_scr[pl.ds(r, dk), :] + ys_ref[0, 0, pl.ds(pl.multiple_of(s * dk, dk), dk), :])
            for r, v in zip(rows, vals):
                f_scr[pl.ds(r, dk), :] = v
            return 0

        lax.fori_loop(0, cap // ROW_UNROLL, scatter, 0)

    @pl.when(e >= ne)
    def _():
        base = (e - ne) * (tm * dk)
        f = jnp.concatenate([f_scr[pl.ds(base + k, tm, stride=dk), :] for k in range(dk)], axis=1)
        o_ref[0] = x1_ref[0] + gf_ref[0] * (_rms(f) * nw_ref[...])


def _moe_scatter_final(idx, ys, x1, gf, nw, *, tm):
    b, ne, cap = idx.shape
    l, d = x1.shape[1:]
    dk = d // LANES
    nt = l // tm

    def expert(e):
        return jnp.minimum(e, ne - 1)

    def tile(e):
        return jnp.maximum(e - ne, 0)

    return pl.pallas_call(
        functools.partial(_scatter_kernel, cap=cap, ne=ne),
        grid=(b, ne + nt),
        in_specs=[pl.BlockSpec((1, 1, cap), lambda i, e: (i * ne + expert(e), 0, 0), memory_space=pltpu.SMEM),
                  pl.BlockSpec((1, 1, cap * dk, LANES), lambda i, e: (i, expert(e), 0, 0)),
                  pl.BlockSpec((1, tm, d), lambda i, e: (i, tile(e), 0)),
                  pl.BlockSpec((1, 1, d), lambda i, e: (i, 0, 0)),
                  pl.BlockSpec((1, d), lambda i, e: (0, 0))],
        out_specs=pl.BlockSpec((1, tm, d), lambda i, e: (i, tile(e), 0)),
        out_shape=jax.ShapeDtypeStruct((b, l, d), F32),
        scratch_shapes=[pltpu.VMEM((l * dk, LANES), F32)],
        compiler_params=_cparams(("arbitrary", "arbitrary")),
        name="moe_scatter_final",
    )(idx.reshape(b * ne, 1, cap), ys, x1, gf, nw)


def kernel(x, c, ctx, c_ctx, w_ada, b_ada, norm_pre_mix, norm_post_mix, norm_pre_ffn, norm_post_ffn, w_in, s5_lam_re, s5_lam_im, s5_log_dt, s5_b_re, s5_b_im, s5_c_re, s5_c_im, s5_d, w_glu, da_lambda, da_subln, w_proj_a, w_proj_b, w_out, w_router, w_exp_gate, w_exp_up, w_exp_down):
    depth = w_ada.shape[0]
    assert depth == 1, "single trunk layer: the context stream's outputs are never consumed"
    b, l, d = x.shape
    lc = ctx.shape[1]
    assert b == SUBLANES and l % (S5_CHUNK * 32) == 0 and lc % (S5_CHUNK * 16) == 0
    s5w = s5_d.shape[1]
    qkw = DA_HEADS * 2 * DA_HEAD_DIM
    vw = DA_HEADS * DA_V_DIM
    widths = (s5w, qkw, vw, d)
    lam_init = 0.8 - 0.6 * math.exp(-0.3 * 0)

    c_all = jnp.zeros((2 * SUBLANES, d), F32).at[:b].set(c).at[b].set(c_ctx)
    mod = _ada(c_all, w_ada[0], b_ada[0])
    sh_m, sc_m, g_m, sh_f, sc_f, g_f = [mod[:b, i * d:(i + 1) * d].reshape(b, 1, d) for i in range(6)]
    csh_m = mod[b:b + 1, 0:d].reshape(1, 1, d)
    csc_m = mod[b:b + 1, d:2 * d].reshape(1, 1, d)

    w_in_b = w_in[0].astype(BF16)
    cos_t, sin_t = _rope_tables(l)
    npm = norm_pre_mix[0].reshape(1, d)
    u, q, k, v, ga, gb = _inproj(x, npm, sh_m, sc_m, w_in_b, cos_t, sin_t, latent=True, widths=widths, tm=1024)
    uc, kc, vc = _inproj(ctx, npm, csh_m, csc_m, w_in_b, cos_t[:lc], sin_t[:lc], latent=False, widths=widths, tm=lc)

    w_mat, m_mat, v_mat, a_mat = _s5_matrices(s5_lam_re[0], s5_lam_im[0], s5_log_dt[0], s5_b_re[0], s5_b_im[0],
                                              s5_c_re[0], s5_c_im[0], s5_d[0])
    ys_rows = _s5(uc, u, w_mat, m_mat, v_mat, a_mat, n_ctx=lc // S5_CHUNK, n_lat=l // S5_CHUNK)

    lq1, lk1, lq2, lk2 = da_lambda[0].astype(F32)
    lam = jnp.exp(jnp.sum(lq1 * lk1)) - jnp.exp(jnp.sum(lq2 * lk2)) + lam_init
    kt_ctx = kc.reshape(b, lc, DA_HEADS, DA_V_DIM).transpose(0, 2, 3, 1)
    kt_lat = k.reshape(b, l, DA_HEADS, DA_V_DIM).transpose(0, 2, 3, 1)
    yb = _attn(lam.reshape(1, 1), q, kt_ctx, kt_lat, vc, v, da_subln[0].reshape(1, DA_V_DIM),
               tq=256, out_scale=1.0 - lam_init)

    wr = jnp.zeros((d, LANES), F32).at[:, :N_EXPERTS].set(w_router[0])
    wr_hi = wr.astype(BF16)
    wr_lo = (wr - wr_hi.astype(F32)).astype(BF16)
    wr2 = jnp.concatenate([jnp.concatenate([wr_hi, wr_lo], axis=1),
                           jnp.concatenate([wr_hi, jnp.zeros_like(wr_hi)], axis=1)], axis=0)
    x1, h2, aff, aff_t = _merge(x, ys_rows, yb, ga, gb, g_m, sh_f, sc_f,
                                w_glu[0].astype(BF16), w_proj_a[0].astype(BF16), w_proj_b[0].astype(BF16),
                                w_out[0].astype(BF16), norm_post_mix[0].reshape(1, d), norm_pre_ffn[0].reshape(1, d),
                                wr2, tm=512)

    cap = EC_CAPACITY * l // N_EXPERTS
    idx = _route(aff_t, cap=cap)
    xs, gates = _moe_gather(idx, h2, aff, d=d)
    ys = _moe_experts(xs, gates, w_exp_gate[0].astype(BF16), w_exp_up[0].astype(BF16), w_exp_down[0].astype(BF16))
    return _moe_scatter_final(idx, ys, x1, g_f, norm_post_ffn[0].reshape(1, d), tm=512)
```
